```python
import math
import jax
import jax.numpy as jnp
from jax import lax
import numpy as np

D_MODEL = 4096
BATCH = 16
SEQ = 256
DEPTH = 2
DEC_BATCH = 2
DEC_SEQ = 1024
PAST_LEN = 256

GRID_W = 64
W_BR = D_MODEL // 2
S5_GROUP = 16
S5_GROUPS = W_BR // S5_GROUP
S5_STATE = 64
LRU_BLOCKS = 16
LRU_BW = W_BR // LRU_BLOCKS
LRU_C = 8.0
CONV_W = 4
HG_DK = 128
HG_HEADS = W_BR // HG_DK
HG_DV = W_BR // HG_HEADS
HG_CHUNK = 16
N_BRANCH = 3
IN_COLS = 9 * W_BR + N_BRANCH * D_MODEL
DN_ALPHA = (2 * DEPTH) ** 0.25
DN_BETA = (8 * DEPTH) ** -0.25
LN_EPS = 1e-5
RMS_EPS = 1e-6

kernel_name = "hybrid_s5_rglru_hgrn2_diffusion_step"


def layer_norm(x):
    xf = x.astype(jnp.float32)
    mu = jnp.mean(xf, axis=-1, keepdims=True)
    var = jnp.mean(jnp.square(xf - mu), axis=-1, keepdims=True)
    return ((xf - mu) * lax.rsqrt(var + LN_EPS)).astype(x.dtype)


def layer_norm_affine(x, g, b):
    xf = x.astype(jnp.float32)
    mu = jnp.mean(xf, axis=-1, keepdims=True)
    var = jnp.mean(jnp.square(xf - mu), axis=-1, keepdims=True)
    y = (xf - mu) * lax.rsqrt(var + LN_EPS) * g.astype(jnp.float32) + b.astype(jnp.float32)
    return y.astype(x.dtype)


def _real_combine(left, right):
    a_l, b_l = left
    a_r, b_r = right
    return a_l * a_r, a_r * b_l + b_r


def real_linear_scan(a, b, h0, reverse):
    if reverse:
        a = jnp.flip(a, axis=1)
        b = jnp.flip(b, axis=1)
    b = b.at[:, 0].add(a[:, 0] * h0)
    _, h = lax.associative_scan(_real_combine, (a, b), axis=1)
    h_final = h[:, -1]
    if reverse:
        h = jnp.flip(h, axis=1)
    return h, h_final


def _complex_combine(left, right):
    ar_l, ai_l, br_l, bi_l = left
    ar_r, ai_r, br_r, bi_r = right
    return (ar_r * ar_l - ai_r * ai_l,
            ar_r * ai_l + ai_r * ar_l,
            ar_r * br_l - ai_r * bi_l + br_r,
            ar_r * bi_l + ai_r * br_l + bi_r)


def complex_linear_scan(lam_re, lam_im, b_re, b_im, h0_re, h0_im, reverse):
    if reverse:
        b_re = jnp.flip(b_re, axis=1)
        b_im = jnp.flip(b_im, axis=1)
    b_re = b_re.at[:, 0].add(lam_re * h0_re - lam_im * h0_im)
    b_im = b_im.at[:, 0].add(lam_re * h0_im + lam_im * h0_re)
    a_re = jnp.broadcast_to(lam_re, b_re.shape)
    a_im = jnp.broadcast_to(lam_im, b_im.shape)
    _, _, h_re, h_im = lax.associative_scan(_complex_combine, (a_re, a_im, b_re, b_im), axis=1)
    last_re, last_im = h_re[:, -1], h_im[:, -1]
    if reverse:
        h_re = jnp.flip(h_re, axis=1)
        h_im = jnp.flip(h_im, axis=1)
    return h_re, h_im, last_re, last_im


def s5_mixer(u, a_re, a_im, log_dt, b_re, b_im, c_re, c_im, d_skip, w_glu, b_glu, h0):
    f32 = jnp.float32
    bsz, seqlen, _ = u.shape
    uf = u.astype(f32)
    ug = uf.reshape(bsz, seqlen, S5_GROUPS, S5_GROUP)
    bu_re = jnp.einsum('blgc,gpc->blgp', ug, b_re.astype(f32))
    bu_im = jnp.einsum('blgc,gpc->blgp', ug, b_im.astype(f32))
    h0f = h0.astype(f32)
    y = uf * d_skip.astype(f32)
    finals = []
    for dr in range(2):
        ar = jnp.minimum(a_re[dr].astype(f32), -1e-4)
        ai = a_im[dr].astype(f32)
        dt = jnp.exp(log_dt[dr].astype(f32))[:, None]
        mag = jnp.exp(dt * ar)
        lam_re = mag * jnp.cos(dt * ai)
        lam_im = mag * jnp.sin(dt * ai)
        den = ar * ar + ai * ai
        z_re = ((lam_re - 1.0) * ar + lam_im * ai) / den
        z_im = (lam_im * ar - (lam_re - 1.0) * ai) / den
        x_re = z_re * bu_re - z_im * bu_im
        x_im = z_re * bu_im + z_im * bu_re
        h_re, h_im, last_re, last_im = complex_linear_scan(
            lam_re, lam_im, x_re, x_im, h0f[:, dr, :, :, 0], h0f[:, dr, :, :, 1], dr == 1)
        y_dir = (jnp.einsum('blgp,gcp->blgc', h_re, c_re.astype(f32))
                 - jnp.einsum('blgp,gcp->blgc', h_im, c_im.astype(f32)))
        y = y + y_dir.reshape(bsz, seqlen, W_BR)
        finals.append(jnp.stack([last_re, last_im], axis=-1))
    y = jax.nn.gelu(y)
    y = y * jax.nn.sigmoid(y @ w_glu.astype(f32) + b_glu.astype(f32))
    return y.astype(u.dtype), jnp.stack(finals, axis=1)


def dwconv_centred(x, w, b, axis):
    n = x.shape[axis]
    pad = [(0, 0)] * x.ndim
    pad[axis] = (CONV_W // 2, CONV_W - 1 - CONV_W // 2)
    xp = jnp.pad(x, pad)
    out = b
    for k in range(CONV_W):
        out = out + w[k] * lax.slice_in_dim(xp, k, k + n, axis=axis)
    return out


def rglru_mixer(xb, conv_w, conv_b, w_a, b_a, w_x, b_x, lam, h0, rows):
    f32 = jnp.float32
    bsz, seqlen, _ = xb.shape
    xf = xb.astype(f32)
    cw, cb = conv_w.astype(f32), conv_b.astype(f32)
    if rows is None:
        xc = dwconv_centred(xf, cw, cb, axis=1)
    else:
        xg = xf.reshape(bsz, rows, GRID_W, W_BR)
        xc = dwconv_centred(xg, cw, cb, axis=2).reshape(bsz, seqlen, W_BR)
    xblk = xc.reshape(bsz, seqlen, LRU_BLOCKS, LRU_BW)
    h0f = h0.astype(f32)
    outs, finals = [], []
    for dr in range(2):
        r = jax.nn.sigmoid(jnp.einsum('blnc,ncd->blnd', xblk, w_a[dr].astype(f32)).reshape(bsz, seqlen, W_BR)
                           + b_a[dr].astype(f32))
        ig = jax.nn.sigmoid(jnp.einsum('blnc,ncd->blnd', xblk, w_x[dr].astype(f32)).reshape(bsz, seqlen, W_BR)
                            + b_x[dr].astype(f32))
        log_a = -LRU_C * r * jax.nn.softplus(-lam[dr].astype(f32))
        a = jnp.exp(log_a)
        mult = jnp.sqrt(-jnp.expm1(2.0 * log_a))
        h, last = real_linear_scan(a, mult * (ig * xc), h0f[:, dr], dr == 1)
        outs.append(h)
        finals.append(last)
    y = outs[0] + outs[1]
    return y.astype(xb.dtype), jnp.stack(finals, axis=1)


def chunk_gated_linear(q, k, v, log_f, h0, reverse):
    if reverse:
        q, k, v, log_f = (jnp.flip(t, axis=1) for t in (q, k, v, log_f))
    bsz, seqlen, nh, dk = q.shape
    dv = v.shape[-1]
    nch = seqlen // HG_CHUNK

    def to_chunks(t):
        return jnp.moveaxis(t.reshape(bsz, nch, HG_CHUNK, nh, t.shape[-1]), 1, 0)

    cum = jnp.cumsum(log_f.reshape(bsz, nch, HG_CHUNK, nh, dk), axis=2)
    cum = jnp.moveaxis(cum, 1, 0)
    causal = jnp.tril(jnp.ones((HG_CHUNK, HG_CHUNK), dtype=bool))[None, :, :, None, None]

    def body(state, xs):
        qn, kn, vn, bn = xs
        b_last = bn[:, -1]
        diff = bn[:, :, None] - bn[:, None, :]
        dec = jnp.where(causal, jnp.exp(jnp.minimum(diff, 0.0)), 0.0)
        scores = jnp.einsum('btshk,bshk->bhts', qn[:, :, None] * dec, kn)
        o = (jnp.einsum('bhts,bshv->bthv', scores, vn)
             + jnp.einsum('bthk,bhkv->bthv', qn * jnp.exp(bn), state))
        k_dec = kn * jnp.exp(b_last[:, None] - bn)
        state = state * jnp.exp(b_last)[..., None] + jnp.einsum('bshk,bshv->bhkv', k_dec, vn)
        return state, o

    final, o = lax.scan(body, h0, (to_chunks(q), to_chunks(k), to_chunks(v), cum))
    o = jnp.moveaxis(o, 0, 1).reshape(bsz, seqlen, nh, dv)
    if reverse:
        o = jnp.flip(o, axis=1)
    return o, final


def hgrn2_mixer(q, f_fwd, f_bwd, v, lb, norm_w, h0):
    f32 = jnp.float32
    bsz, seqlen, _ = q.shape
    qf = jax.nn.silu(q.astype(f32)).reshape(bsz, seqlen, HG_HEADS, HG_DK)
    vf = v.astype(f32).reshape(bsz, seqlen, HG_HEADS, HG_DV)
    h0f = h0.astype(f32)
    outs, finals = [], []
    for dr, f in enumerate((f_fwd, f_bwd)):
        lbd = lb[dr].astype(f32)
        g = lbd + (1.0 - lbd) * jax.nn.sigmoid(f.astype(f32))
        log_g = jnp.log(g).reshape(bsz, seqlen, HG_HEADS, HG_DK)
        kf = (1.0 - g).reshape(bsz, seqlen, HG_HEADS, HG_DK)
        o, last = chunk_gated_linear(qf, kf, vf, log_g, h0f[:, dr], dr == 1)
        outs.append(o)
        finals.append(last)
    o = outs[0] + outs[1]
    o = o * lax.rsqrt(jnp.mean(jnp.square(o), axis=-1, keepdims=True) + RMS_EPS)
    o = o * norm_w.astype(f32).reshape(HG_HEADS, HG_DV)
    return o.reshape(bsz, seqlen, W_BR).astype(q.dtype), jnp.stack(finals, axis=1)


def trunk_layer(x, mod, h0_s5, h0_lru, h0_hg, rows, lb,
                w_in, b_in, s5_a_re, s5_a_im, s5_log_dt, s5_b_re, s5_b_im, s5_c_re, s5_c_im,
                s5_d, s5_w_glu, s5_b_glu, lru_conv_w, lru_conv_b, lru_w_a, lru_b_a, lru_w_x, lru_b_x,
                lru_lambda, hg_norm_w, w_br, w_out, b_out, ln_g, ln_b):
    shift, scale, gate = jnp.split(mod, 3, axis=-1)
    h = layer_norm(x) * (1.0 + scale) + shift
    proj = jnp.einsum('bld,dc->blc', h, w_in) + b_in
    sizes = [W_BR] * 9 + [D_MODEL] * N_BRANCH
    idx, acc = [], 0
    for s in sizes[:-1]:
        acc += s
        idx.append(acc)
    (u_a, g_a, x_b, g_b, q_c, f_c_fwd, f_c_bwd, i_c, g_c,
     m_a, m_b, m_c) = jnp.split(proj, idx, axis=-1)
    y_a, st_s5 = s5_mixer(u_a, s5_a_re, s5_a_im, s5_log_dt, s5_b_re, s5_b_im, s5_c_re, s5_c_im,
                          s5_d, s5_w_glu, s5_b_glu, h0_s5)
    y_b, st_lru = rglru_mixer(x_b, lru_conv_w, lru_conv_b, lru_w_a, lru_b_a, lru_w_x, lru_b_x,
                              lru_lambda, h0_lru, rows)
    y_c, st_hg = hgrn2_mixer(q_c, f_c_fwd, f_c_bwd, i_c, lb, hg_norm_w, h0_hg)
    y_a = y_a * jax.nn.silu(g_a)
    y_b = y_b * jax.nn.silu(g_b)
    y_c = y_c * jax.nn.silu(g_c)
    merged = (jax.nn.sigmoid(m_a) * jnp.einsum('blw,wd->bld', y_a, w_br[0])
              + jax.nn.sigmoid(m_b) * jnp.einsum('blw,wd->bld', y_b, w_br[1])
              + jax.nn.sigmoid(m_c) * jnp.einsum('blw,wd->bld', y_c, w_br[2]))
    out = jnp.einsum('bld,de->ble', merged, w_out) + b_out
    x_new = layer_norm_affine(DN_ALPHA * x + gate * out, ln_g, ln_b)
    return x_new, st_s5, st_lru, st_hg


def setup_inputs(seed: int = 0) -> dict:
    key = jax.random.key(seed)
    ks = iter(jax.random.split(key, 48))
    f32 = jnp.float32

    def nrm(shape, scale):
        return jax.random.normal(next(ks), shape, f32) * scale

    a_im_base = jnp.pi * jnp.arange(S5_STATE, dtype=f32)
    lru_u = jax.random.uniform(next(ks), (DEPTH, 2, W_BR), f32, minval=0.9, maxval=0.999)
    lru_a = lru_u ** (1.0 / LRU_C)
    return {
        "x_prompt": nrm((BATCH, SEQ, D_MODEL), 1.0),
        "x_sample": nrm((DEC_BATCH, DEC_SEQ, D_MODEL), 1.0),
        "state_s5": nrm((DEC_BATCH, DEPTH, 2, S5_GROUPS, S5_STATE, 2), 0.5),
        "state_lru": nrm((DEC_BATCH, DEPTH, 2, W_BR), 0.5),
        "state_hgrn": nrm((DEC_BATCH, DEPTH, 2, HG_HEADS, HG_DK, HG_DV), 0.5),
        "c": nrm((DEC_BATCH, D_MODEL), 1.0),
        "c_ctx": nrm((D_MODEL,), 1.0),
        "w_ada": nrm((DEPTH, D_MODEL, 3 * D_MODEL), D_MODEL ** -0.5),
        "b_ada": nrm((DEPTH, 3 * D_MODEL), 0.01),
        "w_in": nrm((DEPTH, D_MODEL, IN_COLS), D_MODEL ** -0.5),
        "b_in": nrm((DEPTH, IN_COLS), 0.01),
        "s5_a_re": -0.5 + nrm((DEPTH, 2, S5_GROUPS, S5_STATE), 0.01),
        "s5_a_im": a_im_base + nrm((DEPTH, 2, S5_GROUPS, S5_STATE), 0.01),
        "s5_log_dt": jax.random.uniform(next(ks), (DEPTH, 2, S5_GROUPS), f32,
                                        minval=math.log(1e-3), maxval=math.log(1e-1)),
        "s5_b_re": nrm((DEPTH, S5_GROUPS, S5_STATE, S5_GROUP), (2.0 * S5_GROUP) ** -0.5),
        "s5_b_im": nrm((DEPTH, S5_GROUPS, S5_STATE, S5_GROUP), (2.0 * S5_GROUP) ** -0.5),
        "s5_c_re": nrm((DEPTH, S5_GROUPS, S5_GROUP, S5_STATE), (2.0 * S5_STATE) ** -0.5),
        "s5_c_im": nrm((DEPTH, S5_GROUPS, S5_GROUP, S5_STATE), (2.0 * S5_STATE) ** -0.5),
        "s5_d": nrm((DEPTH, W_BR), 1.0),
        "s5_w_glu": nrm((DEPTH, W_BR, W_BR), W_BR ** -0.5),
        "s5_b_glu": nrm((DEPTH, W_BR), 0.01),
        "lru_conv_w": nrm((DEPTH, CONV_W, W_BR), CONV_W ** -0.5),
        "lru_conv_b": nrm((DEPTH, W_BR), 0.01),
        "lru_w_a": nrm((DEPTH, 2, LRU_BLOCKS, LRU_BW, LRU_BW), LRU_BW ** -0.5),
        "lru_b_a": nrm((DEPTH, 2, W_BR), 0.01),
        "lru_w_x": nrm((DEPTH, 2, LRU_BLOCKS, LRU_BW, LRU_BW), LRU_BW ** -0.5),
        "lru_b_x": nrm((DEPTH, 2, W_BR), 0.01),
        "lru_lambda": jnp.log(lru_a) - jnp.log1p(-lru_a),
        "hg_lb": nrm((DEPTH, 2, W_BR), 0.5),
        "hg_norm_w": 1.0 + nrm((DEPTH, W_BR), 0.01),
        "w_br": nrm((DEPTH, N_BRANCH, W_BR, D_MODEL), W_BR ** -0.5 * DN_BETA),
        "w_out": nrm((DEPTH, D_MODEL, D_MODEL), D_MODEL ** -0.5 * DN_BETA),
        "b_out": nrm((DEPTH, D_MODEL), 0.01),
        "ln_g": 1.0 + nrm((DEPTH, D_MODEL), 0.01),
        "ln_b": nrm((DEPTH, D_MODEL), 0.01),
    }


def reference(x_prompt, x_sample, state_s5, state_lru, state_hgrn, c, c_ctx,
              w_ada, b_ada, w_in, b_in, s5_a_re, s5_a_im, s5_log_dt, s5_b_re, s5_b_im,
              s5_c_re, s5_c_im, s5_d, s5_w_glu, s5_b_glu, lru_conv_w, lru_conv_b,
              lru_w_a, lru_b_a, lru_w_x, lru_b_x, lru_lambda, hg_lb, hg_norm_w,
              w_br, w_out, b_out, ln_g, ln_b):
    f32 = jnp.float32
    lb_soft = jax.nn.softmax(hg_lb.astype(f32), axis=0)
    lb_all = jnp.cumsum(lb_soft, axis=0) - lb_soft[0]

    def run_layer(l, x, mod, h0_s5, h0_lru, h0_hg, rows):
        return trunk_layer(x, mod, h0_s5, h0_lru, h0_hg, rows, lb_all[l],
                           w_in[l], b_in[l], s5_a_re[l], s5_a_im[l], s5_log_dt[l],
                           s5_b_re[l], s5_b_im[l], s5_c_re[l], s5_c_im[l], s5_d[l],
                           s5_w_glu[l], s5_b_glu[l], lru_conv_w[l], lru_conv_b[l],
                           lru_w_a[l], lru_b_a[l], lru_w_x[l], lru_b_x[l], lru_lambda[l],
                           hg_norm_w[l], w_br[l], w_out[l], b_out[l], ln_g[l], ln_b[l])

    nb = x_prompt.shape[0]
    z_s5 = jnp.zeros((nb, 2, S5_GROUPS, S5_STATE, 2), f32)
    z_lru = jnp.zeros((nb, 2, W_BR), f32)
    z_hg = jnp.zeros((nb, 2, HG_HEADS, HG_DK, HG_DV), f32)
    silu_ctx = jax.nn.silu(c_ctx)
    x = x_prompt
    st_s5_list, st_lru_list, st_hg_list = [], [], []
    for l in range(DEPTH):
        mod_ctx = (silu_ctx @ w_ada[l] + b_ada[l])[None, None, :]
        x, st_s5, st_lru, st_hg = run_layer(l, x, mod_ctx, z_s5, z_lru, z_hg, None)
        st_s5_list.append(st_s5)
        st_lru_list.append(st_lru)
        st_hg_list.append(st_hg)
    y_prompt = x
    new_state_s5 = jnp.stack(st_s5_list, axis=1).astype(x_prompt.dtype)
    new_state_lru = jnp.stack(st_lru_list, axis=1).astype(x_prompt.dtype)
    new_state_hgrn = jnp.stack(st_hg_list, axis=1).astype(x_prompt.dtype)

    rows = x_sample.shape[1] // GRID_W
    silu_c = jax.nn.silu(c)
    x = x_sample
    for l in range(DEPTH):
        mod_lat = (silu_c @ w_ada[l] + b_ada[l])[:, None, :]
        x, _, _, _ = run_layer(l, x, mod_lat, state_s5[:, l], state_lru[:, l], state_hgrn[:, l], rows)
    y_sample = x

    return (y_prompt, y_sample, new_state_s5, new_state_lru, new_state_hgrn)
```

```python
import functools
import math

import jax
import jax.numpy as jnp
from jax import lax
from jax.experimental import pallas as pl
from jax.experimental.pallas import tpu as pltpu

F32 = jnp.float32
BF16 = jnp.bfloat16

LANE = 128
SUBLANE = 8
VMEM_LIMIT = 56 * 1024 * 1024

D_MODEL = 4096
DEPTH = 2
BATCH, SEQ = 16, 256
DEC_BATCH, DEC_SEQ = 2, 1024
GRID_W = 64
W_BR = D_MODEL // 2
S5_GROUP = 16
S5_STATE = 64
LRU_C = 8.0
CONV_W = 4
HG_DK = 128
HG_CHUNK = 16
N_BRANCH = 3
IN_COLS = 9 * W_BR + N_BRANCH * D_MODEL
DN_ALPHA = (2 * DEPTH) ** 0.25
LN_EPS = 1e-5
RMS_EPS = 1e-6

N_CTX = BATCH * SEQ
N_LAT = DEC_BATCH * DEC_SEQ
N_TOK = N_CTX + N_LAT
LAT_SEG = DEC_SEQ // SEQ
LAT_ROWS = LAT_SEG * DEC_BATCH

C_UA, C_GA, C_XB, C_GB, C_QC, C_FF, C_FB, C_IC, C_GC = (i * W_BR for i in range(9))
C_MA = 9 * W_BR

TM = 1024
TN = 512
TM_MERGE = 512
TM_LN = 256
HG_RB = 256
S5_PAIRS = 4


def _cparams(*sem):
    return pltpu.CompilerParams(dimension_semantics=sem, vmem_limit_bytes=VMEM_LIMIT)


def _sigmoid(x):
    return jax.nn.sigmoid(x)


def _silu(x):
    return x * jax.nn.sigmoid(x)


def _mods_kernel(c_ref, w_ref, b_ref, o_ref):
    c = c_ref[...]
    s = _silu(c).astype(BF16)
    w = w_ref[...].astype(BF16)
    o_ref[...] = jnp.dot(s, w, preferred_element_type=F32) + b_ref[...]


def _mods_call(cvec, w_ada, b_ada):
    tn = TN
    return pl.pallas_call(
        _mods_kernel,
        out_shape=jax.ShapeDtypeStruct((DEPTH, SUBLANE, 3 * D_MODEL), F32),
        grid=(DEPTH, 3 * D_MODEL // tn),
        in_specs=[
            pl.BlockSpec((SUBLANE, D_MODEL), lambda l, j: (0, 0)),
            pl.BlockSpec((None, D_MODEL, tn), lambda l, j: (l, 0, j)),
            pl.BlockSpec((None, 1, tn), lambda l, j: (l, 0, j)),
        ],
        out_specs=pl.BlockSpec((None, SUBLANE, tn), lambda l, j: (l, 0, j)),
        compiler_params=_cparams("arbitrary", "arbitrary"),
        name="adaln_mods",
    )(cvec, w_ada, b_ada.reshape(DEPTH, 1, 3 * D_MODEL))


def _mod_row_ln(i):
    n_ctx_tiles = N_CTX // TM_LN
    return jnp.where(i < n_ctx_tiles, 0, 1 + (i - n_ctx_tiles) // (DEC_SEQ // TM_LN))


def _mod_row_mm(i):
    return jnp.maximum(i - (N_CTX // TM - 1), 0)


def _ln_mod_kernel(x_ref, shift_ref, scale_ref, o_ref):
    x = x_ref[...]
    mu = jnp.mean(x, axis=-1, keepdims=True)
    xc = x - mu
    var = jnp.mean(xc * xc, axis=-1, keepdims=True)
    h = xc * lax.rsqrt(var + LN_EPS) * (1.0 + scale_ref[0]) + shift_ref[0]
    o_ref[...] = h.astype(BF16)


def _ln_mod_call(x, mods3):
    return pl.pallas_call(
        _ln_mod_kernel,
        out_shape=jax.ShapeDtypeStruct((N_TOK, D_MODEL), BF16),
        grid=(N_TOK // TM_LN,),
        in_specs=[
            pl.BlockSpec((TM_LN, D_MODEL), lambda i: (i, 0)),
            pl.BlockSpec((1, 1, D_MODEL), lambda i: (_mod_row_ln(i), 0, 0)),
            pl.BlockSpec((1, 1, D_MODEL), lambda i: (_mod_row_ln(i), 0, 1)),
        ],
        out_specs=pl.BlockSpec((TM_LN, D_MODEL), lambda i: (i, 0)),
        compiler_params=_cparams("arbitrary"),
        name="ln_modulate",
    )(x, mods3, mods3)


def _inproj_kernel(h_ref, w_ref, b_ref, o_ref):
    w = w_ref[...].astype(BF16)
    o_ref[...] = jnp.dot(h_ref[...], w, preferred_element_type=F32) + b_ref[...]


def _inproj_call(h, w_in, b_in3, l):
    return pl.pallas_call(
        _inproj_kernel,
        out_shape=jax.ShapeDtypeStruct((N_TOK, IN_COLS), F32),
        grid=(N_TOK // TM, IN_COLS // TN),
        in_specs=[
            pl.BlockSpec((TM, D_MODEL), lambda i, j: (i, 0)),
            pl.BlockSpec((None, D_MODEL, TN), lambda i, j: (l, 0, j)),
            pl.BlockSpec((None, 1, TN), lambda i, j: (l, 0, j)),
        ],
        out_specs=pl.BlockSpec((TM, TN), lambda i, j: (i, j)),
        compiler_params=_cparams("arbitrary", "arbitrary"),
        name="in_proj",
    )(h, w_in, b_in3)


def _s5_disc_kernel(are_ref, aim_ref, ldt_ref, lre_ref, lim_ref, zre_ref, zim_ref):
    ar = jnp.minimum(are_ref[...], -1e-4)
    ai = aim_ref[...]
    dt = jnp.exp(ldt_ref[...])
    mag = jnp.exp(dt * ar)
    lam_re = mag * jnp.cos(dt * ai)
    lam_im = mag * jnp.sin(dt * ai)
    den = ar * ar + ai * ai
    lre_ref[...] = lam_re
    lim_ref[...] = lam_im
    zre_ref[...] = ((lam_re - 1.0) * ar + lam_im * ai) / den
    zim_ref[...] = (lam_im * ar - (lam_re - 1.0) * ai) / den


def _s5_disc_call(a_re, a_im, log_dt):
    g = a_re.shape[1]
    shp = (2 * g, S5_STATE)
    ldt = jnp.broadcast_to(log_dt[..., None], (2, g, S5_STATE)).reshape(shp)
    outs = pl.pallas_call(
        _s5_disc_kernel,
        out_shape=[jax.ShapeDtypeStruct(shp, F32)] * 4,
        name="s5_discretise",
    )(a_re.reshape(shp), a_im.reshape(shp), ldt)
    return [o.reshape(2, g, S5_STATE) for o in outs]


def _s5_pack(lam_re, lam_im, z_re, z_im, b_re, b_im, c_re, c_im):
    g = lam_re.shape[1]
    nblk = g // (2 * S5_PAIRS)
    eye_j = jnp.eye(S5_PAIRS, dtype=F32)
    eye_g = jnp.eye(2, dtype=F32)
    bz_re = z_re[..., None] * b_re[None] - z_im[..., None] * b_im[None]
    bz_im = z_re[..., None] * b_im[None] + z_im[..., None] * b_re[None]

    def pack_b(t):
        t = t.reshape(2, nblk, S5_PAIRS, 2, S5_STATE, S5_GROUP)
        t = jnp.einsum('dbjgpc,jk,gh->dbjkgchp', t, eye_j, eye_g)
        return t.reshape(2, nblk, S5_PAIRS, LANE, LANE)

    bp = jnp.concatenate([pack_b(bz_re), pack_b(bz_im)], axis=-1).astype(BF16)

    def pack_c(t):
        t = t.reshape(nblk, S5_PAIRS, 2, S5_GROUP, S5_STATE)
        t = jnp.einsum('bjgcp,jk,gh->bjhpkgc', t, eye_j, eye_g)
        return t.reshape(nblk, S5_PAIRS, LANE, LANE)

    cp = jnp.concatenate([pack_c(c_re), -pack_c(c_im)], axis=-2).astype(BF16)
    cp = jnp.broadcast_to(cp[None], (2,) + cp.shape)

    def pack_l(t):
        return t.reshape(2, nblk, S5_PAIRS, LANE)

    lam = jnp.stack([pack_l(lam_re), pack_l(lam_im)], axis=-2)
    return bp, cp, lam


def _s5_pack_state(h0):
    b, _, g, _, _ = h0.shape
    nblk = g // (2 * S5_PAIRS)
    t = h0.reshape(b, 2, nblk, S5_PAIRS, 2, S5_STATE, 2)
    t = jnp.transpose(t, (1, 2, 3, 0, 6, 4, 5))
    return t.reshape(2, nblk, S5_PAIRS, b, 2 * LANE)


def _s5_unpack_state(fin, b):
    nblk = fin.shape[1]
    t = fin.reshape(2, nblk, S5_PAIRS, b, 2, 2, S5_STATE)
    t = jnp.transpose(t, (3, 0, 1, 2, 5, 6, 4))
    return t.reshape(b, 2, nblk * S5_PAIRS * 2, S5_STATE, 2)


def _cmul(ar, ai, br, bi):
    return ar * br - ai * bi, ar * bi + ai * br


def _s5_kernel(*refs, L, B, nseg):
    if nseg > 1:
        (u_ref, bp_ref, cp_ref, lam_ref, d_ref, h0_ref, y_ref,
         xs_ref, yacc_ref, fl_ref, hp_ref) = refs
    else:
        u_ref, bp_ref, cp_ref, lam_ref, d_ref, y_ref, fin_ref, xs_ref, yacc_ref = refs
    nb = B // nseg
    u = u_ref[...]
    u2 = u.reshape(L * B, LANE)
    u2b = u2.astype(BF16)
    yacc_ref[...] = jnp.zeros((L * B, LANE), F32)

    for half in range(S5_PAIRS // 2):
        chains = [(d, jj) for d in range(2) for jj in range(2)]
        for d, jj in chains:
            x = jnp.dot(u2b, bp_ref[d, 0, 2 * half + jj], preferred_element_type=F32)
            xs_ref[d, jj] = x.reshape(L, B, 2 * LANE)
        lam = {}
        for d, jj in chains:
            lr = jnp.broadcast_to(lam_ref[d, 0, 2 * half + jj, 0:1, :], (B, LANE))
            li = jnp.broadcast_to(lam_ref[d, 0, 2 * half + jj, 1:2, :], (B, LANE))
            lam[(d, jj)] = (lr, li)

        def step(i, carry):
            out = []
            for n, (d, jj) in enumerate(chains):
                t = i if d == 0 else L - 1 - i
                hr, hi = carry[2 * n], carry[2 * n + 1]
                lr, li = lam[(d, jj)]
                pr, pi = _cmul(lr, li, hr, hi)
                nr = pr + xs_ref[d, jj, t, :, 0:LANE]
                ni = pi + xs_ref[d, jj, t, :, LANE:2 * LANE]
                xs_ref[d, jj, t, :, 0:LANE] = nr
                xs_ref[d, jj, t, :, LANE:2 * LANE] = ni
                out += [nr, ni]
            return tuple(out)

        zero = jnp.zeros((B, LANE), F32)
        carry = lax.fori_loop(0, L, step, (zero,) * (2 * len(chains)), unroll=2)

        if nseg == 1:
            for n, (d, jj) in enumerate(chains):
                fin_ref[d, 0, 2 * half + jj, :, 0:LANE] = carry[2 * n]
                fin_ref[d, 0, 2 * half + jj, :, LANE:2 * LANE] = carry[2 * n + 1]
        else:
            hp = {}
            for n, (d, jj) in enumerate(chains):
                plr = jnp.broadcast_to(lam_ref[d, 0, 2 * half + jj, 0:1, :], (nb, LANE))
                pli = jnp.broadcast_to(lam_ref[d, 0, 2 * half + jj, 1:2, :], (nb, LANE))
                for _ in range(int(math.log2(L))):
                    plr, pli = _cmul(plr, pli, plr, pli)
                fl_ref[0] = carry[2 * n]
                fl_ref[1] = carry[2 * n + 1]
                order = list(range(nseg)) if d == 0 else list(range(nseg - 1, -1, -1))
                cr = h0_ref[d, 0, 2 * half + jj, :, 0:LANE]
                ci = h0_ref[d, 0, 2 * half + jj, :, LANE:2 * LANE]
                for k, s in enumerate(order):
                    if k > 0:
                        sp = order[k - 1]
                        mr, mi = _cmul(plr, pli, cr, ci)
                        cr = fl_ref[0, sp * nb:(sp + 1) * nb, :] + mr
                        ci = fl_ref[1, sp * nb:(sp + 1) * nb, :] + mi
                    hp_ref[n, 0, s * nb:(s + 1) * nb, :] = cr
                    hp_ref[n, 1, s * nb:(s + 1) * nb, :] = ci
                hp[(d, jj)] = (hp_ref[n, 0], hp_ref[n, 1])

            def cstep(i, pw):
                out = []
                for n, (d, jj) in enumerate(chains):
                    t = i if d == 0 else L - 1 - i
                    pr, pi = pw[2 * n], pw[2 * n + 1]
                    hr, hi = hp[(d, jj)]
                    ar, ai = _cmul(pr, pi, hr, hi)
                    xs_ref[d, jj, t, :, 0:LANE] = xs_ref[d, jj, t, :, 0:LANE] + ar
                    xs_ref[d, jj, t, :, LANE:2 * LANE] = xs_ref[d, jj, t, :, LANE:2 * LANE] + ai
                    lr, li = lam[(d, jj)]
                    nr, ni = _cmul(lr, li, pr, pi)
                    out += [nr, ni]
                return tuple(out)

            pw0 = []
            for d, jj in chains:
                pw0 += list(lam[(d, jj)])
            lax.fori_loop(0, L, cstep, tuple(pw0), unroll=2)

        for d, jj in chains:
            hs = xs_ref[d, jj].reshape(L * B, 2 * LANE).astype(BF16)
            yacc_ref[...] += jnp.dot(hs, cp_ref[d, 0, 2 * half + jj], preferred_element_type=F32)

    y = u2 * d_ref[...] + yacc_ref[...]
    y_ref[...] = jax.nn.gelu(y).reshape(L, B, LANE)


def _s5_call(u_tm, bp, cp, lam, d_skip, h0, *, L, B, nseg):
    w = u_tm.shape[-1]
    nblk = w // LANE
    in_specs = [
        pl.BlockSpec((L, B, LANE), lambda k: (0, 0, k)),
        pl.BlockSpec((2, 1, S5_PAIRS, LANE, 2 * LANE), lambda k: (0, k, 0, 0, 0)),
        pl.BlockSpec((2, 1, S5_PAIRS, 2 * LANE, LANE), lambda k: (0, k, 0, 0, 0)),
        pl.BlockSpec((2, 1, S5_PAIRS, 2, LANE), lambda k: (0, k, 0, 0, 0)),
        pl.BlockSpec((1, LANE), lambda k: (0, k)),
    ]
    args = [u_tm, bp, cp, lam, d_skip.reshape(1, w)]
    y_shape = jax.ShapeDtypeStruct((L, B, w), F32)
    y_spec = pl.BlockSpec((L, B, LANE), lambda k: (0, 0, k))
    scratch = [pltpu.VMEM((2, 2, L, B, 2 * LANE), F32), pltpu.VMEM((L * B, LANE), F32)]
    if nseg > 1:
        nb = B // nseg
        in_specs.append(pl.BlockSpec((2, 1, S5_PAIRS, nb, 2 * LANE), lambda k: (0, k, 0, 0, 0)))
        args.append(h0)
        out_shape, out_specs = y_shape, y_spec
        scratch += [pltpu.VMEM((2, B, LANE), F32), pltpu.VMEM((4, 2, B, LANE), F32)]
    else:
        out_shape = [y_shape, jax.ShapeDtypeStruct((2, nblk, S5_PAIRS, B, 2 * LANE), F32)]
        out_specs = [y_spec, pl.BlockSpec((2, 1, S5_PAIRS, B, 2 * LANE), lambda k: (0, k, 0, 0, 0))]
    return pl.pallas_call(
        functools.partial(_s5_kernel, L=L, B=B, nseg=nseg),
        out_shape=out_shape,
        grid=(nblk,),
        in_specs=in_specs,
        out_specs=out_specs,
        scratch_shapes=scratch,
        compiler_params=_cparams("arbitrary"),
        name="s5_scan_seg" if nseg > 1 else "s5_scan",
    )(*args)


def _glu_kernel(yrow_ref, ytile_ref, w_ref, b_ref, g_ref, o_ref):
    w = w_ref[...].astype(BF16)
    z = jnp.dot(yrow_ref[...].astype(BF16), w, preferred_element_type=F32) + b_ref[...]
    o_ref[...] = (ytile_ref[...] * _sigmoid(z) * _silu(g_ref[...])).astype(BF16)


def _glu_call(y, w_glu, b_glu3, proj, l):
    goff = C_GA // TN
    return pl.pallas_call(
        _glu_kernel,
        out_shape=jax.ShapeDtypeStruct((N_TOK, W_BR), BF16),
        grid=(N_TOK // TM, W_BR // TN),
        in_specs=[
            pl.BlockSpec((TM, W_BR), lambda i, j: (i, 0)),
            pl.BlockSpec((TM, TN), lambda i, j: (i, j)),
            pl.BlockSpec((None, W_BR, TN), lambda i, j: (l, 0, j)),
            pl.BlockSpec((None, 1, TN), lambda i, j: (l, 0, j)),
            pl.BlockSpec((TM, TN), lambda i, j: (i, goff + j)),
        ],
        out_specs=pl.BlockSpec((TM, TN), lambda i, j: (i, j)),
        compiler_params=_cparams("arbitrary", "arbitrary"),
        name="s5_glu",
    )(y, y, w_glu, b_glu3, proj)


def _expm1(x):
    t = jnp.tanh(0.5 * x)
    return 2.0 * t / (1.0 - t)


def _lru_kernel(*refs, L, B, nseg, period):
    if nseg > 1:
        (x_ref, g_ref, cw_ref, cb_ref, wa_ref, ba_ref, wx_ref, bx_ref, sp_ref, h0_ref,
         y_ref, a_ref, b_ref, fl_ref, hp_ref) = refs
    else:
        (x_ref, g_ref, cw_ref, cb_ref, wa_ref, ba_ref, wx_ref, bx_ref, sp_ref,
         y_ref, fin_ref, a_ref, b_ref) = refs
    nb = B // nseg
    x = x_ref[...]
    pos = lax.broadcasted_iota(jnp.int32, (L, B, LANE), 0) % period
    xc = jnp.broadcast_to(cb_ref[...].reshape(1, 1, LANE), (L, B, LANE))
    for k in range(CONV_W):
        off = k - CONV_W // 2
        if off < 0:
            xs = jnp.concatenate([jnp.zeros((-off, B, LANE), F32), x[:L + off]], axis=0)
        elif off > 0:
            xs = jnp.concatenate([x[off:], jnp.zeros((off, B, LANE), F32)], axis=0)
        else:
            xs = x
        valid = (pos + off >= 0) & (pos + off < period)
        xc = xc + cw_ref[k:k + 1, :].reshape(1, 1, LANE) * jnp.where(valid, xs, 0.0)
    xc2 = xc.reshape(L * B, LANE)
    xcb = xc2.astype(BF16)
    for d in range(2):
        r = _sigmoid(jnp.dot(xcb, wa_ref[d, 0].astype(BF16), preferred_element_type=F32)
                     + ba_ref[d:d + 1, :])
        ig = _sigmoid(jnp.dot(xcb, wx_ref[d, 0].astype(BF16), preferred_element_type=F32)
                      + bx_ref[d:d + 1, :])
        log_a = (-LRU_C) * r * sp_ref[d:d + 1, :]
        a_ref[d] = jnp.exp(log_a).reshape(L, B, LANE)
        mult = jnp.sqrt(-_expm1(2.0 * log_a))
        b_ref[d] = (mult * (ig * xc2)).reshape(L, B, LANE)

    def step(i, carry):
        out = []
        for d in range(2):
            t = i if d == 0 else L - 1 - i
            a = a_ref[d, t]
            h = a * carry[2 * d] + b_ref[d, t]
            b_ref[d, t] = h
            if nseg > 1:
                p = a * carry[2 * d + 1]
                a_ref[d, t] = p
            else:
                p = carry[2 * d + 1]
            out += [h, p]
        return tuple(out)

    zero = jnp.zeros((B, LANE), F32)
    one = jnp.ones((B, LANE), F32)
    carry = lax.fori_loop(0, L, step, (zero, one, zero, one), unroll=2)

    if nseg == 1:
        fin_ref[0] = carry[0]
        fin_ref[1] = carry[2]
    else:
        hps = []
        for d in range(2):
            fl_ref[0] = carry[2 * d]
            fl_ref[1] = carry[2 * d + 1]
            order = list(range(nseg)) if d == 0 else list(range(nseg - 1, -1, -1))
            c = h0_ref[d]
            for k, s in enumerate(order):
                if k > 0:
                    sp = order[k - 1]
                    c = fl_ref[0, sp * nb:(sp + 1) * nb, :] + fl_ref[1, sp * nb:(sp + 1) * nb, :] * c
                hp_ref[d, s * nb:(s + 1) * nb, :] = c
            hps.append(hp_ref[d])

        def cstep(i, c):
            for d in range(2):
                t = i if d == 0 else L - 1 - i
                b_ref[d, t] = b_ref[d, t] + a_ref[d, t] * hps[d]
            return c

        lax.fori_loop(0, L, cstep, 0, unroll=2)

    y_ref[...] = (b_ref[0] + b_ref[1]) * _silu(g_ref[...])


def _lru_call(x_tm, g_tm, conv_w, conv_b, w_a, b_a, w_x, b_x, sp, h0, *, L, B, nseg, period):
    w = x_tm.shape[-1]
    nblk = w // LANE
    blk3 = pl.BlockSpec((L, B, LANE), lambda k: (0, 0, k))
    in_specs = [
        blk3, blk3,
        pl.BlockSpec((CONV_W, LANE), lambda k: (0, k)),
        pl.BlockSpec((1, LANE), lambda k: (0, k)),
        pl.BlockSpec((2, 1, LANE, LANE), lambda k: (0, k, 0, 0)),
        pl.BlockSpec((2, LANE), lambda k: (0, k)),
        pl.BlockSpec((2, 1, LANE, LANE), lambda k: (0, k, 0, 0)),
        pl.BlockSpec((2, LANE), lambda k: (0, k)),
        pl.BlockSpec((2, LANE), lambda k: (0, k)),
    ]
    args = [x_tm, g_tm, conv_w, conv_b.reshape(1, w), w_a, b_a, w_x, b_x, sp]
    y_shape = jax.ShapeDtypeStruct((L, B, w), F32)
    scratch = [pltpu.VMEM((2, L, B, LANE), F32), pltpu.VMEM((2, L, B, LANE), F32)]
    if nseg > 1:
        nb = B // nseg
        in_specs.append(pl.BlockSpec((2, nb, LANE), lambda k: (0, 0, k)))
        args.append(h0)
        out_shape, out_specs = y_shape, blk3
        scratch += [pltpu.VMEM((2, B, LANE), F32), pltpu.VMEM((2, B, LANE), F32)]
    else:
        out_shape = [y_shape, jax.ShapeDtypeStruct((2, B, w), F32)]
        out_specs = [blk3, pl.BlockSpec((2, B, LANE), lambda k: (0, 0, k))]
    return pl.pallas_call(
        functools.partial(_lru_kernel, L=L, B=B, nseg=nseg, period=period),
        out_shape=out_shape,
        grid=(nblk,),
        in_specs=in_specs,
        out_specs=out_specs,
        scratch_shapes=scratch,
        compiler_params=_cparams("arbitrary"),
        name="rglru_seg" if nseg > 1 else "rglru",
    )(*args)


def _nt_dot(a, b):
    return lax.dot_general(a, b, (((1,), (1,)), ((), ())), preferred_element_type=F32)


def _hgrn_kernel(*refs, L, has_h0):
    if has_h0:
        q_ref, ff_ref, fb_ref, v_ref, g_ref, lb_ref, nw_ref, h0_ref, y_ref, fin_ref, of_ref = refs
    else:
        q_ref, ff_ref, fb_ref, v_ref, g_ref, lb_ref, nw_ref, y_ref, fin_ref, of_ref = refs
    rb_rows = HG_RB
    nrb = L // rb_rows
    nch = rb_rows // HG_CHUNK
    row = lax.broadcasted_iota(jnp.int32, (rb_rows, rb_rows), 0)
    col = lax.broadcasted_iota(jnp.int32, (rb_rows, rb_rows), 1)
    same = (row // HG_CHUNK) == (col // HG_CHUNK)
    pos = lax.broadcasted_iota(jnp.int32, (rb_rows, HG_DK), 0) % HG_CHUNK
    chunk_id = lax.broadcasted_iota(jnp.int32, (rb_rows, HG_DK), 0) // HG_CHUNK

    def chunk_bcast(b3, idx):
        return jnp.broadcast_to(b3[:, idx:idx + 1, :], (nch, HG_CHUNK, HG_DK)).reshape(rb_rows, HG_DK)

    def run_block(rb, st, d):
        r0 = pl.multiple_of(rb * rb_rows, rb_rows)
        q = q_ref[pl.ds(r0, rb_rows), :]
        qf = _silu(q)
        f = (ff_ref if d == 0 else fb_ref)[pl.ds(r0, rb_rows), :]
        lbd = lb_ref[d:d + 1, :]
        g = lbd + (1.0 - lbd) * _sigmoid(f)
        kk = 1.0 - g
        b = jnp.log(g)
        v = v_ref[pl.ds(r0, rb_rows), :]
        s = 1
        while s < HG_CHUNK:
            if d == 0:
                b = b + jnp.where(pos >= s, pltpu.roll(b, s, 0), 0.0)
            else:
                b = b + jnp.where(pos <= HG_CHUNK - 1 - s, pltpu.roll(b, rb_rows - s, 0), 0.0)
            s *= 2
        b3 = b.reshape(nch, HG_CHUNK, HG_DK)
        bmid = chunk_bcast(b3, HG_CHUNK // 2)
        btot = chunk_bcast(b3, HG_CHUNK - 1 if d == 0 else 0)
        qt = (qf * jnp.exp(b - bmid)).astype(BF16)
        kt = (kk * jnp.exp(bmid - b)).astype(BF16)
        sc = _nt_dot(qt, kt)
        causal = (col <= row) if d == 0 else (col >= row)
        sc = jnp.where(same & causal, sc, 0.0)
        vb = v.astype(BF16)
        o = jnp.dot(sc.astype(BF16), vb, preferred_element_type=F32)
        qe = (qf * jnp.exp(b)).astype(BF16)
        kd = kk * jnp.exp(btot - b)
        vt = v.T.astype(BF16)
        outs = [None] * nch
        order = range(nch) if d == 0 else range(nch - 1, -1, -1)
        for c in order:
            lo = c * HG_CHUNK
            oi = _nt_dot(qe[lo:lo + HG_CHUNK], st.astype(BF16))
            outs[c] = o[lo:lo + HG_CHUNK] + oi
            kdc = jnp.where(chunk_id == c, kd, 0.0).astype(BF16)
            ut = jnp.dot(vt, kdc, preferred_element_type=F32)
            tot_row = lo + HG_CHUNK - 1 if d == 0 else lo
            st = st * jnp.exp(b[tot_row:tot_row + 1, :]) + ut
        return jnp.concatenate(outs, axis=0), st

    for d in range(2):
        if has_h0:
            st0 = h0_ref[0, d, 0].T
        else:
            st0 = jnp.zeros((HG_DK, HG_DK), F32)

        def body(i, st, d=d):
            rb = i if d == 0 else nrb - 1 - i
            oblk, st = run_block(rb, st, d)
            r0 = pl.multiple_of(rb * rb_rows, rb_rows)
            if d == 0:
                of_ref[pl.ds(r0, rb_rows), :] = oblk
            else:
                of_ref[pl.ds(r0, rb_rows), :] = of_ref[pl.ds(r0, rb_rows), :] + oblk
            return st

        st = lax.fori_loop(0, nrb, body, st0)
        fin_ref[0, d, 0] = st.T

    o = of_ref[...]
    o = o * lax.rsqrt(jnp.mean(o * o, axis=-1, keepdims=True) + RMS_EPS)
    y_ref[...] = (o * nw_ref[...] * _silu(g_ref[...])).astype(BF16)


def _hgrn_call(proj, lb, norm_w, h0, *, L, nb, row0):
    w = lb.shape[-1]
    nh = w // HG_DK
    rblk0 = row0 // L

    def col(off):
        return pl.BlockSpec((L, HG_DK), lambda b, h: (rblk0 + b, off // HG_DK + h))

    in_specs = [col(C_QC), col(C_FF), col(C_FB), col(C_IC), col(C_GC),
                pl.BlockSpec((2, HG_DK), lambda b, h: (0, h)),
                pl.BlockSpec((1, HG_DK), lambda b, h: (0, h))]
    args = [proj] * 5 + [lb, norm_w.reshape(1, w)]
    st_spec = pl.BlockSpec((1, 2, 1, HG_DK, HG_DK), lambda b, h: (b, 0, h, 0, 0))
    if h0 is not None:
        in_specs.append(st_spec)
        args.append(h0)
    return pl.pallas_call(
        functools.partial(_hgrn_kernel, L=L, has_h0=h0 is not None),
        out_shape=[jax.ShapeDtypeStruct((nb * L, w), BF16),
                   jax.ShapeDtypeStruct((nb, 2, nh, HG_DK, HG_DK), F32)],
        grid=(nb, nh),
        in_specs=in_specs,
        out_specs=[pl.BlockSpec((L, HG_DK), lambda b, h: (b, h)), st_spec],
        scratch_shapes=[pltpu.VMEM((L, HG_DK), F32)],
        compiler_params=_cparams("arbitrary", "arbitrary"),
        name="hgrn2",
    )(*args)


def _merge_kernel(ya_ref, yb_ref, yc_ref, wa_ref, wb_ref, wc_ref, ma_ref, mb_ref, mc_ref, o_ref):
    acc = None
    for y_ref, w_ref, m_ref in ((ya_ref, wa_ref, ma_ref), (yb_ref, wb_ref, mb_ref), (yc_ref, wc_ref, mc_ref)):
        t = jnp.dot(y_ref[...], w_ref[...].astype(BF16), preferred_element_type=F32)
        t = _sigmoid(m_ref[...]) * t
        acc = t if acc is None else acc + t
    o_ref[...] = acc.astype(BF16)


def _merge_call(ya, yb, yc, w_br, proj, l):
    moff = C_MA // TN
    mstep = D_MODEL // TN
    tm = TM_MERGE
    lhs = pl.BlockSpec((tm, W_BR), lambda i, j: (i, 0))

    def wspec(br):
        return pl.BlockSpec((None, None, W_BR, TN), lambda i, j: (l, br, 0, j))

    def mspec(br):
        return pl.BlockSpec((tm, TN), lambda i, j: (i, moff + br * mstep + j))

    return pl.pallas_call(
        _merge_kernel,
        out_shape=jax.ShapeDtypeStruct((N_TOK, D_MODEL), BF16),
        grid=(N_TOK // tm, D_MODEL // TN),
        in_specs=[lhs, lhs, lhs, wspec(0), wspec(1), wspec(2), mspec(0), mspec(1), mspec(2)],
        out_specs=pl.BlockSpec((tm, TN), lambda i, j: (i, j)),
        compiler_params=_cparams("arbitrary", "arbitrary"),
        name="branch_merge",
    )(ya, yb, yc, w_br, w_br, w_br, proj, proj, proj)


def _outproj_kernel(m_ref, w_ref, b_ref, x_ref, gate_ref, o_ref):
    out = jnp.dot(m_ref[...], w_ref[...].astype(BF16), preferred_element_type=F32) + b_ref[...]
    o_ref[...] = DN_ALPHA * x_ref[...] + gate_ref[0] * out


def _outproj_call(merged, w_out, b_out3, x, mods3, l):
    goff = 2 * D_MODEL // TN
    return pl.pallas_call(
        _outproj_kernel,
        out_shape=jax.ShapeDtypeStruct((N_TOK, D_MODEL), F32),
        grid=(N_TOK // TM, D_MODEL // TN),
        in_specs=[
            pl.BlockSpec((TM, D_MODEL), lambda i, j: (i, 0)),
            pl.BlockSpec((None, D_MODEL, TN), lambda i, j: (l, 0, j)),
            pl.BlockSpec((None, 1, TN), lambda i, j: (l, 0, j)),
            pl.BlockSpec((TM, TN), lambda i, j: (i, j)),
            pl.BlockSpec((1, 1, TN), lambda i, j: (_mod_row_mm(i), 0, goff + j)),
        ],
        out_specs=pl.BlockSpec((TM, TN), lambda i, j: (i, j)),
        compiler_params=_cparams("arbitrary", "arbitrary"),
        name="out_proj",
    )(merged, w_out, b_out3, x, mods3)


def _ln_affine_kernel(r_ref, g_ref, b_ref, o_ref):
    r = r_ref[...]
    mu = jnp.mean(r, axis=-1, keepdims=True)
    rc = r - mu
    var = jnp.mean(rc * rc, axis=-1, keepdims=True)
    o_ref[...] = rc * lax.rsqrt(var + LN_EPS) * g_ref[...] + b_ref[...]


def _ln_affine_call(r, g, b, l):
    return pl.pallas_call(
        _ln_affine_kernel,
        out_shape=jax.ShapeDtypeStruct((N_TOK, D_MODEL), F32),
        grid=(N_TOK // TM_LN,),
        in_specs=[
            pl.BlockSpec((TM_LN, D_MODEL), lambda i: (i, 0)),
            pl.BlockSpec((None, 1, D_MODEL), lambda i: (l, 0, 0)),
            pl.BlockSpec((None, 1, D_MODEL), lambda i: (l, 0, 0)),
        ],
        out_specs=pl.BlockSpec((TM_LN, D_MODEL), lambda i: (i, 0)),
        compiler_params=_cparams("arbitrary"),
        name="post_ln",
    )(r, g, b)


def _ctx_to_tm(a):
    return jnp.transpose(a.reshape(BATCH, SEQ, -1), (1, 0, 2))


def _lat_to_tm(a):
    t = a.reshape(DEC_BATCH, LAT_SEG, SEQ, -1)
    return jnp.transpose(t, (2, 1, 0, 3)).reshape(SEQ, LAT_ROWS, -1)


def _tm_to_tokens(y_ctx, y_lat):
    w = y_ctx.shape[-1]
    c = jnp.transpose(y_ctx, (1, 0, 2)).reshape(N_CTX, w)
    t = y_lat.reshape(SEQ, LAT_SEG, DEC_BATCH, w)
    t = jnp.transpose(t, (2, 1, 0, 3)).reshape(N_LAT, w)
    return jnp.concatenate([c, t], axis=0)


def kernel(x_prompt, x_sample, state_s5, state_lru, state_hgrn, c, c_ctx, w_ada, b_ada, w_in, b_in,
           s5_a_re, s5_a_im, s5_log_dt, s5_b_re, s5_b_im, s5_c_re, s5_c_im, s5_d, s5_w_glu, s5_b_glu,
           lru_conv_w, lru_conv_b, lru_w_a, lru_b_a, lru_w_x, lru_b_x, lru_lambda, hg_lb, hg_norm_w,
           w_br, w_out, b_out, ln_g, ln_b):
    lb_soft = jax.nn.softmax(hg_lb.astype(F32), axis=0)
    lb_all = jnp.cumsum(lb_soft, axis=0) - lb_soft[0]
    softplus_neg_lam = jax.nn.softplus(-lru_lambda.astype(F32))

    cvec = jnp.zeros((SUBLANE, D_MODEL), F32)
    cvec = cvec.at[0].set(c_ctx).at[1:1 + DEC_BATCH].set(c)
    mods = _mods_call(cvec, w_ada, b_ada)

    b_in3 = b_in.reshape(DEPTH, 1, IN_COLS)
    b_glu3 = s5_b_glu.reshape(DEPTH, 1, W_BR)
    b_out3 = b_out.reshape(DEPTH, 1, D_MODEL)
    ln_g3 = ln_g.reshape(DEPTH, 1, D_MODEL)
    ln_b3 = ln_b.reshape(DEPTH, 1, D_MODEL)

    x = jnp.concatenate([x_prompt.reshape(N_CTX, D_MODEL), x_sample.reshape(N_LAT, D_MODEL)], axis=0)
    st_s5, st_lru, st_hg = [], [], []
    for l in range(DEPTH):
        mods3 = mods[l, :1 + DEC_BATCH].reshape(1 + DEC_BATCH, 1, 3 * D_MODEL)
        h = _ln_mod_call(x, mods3)
        proj = _inproj_call(h, w_in, b_in3, l)

        lam_re, lam_im, z_re, z_im = _s5_disc_call(s5_a_re[l], s5_a_im[l], s5_log_dt[l])
        bp, cp, lam = _s5_pack(lam_re, lam_im, z_re, z_im, s5_b_re[l], s5_b_im[l], s5_c_re[l], s5_c_im[l])
        u_a = proj[:, C_UA:C_UA + W_BR]
        ya_ctx, fin_s5 = _s5_call(_ctx_to_tm(u_a[:N_CTX]), bp, cp, lam, s5_d[l], None,
                                  L=SEQ, B=BATCH, nseg=1)
        ya_lat = _s5_call(_lat_to_tm(u_a[N_CTX:]), bp, cp, lam, s5_d[l],
                          _s5_pack_state(state_s5[:, l]), L=SEQ, B=LAT_ROWS, nseg=LAT_SEG)
        y_pre = _tm_to_tokens(ya_ctx, ya_lat)
        y_a = _glu_call(y_pre, s5_w_glu, b_glu3, proj, l)
        st_s5.append(_s5_unpack_state(fin_s5, BATCH))

        x_b = proj[:, C_XB:C_XB + W_BR]
        g_b = proj[:, C_GB:C_GB + W_BR]
        lru_args = (lru_conv_w[l], lru_conv_b[l], lru_w_a[l], lru_b_a[l], lru_w_x[l], lru_b_x[l],
                    softplus_neg_lam[l])
        yb_ctx, fin_lru = _lru_call(_ctx_to_tm(x_b[:N_CTX]), _ctx_to_tm(g_b[:N_CTX]), *lru_args, None,
                                    L=SEQ, B=BATCH, nseg=1, period=SEQ)
        yb_lat = _lru_call(_lat_to_tm(x_b[N_CTX:]), _lat_to_tm(g_b[N_CTX:]), *lru_args,
                           jnp.transpose(state_lru[:, l], (1, 0, 2)),
                           L=SEQ, B=LAT_ROWS, nseg=LAT_SEG, period=GRID_W)
        y_b = _tm_to_tokens(yb_ctx, yb_lat).astype(BF16)
        st_lru.append(jnp.transpose(fin_lru, (1, 0, 2)))

        yc_ctx, fin_hg = _hgrn_call(proj, lb_all[l], hg_norm_w[l], None, L=SEQ, nb=BATCH, row0=0)
        yc_lat, _ = _hgrn_call(proj, lb_all[l], hg_norm_w[l], state_hgrn[:, l],
                               L=DEC_SEQ, nb=DEC_BATCH, row0=N_CTX)
        y_c = jnp.concatenate([yc_ctx, yc_lat], axis=0)
        st_hg.append(fin_hg)

        merged = _merge_call(y_a, y_b, y_c, w_br, proj, l)
        r = _outproj_call(merged, w_out, b_out3, x, mods3, l)
        x = _ln_affine_call(r, ln_g3, ln_b3, l)

    y_prompt = x[:N_CTX].reshape(BATCH, SEQ, D_MODEL)
    y_sample = x[N_CTX:].reshape(DEC_BATCH, DEC_SEQ, D_MODEL)
    new_state_s5 = jnp.stack(st_s5, axis=1)
    new_state_lru = jnp.stack(st_lru, axis=1)
    new_state_hgrn = jnp.stack(st_hg, axis=1)
    return (y_prompt, y_sample, new_state_s5, new_state_lru, new_state_hgrn)
```

```python
import functools
import math

import jax
import jax.numpy as jnp
from jax import lax
from jax.experimental import pallas as pl
from jax.experimental.pallas import tpu as pltpu

F32 = jnp.float32
BF16 = jnp.bfloat16

LANE = 128
SUBLANE = 8
VMEM_LIMIT = 56 * 1024 * 1024

D_MODEL = 4096
DEPTH = 2
BATCH, SEQ = 16, 256
DEC_BATCH, DEC_SEQ = 2, 1024
GRID_W = 64
W_BR = D_MODEL // 2
S5_GROUP = 16
S5_STATE = 64
LRU_C = 8.0
CONV_W = 4
HG_DK = 128
HG_CHUNK = 16
N_BRANCH = 3
IN_COLS = 9 * W_BR + N_BRANCH * D_MODEL
DN_ALPHA = (2 * DEPTH) ** 0.25
LN_EPS = 1e-5
RMS_EPS = 1e-6

N_CTX = BATCH * SEQ
N_LAT = DEC_BATCH * DEC_SEQ
N_TOK = N_CTX + N_LAT
LAT_SEG = DEC_SEQ // SEQ
LAT_ROWS = LAT_SEG * DEC_BATCH

C_UA, C_GA, C_XB, C_GB, C_QC, C_FF, C_FB, C_IC, C_GC = (i * W_BR for i in range(9))
C_MA = 9 * W_BR

TM = 1024
TN = 512
TM_MERGE = 512
TM_LN = 256
HG_RB = 256
HG_ST = 64
S5_PAIRS = 4


def _cparams(*sem):
    return pltpu.CompilerParams(dimension_semantics=sem, vmem_limit_bytes=VMEM_LIMIT)


def _sigmoid(x):
    return jax.nn.sigmoid(x)


def _silu(x):
    return x * jax.nn.sigmoid(x)


def _mods_kernel(c_ref, w_ref, b_ref, o_ref):
    c = c_ref[...]
    s = _silu(c).astype(BF16)
    w = w_ref[...].astype(BF16)
    o_ref[...] = jnp.dot(s, w, preferred_element_type=F32) + b_ref[...]


def _mods_call(cvec, w_ada, b_ada):
    tn = TN
    return pl.pallas_call(
        _mods_kernel,
        out_shape=jax.ShapeDtypeStruct((DEPTH, SUBLANE, 3 * D_MODEL), F32),
        grid=(DEPTH, 3 * D_MODEL // tn),
        in_specs=[
            pl.BlockSpec((SUBLANE, D_MODEL), lambda l, j: (0, 0)),
            pl.BlockSpec((None, D_MODEL, tn), lambda l, j: (l, 0, j)),
            pl.BlockSpec((None, 1, tn), lambda l, j: (l, 0, j)),
        ],
        out_specs=pl.BlockSpec((None, SUBLANE, tn), lambda l, j: (l, 0, j)),
        compiler_params=_cparams("arbitrary", "arbitrary"),
        name="adaln_mods",
    )(cvec, w_ada, b_ada.reshape(DEPTH, 1, 3 * D_MODEL))


def _mod_row_ln(i):
    n_ctx_tiles = N_CTX // TM_LN
    return jnp.where(i < n_ctx_tiles, 0, 1 + (i - n_ctx_tiles) // (DEC_SEQ // TM_LN))


def _mod_row_mm(i):
    return jnp.maximum(i - (N_CTX // TM - 1), 0)


def _ln_mod_kernel(x_ref, shift_ref, scale_ref, o_ref):
    x = x_ref[...]
    mu = jnp.mean(x, axis=-1, keepdims=True)
    xc = x - mu
    var = jnp.mean(xc * xc, axis=-1, keepdims=True)
    h = xc * lax.rsqrt(var + LN_EPS) * (1.0 + scale_ref[0]) + shift_ref[0]
    o_ref[...] = h.astype(BF16)


def _ln_mod_call(x, mods3):
    return pl.pallas_call(
        _ln_mod_kernel,
        out_shape=jax.ShapeDtypeStruct((N_TOK, D_MODEL), BF16),
        grid=(N_TOK // TM_LN,),
        in_specs=[
            pl.BlockSpec((TM_LN, D_MODEL), lambda i: (i, 0)),
            pl.BlockSpec((1, 1, D_MODEL), lambda i: (_mod_row_ln(i), 0, 0)),
            pl.BlockSpec((1, 1, D_MODEL), lambda i: (_mod_row_ln(i), 0, 1)),
        ],
        out_specs=pl.BlockSpec((TM_LN, D_MODEL), lambda i: (i, 0)),
        compiler_params=_cparams("arbitrary"),
        name="ln_modulate",
    )(x, mods3, mods3)


def _inproj_kernel(h_ref, w_ref, b_ref, o_ref):
    w = w_ref[...].astype(BF16)
    o_ref[...] = jnp.dot(h_ref[...], w, preferred_element_type=F32) + b_ref[...]


def _inproj_call(h, w_in, b_in3, l):
    return pl.pallas_call(
        _inproj_kernel,
        out_shape=jax.ShapeDtypeStruct((N_TOK, IN_COLS), F32),
        grid=(N_TOK // TM, IN_COLS // TN),
        in_specs=[
            pl.BlockSpec((TM, D_MODEL), lambda i, j: (i, 0)),
            pl.BlockSpec((None, D_MODEL, TN), lambda i, j: (l, 0, j)),
            pl.BlockSpec((None, 1, TN), lambda i, j: (l, 0, j)),
        ],
        out_specs=pl.BlockSpec((TM, TN), lambda i, j: (i, j)),
        compiler_params=_cparams("arbitrary", "arbitrary"),
        name="in_proj",
    )(h, w_in, b_in3)


def _s5_disc_kernel(are_ref, aim_ref, ldt_ref, lre_ref, lim_ref, zre_ref, zim_ref):
    ar = jnp.minimum(are_ref[...], -1e-4)
    ai = aim_ref[...]
    dt = jnp.exp(ldt_ref[...])
    mag = jnp.exp(dt * ar)
    lam_re = mag * jnp.cos(dt * ai)
    lam_im = mag * jnp.sin(dt * ai)
    den = ar * ar + ai * ai
    lre_ref[...] = lam_re
    lim_ref[...] = lam_im
    zre_ref[...] = ((lam_re - 1.0) * ar + lam_im * ai) / den
    zim_ref[...] = (lam_im * ar - (lam_re - 1.0) * ai) / den


def _s5_disc_call(a_re, a_im, log_dt):
    g = a_re.shape[1]
    shp = (2 * g, S5_STATE)
    ldt = jnp.broadcast_to(log_dt[..., None], (2, g, S5_STATE)).reshape(shp)
    outs = pl.pallas_call(
        _s5_disc_kernel,
        out_shape=[jax.ShapeDtypeStruct(shp, F32)] * 4,
        name="s5_discretise",
    )(a_re.reshape(shp), a_im.reshape(shp), ldt)
    return [o.reshape(2, g, S5_STATE) for o in outs]


def _s5_pack(lam_re, lam_im, z_re, z_im, b_re, b_im, c_re, c_im):
    g = lam_re.shape[1]
    nblk = g // (2 * S5_PAIRS)
    eye_j = jnp.eye(S5_PAIRS, dtype=F32)
    eye_g = jnp.eye(2, dtype=F32)
    bz_re = z_re[..., None] * b_re[None] - z_im[..., None] * b_im[None]
    bz_im = z_re[..., None] * b_im[None] + z_im[..., None] * b_re[None]

    def pack_b(t):
        t = t.reshape(2, nblk, S5_PAIRS, 2, S5_STATE, S5_GROUP)
        t = jnp.einsum('dbjgpc,jk,gh->dbjkgchp', t, eye_j, eye_g)
        return t.reshape(2, nblk, S5_PAIRS, LANE, LANE)

    bp = jnp.concatenate([pack_b(bz_re), pack_b(bz_im)], axis=-1).astype(BF16)

    def pack_c(t):
        t = t.reshape(nblk, S5_PAIRS, 2, S5_GROUP, S5_STATE)
        t = jnp.einsum('bjgcp,jk,gh->bjhpkgc', t, eye_j, eye_g)
        return t.reshape(nblk, S5_PAIRS, LANE, LANE)

    cp = jnp.concatenate([pack_c(c_re), -pack_c(c_im)], axis=-2).astype(BF16)
    cp = jnp.broadcast_to(cp[None], (2,) + cp.shape)

    def pack_l(t):
        return t.reshape(2, nblk, S5_PAIRS, LANE)

    lam = jnp.stack([pack_l(lam_re), pack_l(lam_im)], axis=-2)
    return bp, cp, lam


def _s5_pack_state(h0):
    b, _, g, _, _ = h0.shape
    nblk = g // (2 * S5_PAIRS)
    t = h0.reshape(b, 2, nblk, S5_PAIRS, 2, S5_STATE, 2)
    t = jnp.transpose(t, (1, 2, 3, 0, 6, 4, 5))
    return t.reshape(2, nblk, S5_PAIRS, b, 2 * LANE)


def _s5_unpack_state(fin, b):
    nblk = fin.shape[1]
    t = fin.reshape(2, nblk, S5_PAIRS, b, 2, 2, S5_STATE)
    t = jnp.transpose(t, (3, 0, 1, 2, 5, 6, 4))
    return t.reshape(b, 2, nblk * S5_PAIRS * 2, S5_STATE, 2)


def _cmul(ar, ai, br, bi):
    return ar * br - ai * bi, ar * bi + ai * br


def _gather_tm(src_ref, dst_ref, L, B):
    def body(t, c):
        dst_ref[t] = src_ref[pl.ds(t, B, stride=L), :]
        return c
    lax.fori_loop(0, L, body, 0, unroll=8)


def _scatter_tm(src_ref, dst_ref, L, B):
    def body(t, c):
        dst_ref[pl.ds(t, B, stride=L), :] = src_ref[t]
        return c
    lax.fori_loop(0, L, body, 0, unroll=8)


def _s5_kernel(*refs, L, B, nseg):
    if nseg > 1:
        (u_ref, bp_ref, cp_ref, lam_ref, d_ref, h0_ref, _, y_ref,
         xs_ref, yacc_ref, utm_ref, fl_ref, hp_ref) = refs
    else:
        u_ref, bp_ref, cp_ref, lam_ref, d_ref, y_ref, fin_ref, xs_ref, yacc_ref, utm_ref = refs
    nb = B // nseg
    _gather_tm(u_ref, utm_ref, L, B)
    u2 = utm_ref[...].reshape(L * B, LANE)
    u2b = u2.astype(BF16)
    yacc_ref[...] = jnp.zeros((L * B, LANE), F32)

    for half in range(S5_PAIRS // 2):
        chains = [(d, jj) for d in range(2) for jj in range(2)]
        for d, jj in chains:
            x = jnp.dot(u2b, bp_ref[d, 0, 2 * half + jj], preferred_element_type=F32)
            xs_ref[d, jj] = x.reshape(L, B, 2 * LANE)
        lam = {}
        for d, jj in chains:
            lr = jnp.broadcast_to(lam_ref[d, 0, 2 * half + jj, 0:1, :], (B, LANE))
            li = jnp.broadcast_to(lam_ref[d, 0, 2 * half + jj, 1:2, :], (B, LANE))
            lam[(d, jj)] = (lr, li)

        def step(i, carry):
            out = []
            for n, (d, jj) in enumerate(chains):
                t = i if d == 0 else L - 1 - i
                hr, hi = carry[2 * n], carry[2 * n + 1]
                lr, li = lam[(d, jj)]
                pr, pi = _cmul(lr, li, hr, hi)
                nr = pr + xs_ref[d, jj, t, :, 0:LANE]
                ni = pi + xs_ref[d, jj, t, :, LANE:2 * LANE]
                xs_ref[d, jj, t, :, 0:LANE] = nr
                xs_ref[d, jj, t, :, LANE:2 * LANE] = ni
                out += [nr, ni]
            return tuple(out)

        zero = jnp.zeros((B, LANE), F32)
        carry = lax.fori_loop(0, L, step, (zero,) * (2 * len(chains)), unroll=2)

        if nseg == 1:
            for n, (d, jj) in enumerate(chains):
                fin_ref[d, 0, 2 * half + jj, :, 0:LANE] = carry[2 * n]
                fin_ref[d, 0, 2 * half + jj, :, LANE:2 * LANE] = carry[2 * n + 1]
        else:
            hp = {}
            for n, (d, jj) in enumerate(chains):
                plr = lam_ref[d, 0, 2 * half + jj, 0:1, :]
                pli = lam_ref[d, 0, 2 * half + jj, 1:2, :]
                for _ in range(int(math.log2(L))):
                    plr, pli = _cmul(plr, pli, plr, pli)
                fl_ref[0] = carry[2 * n]
                fl_ref[1] = carry[2 * n + 1]
                order = list(range(nseg)) if d == 0 else list(range(nseg - 1, -1, -1))
                for bb in range(nb):
                    cr = h0_ref[d, 0, 2 * half + jj, bb:bb + 1, 0:LANE]
                    ci = h0_ref[d, 0, 2 * half + jj, bb:bb + 1, LANE:2 * LANE]
                    for k, s in enumerate(order):
                        if k > 0:
                            rp = bb * nseg + order[k - 1]
                            mr, mi = _cmul(plr, pli, cr, ci)
                            cr = fl_ref[0, rp:rp + 1, :] + mr
                            ci = fl_ref[1, rp:rp + 1, :] + mi
                        r = bb * nseg + s
                        hp_ref[n, 0, r:r + 1, :] = cr
                        hp_ref[n, 1, r:r + 1, :] = ci
                hp[(d, jj)] = (hp_ref[n, 0], hp_ref[n, 1])

            def cstep(i, pw):
                out = []
                for n, (d, jj) in enumerate(chains):
                    t = i if d == 0 else L - 1 - i
                    pr, pi = pw[2 * n], pw[2 * n + 1]
                    hr, hi = hp[(d, jj)]
                    ar, ai = _cmul(pr, pi, hr, hi)
                    xs_ref[d, jj, t, :, 0:LANE] = xs_ref[d, jj, t, :, 0:LANE] + ar
                    xs_ref[d, jj, t, :, LANE:2 * LANE] = xs_ref[d, jj, t, :, LANE:2 * LANE] + ai
                    lr, li = lam[(d, jj)]
                    nr, ni = _cmul(lr, li, pr, pi)
                    out += [nr, ni]
                return tuple(out)

            pw0 = []
            for d, jj in chains:
                pw0 += list(lam[(d, jj)])
            lax.fori_loop(0, L, cstep, tuple(pw0), unroll=2)

        for d, jj in chains:
            hs = xs_ref[d, jj].reshape(L * B, 2 * LANE).astype(BF16)
            yacc_ref[...] += jnp.dot(hs, cp_ref[d, 0, 2 * half + jj], preferred_element_type=F32)

    y = u2 * d_ref[...] + yacc_ref[...]
    utm_ref[...] = jax.nn.gelu(y).reshape(L, B, LANE)
    _scatter_tm(utm_ref, y_ref, L, B)


def _s5_call(src, col0, bp, cp, lam, d_skip, h0, y_prev, *, L, B, nseg, rblk, n_rows):
    w = d_skip.shape[-1]
    nblk = w // LANE
    cblk0 = col0 // LANE
    in_specs = [
        pl.BlockSpec((L * B, LANE), lambda k: (rblk, cblk0 + k)),
        pl.BlockSpec((2, 1, S5_PAIRS, LANE, 2 * LANE), lambda k: (0, k, 0, 0, 0)),
        pl.BlockSpec((2, 1, S5_PAIRS, 2 * LANE, LANE), lambda k: (0, k, 0, 0, 0)),
        pl.BlockSpec((2, 1, S5_PAIRS, 2, LANE), lambda k: (0, k, 0, 0, 0)),
        pl.BlockSpec((1, LANE), lambda k: (0, k)),
    ]
    args = [src, bp, cp, lam, d_skip.reshape(1, w)]
    y_shape = jax.ShapeDtypeStruct((n_rows, w), F32)
    y_spec = pl.BlockSpec((L * B, LANE), lambda k: (rblk, k))
    scratch = [pltpu.VMEM((2, 2, L, B, 2 * LANE), F32), pltpu.VMEM((L * B, LANE), F32),
               pltpu.VMEM((L, B, LANE), F32)]
    aliases = {}
    if nseg > 1:
        nb = B // nseg
        in_specs += [pl.BlockSpec((2, 1, S5_PAIRS, nb, 2 * LANE), lambda k: (0, k, 0, 0, 0)),
                     pl.BlockSpec(memory_space=pl.ANY)]
        args += [h0, y_prev]
        aliases = {len(args) - 1: 0}
        out_shape, out_specs = y_shape, y_spec
        scratch += [pltpu.VMEM((2, B, LANE), F32), pltpu.VMEM((4, 2, B, LANE), F32)]
    else:
        out_shape = [y_shape, jax.ShapeDtypeStruct((2, nblk, S5_PAIRS, B, 2 * LANE), F32)]
        out_specs = [y_spec, pl.BlockSpec((2, 1, S5_PAIRS, B, 2 * LANE), lambda k: (0, k, 0, 0, 0))]
    return pl.pallas_call(
        functools.partial(_s5_kernel, L=L, B=B, nseg=nseg),
        out_shape=out_shape,
        grid=(nblk,),
        in_specs=in_specs,
        out_specs=out_specs,
        scratch_shapes=scratch,
        input_output_aliases=aliases,
        compiler_params=_cparams("arbitrary"),
        name="s5_scan_seg" if nseg > 1 else "s5_scan",
    )(*args)


def _glu_kernel(yrow_ref, ytile_ref, w_ref, b_ref, g_ref, o_ref):
    w = w_ref[...].astype(BF16)
    z = jnp.dot(yrow_ref[...].astype(BF16), w, preferred_element_type=F32) + b_ref[...]
    o_ref[...] = (ytile_ref[...] * _sigmoid(z) * _silu(g_ref[...])).astype(BF16)


def _glu_call(y, w_glu, b_glu3, proj, l):
    goff = C_GA // TN
    return pl.pallas_call(
        _glu_kernel,
        out_shape=jax.ShapeDtypeStruct((N_TOK, W_BR), BF16),
        grid=(N_TOK // TM, W_BR // TN),
        in_specs=[
            pl.BlockSpec((TM, W_BR), lambda i, j: (i, 0)),
            pl.BlockSpec((TM, TN), lambda i, j: (i, j)),
            pl.BlockSpec((None, W_BR, TN), lambda i, j: (l, 0, j)),
            pl.BlockSpec((None, 1, TN), lambda i, j: (l, 0, j)),
            pl.BlockSpec((TM, TN), lambda i, j: (i, goff + j)),
        ],
        out_specs=pl.BlockSpec((TM, TN), lambda i, j: (i, j)),
        compiler_params=_cparams("arbitrary", "arbitrary"),
        name="s5_glu",
    )(y, y, w_glu, b_glu3, proj)


def _expm1(x):
    t = jnp.tanh(0.5 * x)
    return 2.0 * t / (1.0 - t)


def _lru_kernel(*refs, L, B, nseg, period):
    if nseg > 1:
        (x_ref, g_ref, cw_ref, cb_ref, wa_ref, ba_ref, wx_ref, bx_ref, sp_ref, h0_ref, _,
         y_ref, a_ref, b_ref, ytok_ref, fl_ref, hp_ref) = refs
    else:
        (x_ref, g_ref, cw_ref, cb_ref, wa_ref, ba_ref, wx_ref, bx_ref, sp_ref,
         y_ref, fin_ref, a_ref, b_ref, ytok_ref) = refs
    nb = B // nseg
    _gather_tm(x_ref, a_ref.at[0], L, B)
    x = a_ref[0]
    pos = lax.broadcasted_iota(jnp.int32, (L, B, LANE), 0) % period
    xc = jnp.broadcast_to(cb_ref[...].reshape(1, 1, LANE), (L, B, LANE))
    for k in range(CONV_W):
        off = k - CONV_W // 2
        if off < 0:
            xs = jnp.concatenate([jnp.zeros((-off, B, LANE), F32), x[:L + off]], axis=0)
        elif off > 0:
            xs = jnp.concatenate([x[off:], jnp.zeros((off, B, LANE), F32)], axis=0)
        else:
            xs = x
        valid = (pos + off >= 0) & (pos + off < period)
        xc = xc + cw_ref[k:k + 1, :].reshape(1, 1, LANE) * jnp.where(valid, xs, 0.0)
    xc2 = xc.reshape(L * B, LANE)
    xcb = xc2.astype(BF16)
    for d in range(2):
        r = _sigmoid(jnp.dot(xcb, wa_ref[d, 0].astype(BF16), preferred_element_type=F32)
                     + ba_ref[d:d + 1, :])
        ig = _sigmoid(jnp.dot(xcb, wx_ref[d, 0].astype(BF16), preferred_element_type=F32)
                      + bx_ref[d:d + 1, :])
        log_a = (-LRU_C) * r * sp_ref[d:d + 1, :]
        a_ref[d] = jnp.exp(log_a).reshape(L, B, LANE)
        mult = jnp.sqrt(-_expm1(2.0 * log_a))
        b_ref[d] = (mult * (ig * xc2)).reshape(L, B, LANE)

    def step(i, carry):
        out = []
        for d in range(2):
            t = i if d == 0 else L - 1 - i
            a = a_ref[d, t]
            h = a * carry[2 * d] + b_ref[d, t]
            b_ref[d, t] = h
            if nseg > 1:
                p = a * carry[2 * d + 1]
                a_ref[d, t] = p
            else:
                p = carry[2 * d + 1]
            out += [h, p]
        return tuple(out)

    zero = jnp.zeros((B, LANE), F32)
    one = jnp.ones((B, LANE), F32)
    carry = lax.fori_loop(0, L, step, (zero, one, zero, one), unroll=2)

    if nseg == 1:
        fin_ref[0] = carry[0]
        fin_ref[1] = carry[2]
    else:
        hps = []
        for d in range(2):
            fl_ref[0] = carry[2 * d]
            fl_ref[1] = carry[2 * d + 1]
            order = list(range(nseg)) if d == 0 else list(range(nseg - 1, -1, -1))
            for bb in range(nb):
                c = h0_ref[d, bb:bb + 1, :]
                for k, s in enumerate(order):
                    if k > 0:
                        rp = bb * nseg + order[k - 1]
                        c = fl_ref[0, rp:rp + 1, :] + fl_ref[1, rp:rp + 1, :] * c
                    r = bb * nseg + s
                    hp_ref[d, r:r + 1, :] = c
            hps.append(hp_ref[d])

        def cstep(i, c):
            for d in range(2):
                t = i if d == 0 else L - 1 - i
                b_ref[d, t] = b_ref[d, t] + a_ref[d, t] * hps[d]
            return c

        lax.fori_loop(0, L, cstep, 0, unroll=2)

    a_ref[0] = b_ref[0] + b_ref[1]
    _scatter_tm(a_ref.at[0], ytok_ref, L, B)
    y_ref[...] = (ytok_ref[...] * _silu(g_ref[...])).astype(BF16)


def _lru_call(src, col_x, col_g, conv_w, conv_b, w_a, b_a, w_x, b_x, sp, h0, y_prev,
              *, L, B, nseg, period, rblk, n_rows):
    w = conv_b.shape[-1]
    nblk = w // LANE

    def rows(col0):
        return pl.BlockSpec((L * B, LANE), lambda k: (rblk, col0 // LANE + k))

    y_spec = pl.BlockSpec((L * B, LANE), lambda k: (rblk, k))
    in_specs = [
        rows(col_x), rows(col_g),
        pl.BlockSpec((CONV_W, LANE), lambda k: (0, k)),
        pl.BlockSpec((1, LANE), lambda k: (0, k)),
        pl.BlockSpec((2, 1, LANE, LANE), lambda k: (0, k, 0, 0)),
        pl.BlockSpec((2, LANE), lambda k: (0, k)),
        pl.BlockSpec((2, 1, LANE, LANE), lambda k: (0, k, 0, 0)),
        pl.BlockSpec((2, LANE), lambda k: (0, k)),
        pl.BlockSpec((2, LANE), lambda k: (0, k)),
    ]
    args = [src, src, conv_w, conv_b.reshape(1, w), w_a, b_a, w_x, b_x, sp]
    y_shape = jax.ShapeDtypeStruct((n_rows, w), BF16)
    scratch = [pltpu.VMEM((2, L, B, LANE), F32), pltpu.VMEM((2, L, B, LANE), F32),
               pltpu.VMEM((L * B, LANE), F32)]
    aliases = {}
    if nseg > 1:
        nb = B // nseg
        in_specs += [pl.BlockSpec((2, nb, LANE), lambda k: (0, 0, k)), pl.BlockSpec(memory_space=pl.ANY)]
        args += [h0, y_prev]
        aliases = {len(args) - 1: 0}
        out_shape, out_specs = y_shape, y_spec
        scratch += [pltpu.VMEM((2, B, LANE), F32), pltpu.VMEM((2, B, LANE), F32)]
    else:
        out_shape = [y_shape, jax.ShapeDtypeStruct((2, B, w), F32)]
        out_specs = [y_spec, pl.BlockSpec((2, B, LANE), lambda k: (0, 0, k))]
    return pl.pallas_call(
        functools.partial(_lru_kernel, L=L, B=B, nseg=nseg, period=period),
        out_shape=out_shape,
        grid=(nblk,),
        in_specs=in_specs,
        out_specs=out_specs,
        scratch_shapes=scratch,
        input_output_aliases=aliases,
        compiler_params=_cparams("arbitrary"),
        name="rglru_seg" if nseg > 1 else "rglru",
    )(*args)


def _nt_dot(a, b):
    return lax.dot_general(a, b, (((1,), (1,)), ((), ())), preferred_element_type=F32)


def _hgrn_kernel(*refs, L, has_h0):
    if has_h0:
        q_ref, ff_ref, fb_ref, v_ref, g_ref, lb_ref, nw_ref, h0_ref, _, y_ref, fin_ref, of_ref = refs
    else:
        q_ref, ff_ref, fb_ref, v_ref, g_ref, lb_ref, nw_ref, y_ref, fin_ref, of_ref = refs
    rb_rows = HG_RB
    nrb = L // rb_rows
    nsub = HG_ST // HG_CHUNK
    nch = rb_rows // HG_ST
    row = lax.broadcasted_iota(jnp.int32, (rb_rows, rb_rows), 0)
    col = lax.broadcasted_iota(jnp.int32, (rb_rows, rb_rows), 1)
    sub_dist = jnp.where((row // HG_ST) == (col // HG_ST), row // HG_CHUNK - col // HG_CHUNK, 2 * nsub)
    ridx = lax.broadcasted_iota(jnp.int32, (rb_rows, HG_DK), 0)
    pos = ridx % HG_CHUNK
    sic = (ridx // HG_CHUNK) % nsub
    chunk_id = ridx // HG_ST

    def sub_bcast(x, idx):
        x3 = x.reshape(rb_rows // HG_CHUNK, HG_CHUNK, HG_DK)
        return jnp.broadcast_to(x3[:, idx:idx + 1, :], x3.shape).reshape(rb_rows, HG_DK)

    def chunk_bcast(x, idx):
        x3 = x.reshape(nch, HG_ST, HG_DK)
        return jnp.broadcast_to(x3[:, idx:idx + 1, :], x3.shape).reshape(rb_rows, HG_DK)

    def run_block(rb, st, d):
        r0 = pl.multiple_of(rb * rb_rows, rb_rows)
        q = q_ref[pl.ds(r0, rb_rows), :]
        qf = _silu(q)
        f = (ff_ref if d == 0 else fb_ref)[pl.ds(r0, rb_rows), :]
        lbd = lb_ref[d:d + 1, :]
        g = lbd + (1.0 - lbd) * _sigmoid(f)
        kk = 1.0 - g
        b = jnp.log(g)
        v = v_ref[pl.ds(r0, rb_rows), :]
        vb = v.astype(BF16)
        sgn = 1 if d == 0 else -1

        def shift(x, n):
            return pltpu.roll(x, n if d == 0 else rb_rows - n, 0)

        s = 1
        while s < HG_CHUNK:
            keep = (pos >= s) if d == 0 else (pos <= HG_CHUNK - 1 - s)
            b = b + jnp.where(keep, shift(b, s), 0.0)
            s *= 2
        tot = sub_bcast(b, HG_CHUNK - 1 if d == 0 else 0)
        prev = [shift(tot, HG_CHUNK * n) for n in range(1, nsub)]
        seen = sic if d == 0 else nsub - 1 - sic
        base = sum(jnp.where(seen >= n, prev[n - 1], 0.0) for n in range(1, nsub))
        ctot = chunk_bcast(b + base, HG_ST - 1 if d == 0 else 0)

        ed = jnp.exp(b - sub_bcast(b, HG_CHUNK // 2))
        sc0 = _nt_dot((qf * ed).astype(BF16), (kk / ed).astype(BF16))
        qe = qf * jnp.exp(b)
        kd = kk * jnp.exp(tot - b)
        lhs = [qe]
        acc = None
        for n in range(1, nsub - 1):
            acc = prev[n - 1] if acc is None else acc + prev[n - 1]
            lhs.append(qe * jnp.exp(acc))
        scn = _nt_dot(jnp.concatenate(lhs, axis=0).astype(BF16), kd.astype(BF16))
        causal = (col <= row) if d == 0 else (col >= row)
        sc = jnp.where((sub_dist == 0) & causal, sc0, 0.0)
        for n in range(1, nsub):
            sc = jnp.where(sub_dist == sgn * n, scn[(n - 1) * rb_rows:n * rb_rows], sc)
        o = jnp.dot(sc.astype(BF16), vb, preferred_element_type=F32)

        qe_st = (qe * jnp.exp(base)).astype(BF16)
        kd_st = kd * jnp.exp(ctot - base - tot)
        vt = v.T.astype(BF16)
        uts = []
        for c in range(0, nch, 2):
            rhs = jnp.concatenate([jnp.where(chunk_id == c, kd_st, 0.0),
                                   jnp.where(chunk_id == c + 1, kd_st, 0.0)], axis=1).astype(BF16)
            ut2 = jnp.dot(vt, rhs, preferred_element_type=F32)
            uts += [ut2[:, :HG_DK], ut2[:, HG_DK:]]
        order = range(nch) if d == 0 else range(nch - 1, -1, -1)
        entering = [None] * nch
        for c in order:
            entering[c] = st
            tot_row = c * HG_ST + (HG_ST - 1 if d == 0 else 0)
            st = st * jnp.exp(ctot[tot_row:tot_row + 1, :]) + uts[c]
        outs = []
        for c in range(nch):
            lo = c * HG_ST
            oi = _nt_dot(qe_st[lo:lo + HG_ST], entering[c].astype(BF16))
            outs.append(o[lo:lo + HG_ST] + oi)
        return jnp.concatenate(outs, axis=0), st

    for d in range(2):
        if has_h0:
            st0 = h0_ref[0, d, 0].T
        else:
            st0 = jnp.zeros((HG_DK, HG_DK), F32)

        def body(i, st, d=d):
            rb = i if d == 0 else nrb - 1 - i
            oblk, st = run_block(rb, st, d)
            r0 = pl.multiple_of(rb * rb_rows, rb_rows)
            if d == 0:
                of_ref[pl.ds(r0, rb_rows), :] = oblk
            else:
                of_ref[pl.ds(r0, rb_rows), :] = of_ref[pl.ds(r0, rb_rows), :] + oblk
            return st

        st = lax.fori_loop(0, nrb, body, st0)
        fin_ref[0, d, 0] = st.T

    o = of_ref[...]
    o = o * lax.rsqrt(jnp.mean(o * o, axis=-1, keepdims=True) + RMS_EPS)
    y_ref[...] = (o * nw_ref[...] * _silu(g_ref[...])).astype(BF16)


def _hgrn_call(proj, lb, norm_w, h0, y_prev, *, L, nb, row0, n_rows, cols=(C_QC, C_FF, C_FB, C_IC, C_GC)):
    w = lb.shape[-1]
    nh = w // HG_DK
    rblk0 = row0 // L

    def col(off):
        return pl.BlockSpec((L, HG_DK), lambda b, h: (rblk0 + b, off // HG_DK + h))

    in_specs = [col(c) for c in cols] + [
                pl.BlockSpec((2, HG_DK), lambda b, h: (0, h)),
                pl.BlockSpec((1, HG_DK), lambda b, h: (0, h))]
    args = [proj] * 5 + [lb, norm_w.reshape(1, w)]
    st_spec = pl.BlockSpec((1, 2, 1, HG_DK, HG_DK), lambda b, h: (b, 0, h, 0, 0))
    aliases = {}
    if h0 is not None:
        in_specs += [st_spec, pl.BlockSpec(memory_space=pl.ANY)]
        args += [h0, y_prev]
        aliases = {len(args) - 1: 0}
    return pl.pallas_call(
        functools.partial(_hgrn_kernel, L=L, has_h0=h0 is not None),
        out_shape=[jax.ShapeDtypeStruct((n_rows, w), BF16),
                   jax.ShapeDtypeStruct((nb, 2, nh, HG_DK, HG_DK), F32)],
        grid=(nb, nh),
        in_specs=in_specs,
        out_specs=[pl.BlockSpec((L, HG_DK), lambda b, h: (rblk0 + b, h)), st_spec],
        scratch_shapes=[pltpu.VMEM((L, HG_DK), F32)],
        input_output_aliases=aliases,
        compiler_params=_cparams("arbitrary", "arbitrary"),
        name="hgrn2",
    )(*args)


def _merge_kernel(ya_ref, yb_ref, yc_ref, wa_ref, wb_ref, wc_ref, ma_ref, mb_ref, mc_ref, o_ref):
    acc = None
    for y_ref, w_ref, m_ref in ((ya_ref, wa_ref, ma_ref), (yb_ref, wb_ref, mb_ref), (yc_ref, wc_ref, mc_ref)):
        t = jnp.dot(y_ref[...], w_ref[...].astype(BF16), preferred_element_type=F32)
        t = _sigmoid(m_ref[...]) * t
        acc = t if acc is None else acc + t
    o_ref[...] = acc.astype(BF16)


def _merge_call(ya, yb, yc, w_br, proj, l):
    moff = C_MA // TN
    mstep = D_MODEL // TN
    tm = TM_MERGE
    lhs = pl.BlockSpec((tm, W_BR), lambda i, j: (i, 0))

    def wspec(br):
        return pl.BlockSpec((None, None, W_BR, TN), lambda i, j: (l, br, 0, j))

    def mspec(br):
        return pl.BlockSpec((tm, TN), lambda i, j: (i, moff + br * mstep + j))

    return pl.pallas_call(
        _merge_kernel,
        out_shape=jax.ShapeDtypeStruct((N_TOK, D_MODEL), BF16),
        grid=(N_TOK // tm, D_MODEL // TN),
        in_specs=[lhs, lhs, lhs, wspec(0), wspec(1), wspec(2), mspec(0), mspec(1), mspec(2)],
        out_specs=pl.BlockSpec((tm, TN), lambda i, j: (i, j)),
        compiler_params=_cparams("arbitrary", "arbitrary"),
        name="branch_merge",
    )(ya, yb, yc, w_br, w_br, w_br, proj, proj, proj)


def _outproj_kernel(m_ref, w_ref, b_ref, x_ref, gate_ref, o_ref):
    out = jnp.dot(m_ref[...], w_ref[...].astype(BF16), preferred_element_type=F32) + b_ref[...]
    o_ref[...] = DN_ALPHA * x_ref[...] + gate_ref[0] * out


def _outproj_call(merged, w_out, b_out3, x, mods3, l):
    goff = 2 * D_MODEL // TN
    return pl.pallas_call(
        _outproj_kernel,
        out_shape=jax.ShapeDtypeStruct((N_TOK, D_MODEL), F32),
        grid=(N_TOK // TM, D_MODEL // TN),
        in_specs=[
            pl.BlockSpec((TM, D_MODEL), lambda i, j: (i, 0)),
            pl.BlockSpec((None, D_MODEL, TN), lambda i, j: (l, 0, j)),
            pl.BlockSpec((None, 1, TN), lambda i, j: (l, 0, j)),
            pl.BlockSpec((TM, TN), lambda i, j: (i, j)),
            pl.BlockSpec((1, 1, TN), lambda i, j: (_mod_row_mm(i), 0, goff + j)),
        ],
        out_specs=pl.BlockSpec((TM, TN), lambda i, j: (i, j)),
        compiler_params=_cparams("arbitrary", "arbitrary"),
        name="out_proj",
    )(merged, w_out, b_out3, x, mods3)


def _ln_affine_kernel(r_ref, g_ref, b_ref, o_ref):
    r = r_ref[...]
    mu = jnp.mean(r, axis=-1, keepdims=True)
    rc = r - mu
    var = jnp.mean(rc * rc, axis=-1, keepdims=True)
    o_ref[...] = rc * lax.rsqrt(var + LN_EPS) * g_ref[...] + b_ref[...]


def _ln_affine_call(r, g, b, l, row0=0, n_rows=N_TOK):
    rblk0 = row0 // TM_LN
    return pl.pallas_call(
        _ln_affine_kernel,
        out_shape=jax.ShapeDtypeStruct((n_rows, D_MODEL), F32),
        grid=(n_rows // TM_LN,),
        in_specs=[
            pl.BlockSpec((TM_LN, D_MODEL), lambda i: (rblk0 + i, 0)),
            pl.BlockSpec((None, 1, D_MODEL), lambda i: (l, 0, 0)),
            pl.BlockSpec((None, 1, D_MODEL), lambda i: (l, 0, 0)),
        ],
        out_specs=pl.BlockSpec((TM_LN, D_MODEL), lambda i: (i, 0)),
        compiler_params=_cparams("arbitrary"),
        name="post_ln",
    )(r, g, b)


def kernel(x_prompt, x_sample, state_s5, state_lru, state_hgrn, c, c_ctx, w_ada, b_ada, w_in, b_in,
           s5_a_re, s5_a_im, s5_log_dt, s5_b_re, s5_b_im, s5_c_re, s5_c_im, s5_d, s5_w_glu, s5_b_glu,
           lru_conv_w, lru_conv_b, lru_w_a, lru_b_a, lru_w_x, lru_b_x, lru_lambda, hg_lb, hg_norm_w,
           w_br, w_out, b_out, ln_g, ln_b):
    lb_soft = jax.nn.softmax(hg_lb.astype(F32), axis=0)
    lb_all = jnp.cumsum(lb_soft, axis=0) - lb_soft[0]
    softplus_neg_lam = jax.nn.softplus(-lru_lambda.astype(F32))

    cvec = jnp.zeros((SUBLANE, D_MODEL), F32)
    cvec = cvec.at[0].set(c_ctx).at[1:1 + DEC_BATCH].set(c)
    mods = _mods_call(cvec, w_ada, b_ada)

    b_in3 = b_in.reshape(DEPTH, 1, IN_COLS)
    b_glu3 = s5_b_glu.reshape(DEPTH, 1, W_BR)
    b_out3 = b_out.reshape(DEPTH, 1, D_MODEL)
    ln_g3 = ln_g.reshape(DEPTH, 1, D_MODEL)
    ln_b3 = ln_b.reshape(DEPTH, 1, D_MODEL)

    x = jnp.concatenate([x_prompt.reshape(N_CTX, D_MODEL), x_sample.reshape(N_LAT, D_MODEL)], axis=0)
    st_s5, st_lru, st_hg = [], [], []
    for l in range(DEPTH):
        mods3 = mods[l, :1 + DEC_BATCH].reshape(1 + DEC_BATCH, 1, 3 * D_MODEL)
        h = _ln_mod_call(x, mods3)
        proj = _inproj_call(h, w_in, b_in3, l)

        lam_re, lam_im, z_re, z_im = _s5_disc_call(s5_a_re[l], s5_a_im[l], s5_log_dt[l])
        bp, cp, lam = _s5_pack(lam_re, lam_im, z_re, z_im, s5_b_re[l], s5_b_im[l], s5_c_re[l], s5_c_im[l])
        ctx_rows = dict(L=SEQ, B=BATCH, nseg=1, rblk=0, n_rows=N_TOK)
        lat_rows = dict(L=SEQ, B=LAT_ROWS, nseg=LAT_SEG, rblk=N_CTX // N_LAT, n_rows=N_TOK)
        y_pre, fin_s5 = _s5_call(proj, C_UA, bp, cp, lam, s5_d[l], None, None, **ctx_rows)
        y_pre = _s5_call(proj, C_UA, bp, cp, lam, s5_d[l], _s5_pack_state(state_s5[:, l]), y_pre, **lat_rows)
        y_a = _glu_call(y_pre, s5_w_glu, b_glu3, proj, l)
        st_s5.append(_s5_unpack_state(fin_s5, BATCH))

        lru_args = (lru_conv_w[l], lru_conv_b[l], lru_w_a[l], lru_b_a[l], lru_w_x[l], lru_b_x[l],
                    softplus_neg_lam[l])
        y_b, fin_lru = _lru_call(proj, C_XB, C_GB, *lru_args, None, None, period=SEQ, **ctx_rows)
        y_b = _lru_call(proj, C_XB, C_GB, *lru_args, jnp.transpose(state_lru[:, l], (1, 0, 2)), y_b,
                        period=GRID_W, **lat_rows)
        st_lru.append(jnp.transpose(fin_lru, (1, 0, 2)))

        y_c, fin_hg = _hgrn_call(proj, lb_all[l], hg_norm_w[l], None, None,
                                 L=SEQ, nb=BATCH, row0=0, n_rows=N_TOK)
        y_c, _ = _hgrn_call(proj, lb_all[l], hg_norm_w[l], state_hgrn[:, l], y_c,
                            L=DEC_SEQ, nb=DEC_BATCH, row0=N_CTX, n_rows=N_TOK)
        st_hg.append(fin_hg)

        merged = _merge_call(y_a, y_b, y_c, w_br, proj, l)
        r = _outproj_call(merged, w_out, b_out3, x, mods3, l)
        if l < DEPTH - 1:
            x = _ln_affine_call(r, ln_g3, ln_b3, l)

    y_prompt = _ln_affine_call(r, ln_g3, ln_b3, DEPTH - 1, 0, N_CTX).reshape(BATCH, SEQ, D_MODEL)
    y_sample = _ln_affine_call(r, ln_g3, ln_b3, DEPTH - 1, N_CTX, N_LAT).reshape(DEC_BATCH, DEC_SEQ, D_MODEL)
    new_state_s5 = jnp.stack(st_s5, axis=1)
    new_state_lru = jnp.stack(st_lru, axis=1)
    new_state_hgrn = jnp.stack(st_hg, axis=1)
    return (y_prompt, y_sample, new_state_s5, new_state_lru, new_state_hgrn)
```

```python
import functools
import math

import jax
import jax.numpy as jnp
from jax import lax
from jax.experimental import pallas as pl
from jax.experimental.pallas import tpu as pltpu

F32 = jnp.float32
BF16 = jnp.bfloat16

LANE = 128
SUBLANE = 8
VMEM_LIMIT = 56 * 1024 * 1024

D_MODEL = 4096
DEPTH = 2
BATCH, SEQ = 16, 256
DEC_BATCH, DEC_SEQ = 2, 1024
GRID_W = 64
W_BR = D_MODEL // 2
S5_GROUP = 16
S5_STATE = 64
LRU_C = 8.0
CONV_W = 4
HG_DK = 128
HG_CHUNK = 16
N_BRANCH = 3
IN_COLS = 9 * W_BR + N_BRANCH * D_MODEL
DN_ALPHA = (2 * DEPTH) ** 0.25
LN_EPS = 1e-5
RMS_EPS = 1e-6

N_CTX = BATCH * SEQ
N_LAT = DEC_BATCH * DEC_SEQ
N_TOK = N_CTX + N_LAT
LAT_SEG = DEC_SEQ // SEQ
LAT_ROWS = LAT_SEG * DEC_BATCH

C_UA, C_GA, C_XB, C_GB, C_QC, C_FF, C_FB, C_IC, C_GC = (i * W_BR for i in range(9))
C_MA = 9 * W_BR

TM = 1024
TN = 512
TM_MERGE = 1024
TN_MERGE = 256
TM_LN = 256
HG_RB = 256
HG_ST = 64
S5_PAIRS = 4


def _cparams(*sem):
    return pltpu.CompilerParams(dimension_semantics=sem, vmem_limit_bytes=VMEM_LIMIT)


def _sigmoid(x):
    return jax.nn.sigmoid(x)


def _silu(x):
    return x * jax.nn.sigmoid(x)


def _mods_kernel(c_ref, w_ref, b_ref, o_ref):
    c = c_ref[...]
    s = _silu(c).astype(BF16)
    w = w_ref[...].astype(BF16)
    o_ref[...] = jnp.dot(s, w, preferred_element_type=F32) + b_ref[...]


def _mods_call(cvec, w_ada, b_ada):
    tn = TN
    return pl.pallas_call(
        _mods_kernel,
        out_shape=jax.ShapeDtypeStruct((DEPTH, SUBLANE, 3 * D_MODEL), F32),
        grid=(DEPTH, 3 * D_MODEL // tn),
        in_specs=[
            pl.BlockSpec((SUBLANE, D_MODEL), lambda l, j: (0, 0)),
            pl.BlockSpec((None, D_MODEL, tn), lambda l, j: (l, 0, j)),
            pl.BlockSpec((None, 1, tn), lambda l, j: (l, 0, j)),
        ],
        out_specs=pl.BlockSpec((None, SUBLANE, tn), lambda l, j: (l, 0, j)),
        compiler_params=_cparams("arbitrary", "arbitrary"),
        name="adaln_mods",
    )(cvec, w_ada, b_ada.reshape(DEPTH, 1, 3 * D_MODEL))


def _mod_row_ln(i):
    n_ctx_tiles = N_CTX // TM_LN
    return jnp.where(i < n_ctx_tiles, 0, 1 + (i - n_ctx_tiles) // (DEC_SEQ // TM_LN))


def _mod_row_mm(i):
    return jnp.maximum(i - (N_CTX // TM - 1), 0)


def _ln_mod_kernel(xc_ref, xl_ref, shift_ref, scale_ref, o_ref):
    def emit(x_ref):
        x = x_ref[...]
        mu = jnp.mean(x, axis=-1, keepdims=True)
        xc = x - mu
        var = jnp.mean(xc * xc, axis=-1, keepdims=True)
        h = xc * lax.rsqrt(var + LN_EPS) * (1.0 + scale_ref[0]) + shift_ref[0]
        o_ref[...] = h.astype(BF16)

    is_ctx = pl.program_id(0) < N_CTX // TM_LN
    pl.when(is_ctx)(lambda: emit(xc_ref))
    pl.when(jnp.logical_not(is_ctx))(lambda: emit(xl_ref))


def _split_rows(tile, n_col_axes):
    n_ctx_tiles = N_CTX // tile
    if n_col_axes == 0:
        return (lambda i: (jnp.minimum(i, n_ctx_tiles - 1), 0),
                lambda i: (jnp.maximum(i - n_ctx_tiles, 0), 0))
    return (lambda i, j: (jnp.minimum(i, n_ctx_tiles - 1), j),
            lambda i, j: (jnp.maximum(i - n_ctx_tiles, 0), j))


def _ln_mod_call(x_ctx, x_lat, mods3):
    ctx_map, lat_map = _split_rows(TM_LN, 0)
    return pl.pallas_call(
        _ln_mod_kernel,
        out_shape=jax.ShapeDtypeStruct((N_TOK, D_MODEL), BF16),
        grid=(N_TOK // TM_LN,),
        in_specs=[
            pl.BlockSpec((TM_LN, D_MODEL), ctx_map),
            pl.BlockSpec((TM_LN, D_MODEL), lat_map),
            pl.BlockSpec((1, 1, D_MODEL), lambda i: (_mod_row_ln(i), 0, 0)),
            pl.BlockSpec((1, 1, D_MODEL), lambda i: (_mod_row_ln(i), 0, 1)),
        ],
        out_specs=pl.BlockSpec((TM_LN, D_MODEL), lambda i: (i, 0)),
        compiler_params=_cparams("arbitrary"),
        name="ln_modulate",
    )(x_ctx, x_lat, mods3, mods3)


def _inproj_kernel(h_ref, w_ref, b_ref, o_ref):
    w = w_ref[...].astype(BF16)
    o_ref[...] = jnp.dot(h_ref[...], w, preferred_element_type=F32) + b_ref[...]


def _inproj_call(h, w_in, b_in3, l):
    return pl.pallas_call(
        _inproj_kernel,
        out_shape=jax.ShapeDtypeStruct((N_TOK, IN_COLS), F32),
        grid=(N_TOK // TM, IN_COLS // TN),
        in_specs=[
            pl.BlockSpec((TM, D_MODEL), lambda i, j: (i, 0)),
            pl.BlockSpec((None, D_MODEL, TN), lambda i, j: (l, 0, j)),
            pl.BlockSpec((None, 1, TN), lambda i, j: (l, 0, j)),
        ],
        out_specs=pl.BlockSpec((TM, TN), lambda i, j: (i, j)),
        compiler_params=_cparams("arbitrary", "arbitrary"),
        name="in_proj",
    )(h, w_in, b_in3)


def _s5_disc_kernel(are_ref, aim_ref, ldt_ref, lre_ref, lim_ref, zre_ref, zim_ref):
    ar = jnp.minimum(are_ref[...], -1e-4)
    ai = aim_ref[...]
    dt = jnp.exp(ldt_ref[...])
    mag = jnp.exp(dt * ar)
    lam_re = mag * jnp.cos(dt * ai)
    lam_im = mag * jnp.sin(dt * ai)
    den = ar * ar + ai * ai
    lre_ref[...] = lam_re
    lim_ref[...] = lam_im
    zre_ref[...] = ((lam_re - 1.0) * ar + lam_im * ai) / den
    zim_ref[...] = (lam_im * ar - (lam_re - 1.0) * ai) / den


def _s5_disc_call(a_re, a_im, log_dt):
    g = a_re.shape[1]
    shp = (2 * g, S5_STATE)
    ldt = jnp.broadcast_to(log_dt[..., None], (2, g, S5_STATE)).reshape(shp)
    outs = pl.pallas_call(
        _s5_disc_kernel,
        out_shape=[jax.ShapeDtypeStruct(shp, F32)] * 4,
        name="s5_discretise",
    )(a_re.reshape(shp), a_im.reshape(shp), ldt)
    return [o.reshape(2, g, S5_STATE) for o in outs]


def _s5_pack(lam_re, lam_im, z_re, z_im, b_re, b_im, c_re, c_im):
    g = lam_re.shape[1]
    nblk = g // (2 * S5_PAIRS)
    pair_rows = 2 * S5_GROUP
    bz_re = z_re[..., None] * b_re[None] - z_im[..., None] * b_im[None]
    bz_im = z_re[..., None] * b_im[None] + z_im[..., None] * b_re[None]

    def pairs(t):
        t = t.reshape(t.shape[:-3] + (g // 2, 2) + t.shape[-2:])
        return t[..., 0, :, :], t[..., 1, :, :]

    def embed(t, axis):
        t = t.reshape(t.shape[:-3] + (nblk, S5_PAIRS) + t.shape[-2:])
        out = []
        for j in range(S5_PAIRS):
            pad = [(0, 0)] * (t.ndim - 1)
            pad[axis] = (j * pair_rows, LANE - (j + 1) * pair_rows)
            out.append(jnp.pad(t[..., j, :, :], pad))
        return jnp.stack(out, axis=-3)

    br0, br1 = pairs(jnp.swapaxes(bz_re, -1, -2))
    bi0, bi1 = pairs(jnp.swapaxes(bz_im, -1, -2))
    zb = jnp.zeros_like(br0)
    bp = jnp.concatenate([jnp.concatenate([br0, zb, bi0, zb], axis=-1),
                          jnp.concatenate([zb, br1, zb, bi1], axis=-1)], axis=-2)
    bp = embed(bp, -2).astype(BF16)

    cr0, cr1 = pairs(jnp.swapaxes(c_re, -1, -2))
    ci0, ci1 = pairs(jnp.swapaxes(c_im, -1, -2))
    zc = jnp.zeros_like(cr0)
    cp = jnp.concatenate([jnp.concatenate([cr0, zc], axis=-1), jnp.concatenate([zc, cr1], axis=-1),
                          jnp.concatenate([-ci0, zc], axis=-1), jnp.concatenate([zc, -ci1], axis=-1)],
                         axis=-2)
    cp = embed(cp, -1).astype(BF16)
    cp = jnp.broadcast_to(cp[None], (2,) + cp.shape)

    def pack_l(t):
        return t.reshape(2, nblk, S5_PAIRS, LANE)

    lam = jnp.stack([pack_l(lam_re), pack_l(lam_im)], axis=-2)
    return bp, cp, lam


def _s5_pack_state(h0):
    b, _, g, _, _ = h0.shape
    nblk = g // (2 * S5_PAIRS)
    t = h0.reshape(b, 2, nblk, S5_PAIRS, 2, S5_STATE, 2)
    t = jnp.transpose(t, (1, 2, 3, 0, 6, 4, 5))
    return t.reshape(2, nblk, S5_PAIRS, b, 2 * LANE)


def _s5_unpack_state(fin, b):
    nblk = fin.shape[1]
    t = fin.reshape(2, nblk, S5_PAIRS, b, 2, 2, S5_STATE)
    t = jnp.transpose(t, (3, 0, 1, 2, 5, 6, 4))
    return t.reshape(b, 2, nblk * S5_PAIRS * 2, S5_STATE, 2)


def _cmul(ar, ai, br, bi):
    return ar * br - ai * bi, ar * bi + ai * br


def _gather_tm(src_ref, dst_ref, L, B):
    def body(t, c):
        dst_ref[t] = src_ref[pl.ds(t, B, stride=L), :]
        return c
    lax.fori_loop(0, L, body, 0, unroll=8)


def _scatter_tm(src_ref, dst_ref, L, B):
    def body(t, c):
        dst_ref[pl.ds(t, B, stride=L), :] = src_ref[t]
        return c
    lax.fori_loop(0, L, body, 0, unroll=8)


def _s5_kernel(*refs, L, B, nseg):
    if nseg > 1:
        (u_ref, bp_ref, cp_ref, lam_ref, d_ref, h0_ref, _, y_ref,
         xs_ref, yacc_ref, utm_ref, fl_ref, hp_ref) = refs
    else:
        u_ref, bp_ref, cp_ref, lam_ref, d_ref, y_ref, fin_ref, xs_ref, yacc_ref, utm_ref = refs
    nb = B // nseg
    _gather_tm(u_ref, utm_ref, L, B)
    u2 = utm_ref[...].reshape(L * B, LANE)
    u2b = u2.astype(BF16)
    yacc_ref[...] = jnp.zeros((L * B, LANE), F32)

    for half in range(S5_PAIRS // 2):
        chains = [(d, jj) for d in range(2) for jj in range(2)]
        for d, jj in chains:
            x = jnp.dot(u2b, bp_ref[d, 0, 2 * half + jj], preferred_element_type=F32)
            xs_ref[d, jj] = x.reshape(L, B, 2 * LANE)
        lam = {}
        for d, jj in chains:
            lr = jnp.broadcast_to(lam_ref[d, 0, 2 * half + jj, 0:1, :], (B, LANE))
            li = jnp.broadcast_to(lam_ref[d, 0, 2 * half + jj, 1:2, :], (B, LANE))
            lam[(d, jj)] = (lr, li)

        def load_x(i):
            out = []
            for d, jj in chains:
                t = i if d == 0 else L - 1 - i
                out += [xs_ref[d, jj, t, :, 0:LANE], xs_ref[d, jj, t, :, LANE:2 * LANE]]
            return out

        nc = 2 * len(chains)

        def step(i, carry):
            nxt = load_x(jnp.minimum(i + 1, L - 1))
            out = []
            for n, (d, jj) in enumerate(chains):
                t = i if d == 0 else L - 1 - i
                hr, hi = carry[2 * n], carry[2 * n + 1]
                lr, li = lam[(d, jj)]
                pr, pi = _cmul(lr, li, hr, hi)
                nr = pr + carry[nc + 2 * n]
                ni = pi + carry[nc + 2 * n + 1]
                xs_ref[d, jj, t, :, 0:LANE] = nr
                xs_ref[d, jj, t, :, LANE:2 * LANE] = ni
                out += [nr, ni]
            return tuple(out + nxt)

        zero = jnp.zeros((B, LANE), F32)
        carry = lax.fori_loop(0, L, step, (zero,) * nc + tuple(load_x(0)), unroll=4)

        if nseg == 1:
            for n, (d, jj) in enumerate(chains):
                fin_ref[d, 0, 2 * half + jj, :, 0:LANE] = carry[2 * n]
                fin_ref[d, 0, 2 * half + jj, :, LANE:2 * LANE] = carry[2 * n + 1]
        else:
            hp = {}
            for n, (d, jj) in enumerate(chains):
                plr = lam_ref[d, 0, 2 * half + jj, 0:1, :]
                pli = lam_ref[d, 0, 2 * half + jj, 1:2, :]
                for _ in range(int(math.log2(L))):
                    plr, pli = _cmul(plr, pli, plr, pli)
                fl_ref[0] = carry[2 * n]
                fl_ref[1] = carry[2 * n + 1]
                order = list(range(nseg)) if d == 0 else list(range(nseg - 1, -1, -1))
                for bb in range(nb):
                    cr = h0_ref[d, 0, 2 * half + jj, bb:bb + 1, 0:LANE]
                    ci = h0_ref[d, 0, 2 * half + jj, bb:bb + 1, LANE:2 * LANE]
                    for k, s in enumerate(order):
                        if k > 0:
                            rp = bb * nseg + order[k - 1]
                            mr, mi = _cmul(plr, pli, cr, ci)
                            cr = fl_ref[0, rp:rp + 1, :] + mr
                            ci = fl_ref[1, rp:rp + 1, :] + mi
                        r = bb * nseg + s
                        hp_ref[n, 0, r:r + 1, :] = cr
                        hp_ref[n, 1, r:r + 1, :] = ci
                hp[(d, jj)] = (hp_ref[n, 0], hp_ref[n, 1])

            def cstep(i, pw):
                nxt = load_x(jnp.minimum(i + 1, L - 1))
                out = []
                for n, (d, jj) in enumerate(chains):
                    t = i if d == 0 else L - 1 - i
                    pr, pi = pw[2 * n], pw[2 * n + 1]
                    hr, hi = hp[(d, jj)]
                    ar, ai = _cmul(pr, pi, hr, hi)
                    xs_ref[d, jj, t, :, 0:LANE] = pw[nc + 2 * n] + ar
                    xs_ref[d, jj, t, :, LANE:2 * LANE] = pw[nc + 2 * n + 1] + ai
                    lr, li = lam[(d, jj)]
                    nr, ni = _cmul(lr, li, pr, pi)
                    out += [nr, ni]
                return tuple(out + nxt)

            pw0 = []
            for d, jj in chains:
                pw0 += list(lam[(d, jj)])
            lax.fori_loop(0, L, cstep, tuple(pw0 + load_x(0)), unroll=4)

        for d, jj in chains:
            hs = xs_ref[d, jj].reshape(L * B, 2 * LANE).astype(BF16)
            yacc_ref[...] += jnp.dot(hs, cp_ref[d, 0, 2 * half + jj], preferred_element_type=F32)

    y = u2 * d_ref[...] + yacc_ref[...]
    utm_ref[...] = jax.nn.gelu(y).reshape(L, B, LANE)
    _scatter_tm(utm_ref, y_ref, L, B)


def _s5_call(src, col0, bp, cp, lam, d_skip, h0, y_prev, *, L, B, nseg, rblk, n_rows):
    w = d_skip.shape[-1]
    nblk = w // LANE
    cblk0 = col0 // LANE
    in_specs = [
        pl.BlockSpec((L * B, LANE), lambda k: (rblk, cblk0 + k)),
        pl.BlockSpec((2, 1, S5_PAIRS, LANE, 2 * LANE), lambda k: (0, k, 0, 0, 0)),
        pl.BlockSpec((2, 1, S5_PAIRS, 2 * LANE, LANE), lambda k: (0, k, 0, 0, 0)),
        pl.BlockSpec((2, 1, S5_PAIRS, 2, LANE), lambda k: (0, k, 0, 0, 0)),
        pl.BlockSpec((1, LANE), lambda k: (0, k)),
    ]
    args = [src, bp, cp, lam, d_skip.reshape(1, w)]
    y_shape = jax.ShapeDtypeStruct((n_rows, w), F32)
    y_spec = pl.BlockSpec((L * B, LANE), lambda k: (rblk, k))
    scratch = [pltpu.VMEM((2, 2, L, B, 2 * LANE), F32), pltpu.VMEM((L * B, LANE), F32),
               pltpu.VMEM((L, B, LANE), F32)]
    aliases = {}
    if nseg > 1:
        nb = B // nseg
        in_specs += [pl.BlockSpec((2, 1, S5_PAIRS, nb, 2 * LANE), lambda k: (0, k, 0, 0, 0)),
                     pl.BlockSpec(memory_space=pl.ANY)]
        args += [h0, y_prev]
        aliases = {len(args) - 1: 0}
        out_shape, out_specs = y_shape, y_spec
        scratch += [pltpu.VMEM((2, B, LANE), F32), pltpu.VMEM((4, 2, B, LANE), F32)]
    else:
        out_shape = [y_shape, jax.ShapeDtypeStruct((2, nblk, S5_PAIRS, B, 2 * LANE), F32)]
        out_specs = [y_spec, pl.BlockSpec((2, 1, S5_PAIRS, B, 2 * LANE), lambda k: (0, k, 0, 0, 0))]
    return pl.pallas_call(
        functools.partial(_s5_kernel, L=L, B=B, nseg=nseg),
        out_shape=out_shape,
        grid=(nblk,),
        in_specs=in_specs,
        out_specs=out_specs,
        scratch_shapes=scratch,
        input_output_aliases=aliases,
        compiler_params=_cparams("arbitrary"),
        name="s5_scan_seg" if nseg > 1 else "s5_scan",
    )(*args)


def _glu_kernel(yrow_ref, ytile_ref, w_ref, b_ref, g_ref, o_ref):
    w = w_ref[...].astype(BF16)
    z = jnp.dot(yrow_ref[...].astype(BF16), w, preferred_element_type=F32) + b_ref[...]
    o_ref[...] = (ytile_ref[...] * _sigmoid(z) * _silu(g_ref[...])).astype(BF16)


def _glu_call(y, w_glu, b_glu3, proj, l):
    goff = C_GA // TN
    return pl.pallas_call(
        _glu_kernel,
        out_shape=jax.ShapeDtypeStruct((N_TOK, W_BR), BF16),
        grid=(N_TOK // TM, W_BR // TN),
        in_specs=[
            pl.BlockSpec((TM, W_BR), lambda i, j: (i, 0)),
            pl.BlockSpec((TM, TN), lambda i, j: (i, j)),
            pl.BlockSpec((None, W_BR, TN), lambda i, j: (l, 0, j)),
            pl.BlockSpec((None, 1, TN), lambda i, j: (l, 0, j)),
            pl.BlockSpec((TM, TN), lambda i, j: (i, goff + j)),
        ],
        out_specs=pl.BlockSpec((TM, TN), lambda i, j: (i, j)),
        compiler_params=_cparams("arbitrary", "arbitrary"),
        name="s5_glu",
    )(y, y, w_glu, b_glu3, proj)


def _expm1(x):
    t = jnp.tanh(0.5 * x)
    return 2.0 * t / (1.0 - t)


def _lru_kernel(*refs, L, B, nseg, period):
    if nseg > 1:
        (x_ref, g_ref, cw_ref, cb_ref, wa_ref, ba_ref, wx_ref, bx_ref, sp_ref, h0_ref, _,
         y_ref, a_ref, b_ref, ytok_ref, fl_ref, hp_ref) = refs
    else:
        (x_ref, g_ref, cw_ref, cb_ref, wa_ref, ba_ref, wx_ref, bx_ref, sp_ref,
         y_ref, fin_ref, a_ref, b_ref, ytok_ref) = refs
    nb = B // nseg
    _gather_tm(x_ref, a_ref.at[0], L, B)
    x = a_ref[0]
    pos = lax.broadcasted_iota(jnp.int32, (L, B, LANE), 0) % period
    xc = jnp.broadcast_to(cb_ref[...].reshape(1, 1, LANE), (L, B, LANE))
    for k in range(CONV_W):
        off = k - CONV_W // 2
        if off < 0:
            xs = jnp.concatenate([jnp.zeros((-off, B, LANE), F32), x[:L + off]], axis=0)
        elif off > 0:
            xs = jnp.concatenate([x[off:], jnp.zeros((off, B, LANE), F32)], axis=0)
        else:
            xs = x
        if period < L:
            xs = jnp.where((pos + off >= 0) & (pos + off < period), xs, 0.0)
        xc = xc + cw_ref[k:k + 1, :].reshape(1, 1, LANE) * xs
    xc2 = xc.reshape(L * B, LANE)
    xcb = xc2.astype(BF16)
    for d in range(2):
        r = _sigmoid(jnp.dot(xcb, wa_ref[d, 0].astype(BF16), preferred_element_type=F32)
                     + ba_ref[d:d + 1, :])
        ig = _sigmoid(jnp.dot(xcb, wx_ref[d, 0].astype(BF16), preferred_element_type=F32)
                      + bx_ref[d:d + 1, :])
        log_a = (-LRU_C) * r * sp_ref[d:d + 1, :]
        a_ref[d] = jnp.exp(log_a).reshape(L, B, LANE)
        mult = jnp.sqrt(-_expm1(2.0 * log_a))
        b_ref[d] = (mult * (ig * xc2)).reshape(L, B, LANE)

    def load_ab(i):
        out = []
        for d in range(2):
            t = i if d == 0 else L - 1 - i
            out += [a_ref[d, t], b_ref[d, t]]
        return out

    def step(i, carry):
        nxt = load_ab(jnp.minimum(i + 1, L - 1))
        out = []
        for d in range(2):
            t = i if d == 0 else L - 1 - i
            a = carry[4 + 2 * d]
            h = a * carry[2 * d] + carry[4 + 2 * d + 1]
            b_ref[d, t] = h
            if nseg > 1:
                p = a * carry[2 * d + 1]
                a_ref[d, t] = p
            else:
                p = carry[2 * d + 1]
            out += [h, p]
        return tuple(out + nxt)

    zero = jnp.zeros((B, LANE), F32)
    one = jnp.ones((B, LANE), F32)
    carry = lax.fori_loop(0, L, step, (zero, one, zero, one) + tuple(load_ab(0)), unroll=8)

    if nseg == 1:
        fin_ref[0] = carry[0]
        fin_ref[1] = carry[2]
    else:
        hps = []
        for d in range(2):
            fl_ref[0] = carry[2 * d]
            fl_ref[1] = carry[2 * d + 1]
            order = list(range(nseg)) if d == 0 else list(range(nseg - 1, -1, -1))
            for bb in range(nb):
                c = h0_ref[d, bb:bb + 1, :]
                for k, s in enumerate(order):
                    if k > 0:
                        rp = bb * nseg + order[k - 1]
                        c = fl_ref[0, rp:rp + 1, :] + fl_ref[1, rp:rp + 1, :] * c
                    r = bb * nseg + s
                    hp_ref[d, r:r + 1, :] = c
            hps.append(hp_ref[d])

        def cstep(i, c):
            nxt = load_ab(jnp.minimum(i + 1, L - 1))
            for d in range(2):
                t = i if d == 0 else L - 1 - i
                b_ref[d, t] = c[2 * d + 1] + c[2 * d] * hps[d]
            return tuple(nxt)

        lax.fori_loop(0, L, cstep, tuple(load_ab(0)), unroll=4)

    a_ref[0] = b_ref[0] + b_ref[1]
    _scatter_tm(a_ref.at[0], ytok_ref, L, B)
    y_ref[...] = (ytok_ref[...] * _silu(g_ref[...])).astype(BF16)


def _lru_call(src, col_x, col_g, conv_w, conv_b, w_a, b_a, w_x, b_x, sp, h0, y_prev,
              *, L, B, nseg, period, rblk, n_rows):
    w = conv_b.shape[-1]
    nblk = w // LANE

    def rows(col0):
        return pl.BlockSpec((L * B, LANE), lambda k: (rblk, col0 // LANE + k))

    y_spec = pl.BlockSpec((L * B, LANE), lambda k: (rblk, k))
    in_specs = [
        rows(col_x), rows(col_g),
        pl.BlockSpec((CONV_W, LANE), lambda k: (0, k)),
        pl.BlockSpec((1, LANE), lambda k: (0, k)),
        pl.BlockSpec((2, 1, LANE, LANE), lambda k: (0, k, 0, 0)),
        pl.BlockSpec((2, LANE), lambda k: (0, k)),
        pl.BlockSpec((2, 1, LANE, LANE), lambda k: (0, k, 0, 0)),
        pl.BlockSpec((2, LANE), lambda k: (0, k)),
        pl.BlockSpec((2, LANE), lambda k: (0, k)),
    ]
    args = [src, src, conv_w, conv_b.reshape(1, w), w_a, b_a, w_x, b_x, sp]
    y_shape = jax.ShapeDtypeStruct((n_rows, w), BF16)
    scratch = [pltpu.VMEM((2, L, B, LANE), F32), pltpu.VMEM((2, L, B, LANE), F32),
               pltpu.VMEM((L * B, LANE), F32)]
    aliases = {}
    if nseg > 1:
        nb = B // nseg
        in_specs += [pl.BlockSpec((2, nb, LANE), lambda k: (0, 0, k)), pl.BlockSpec(memory_space=pl.ANY)]
        args += [h0, y_prev]
        aliases = {len(args) - 1: 0}
        out_shape, out_specs = y_shape, y_spec
        scratch += [pltpu.VMEM((2, B, LANE), F32), pltpu.VMEM((2, B, LANE), F32)]
    else:
        out_shape = [y_shape, jax.ShapeDtypeStruct((2, B, w), F32)]
        out_specs = [y_spec, pl.BlockSpec((2, B, LANE), lambda k: (0, 0, k))]
    return pl.pallas_call(
        functools.partial(_lru_kernel, L=L, B=B, nseg=nseg, period=period),
        out_shape=out_shape,
        grid=(nblk,),
        in_specs=in_specs,
        out_specs=out_specs,
        scratch_shapes=scratch,
        input_output_aliases=aliases,
        compiler_params=_cparams("arbitrary"),
        name="rglru_seg" if nseg > 1 else "rglru",
    )(*args)


def _nt_dot(a, b):
    return lax.dot_general(a, b, (((1,), (1,)), ((), ())), preferred_element_type=F32)


def _hgrn_kernel(*refs, L, has_h0, n_aliased):
    q_ref, ff_ref, fb_ref, v_ref, g_ref, lb_ref, nw_ref = refs[:7]
    h0_ref = refs[7] if has_h0 else None
    y_ref, fin_ref, of_ref = refs[7 + has_h0 + n_aliased:]
    rb_rows = HG_RB
    nrb = L // rb_rows
    nsub = HG_ST // HG_CHUNK
    nch = rb_rows // HG_ST
    row = lax.broadcasted_iota(jnp.int32, (rb_rows, rb_rows), 0)
    col = lax.broadcasted_iota(jnp.int32, (rb_rows, rb_rows), 1)
    sub_dist = jnp.where((row // HG_ST) == (col // HG_ST), row // HG_CHUNK - col // HG_CHUNK, 2 * nsub)
    ridx = lax.broadcasted_iota(jnp.int32, (rb_rows, HG_DK), 0)
    pos = ridx % HG_CHUNK
    sic = (ridx // HG_CHUNK) % nsub
    chunk_id = ridx // HG_ST

    def sub_bcast(x, idx):
        x3 = x.reshape(rb_rows // HG_CHUNK, HG_CHUNK, HG_DK)
        return jnp.broadcast_to(x3[:, idx:idx + 1, :], x3.shape).reshape(rb_rows, HG_DK)

    def chunk_bcast(x, idx):
        x3 = x.reshape(nch, HG_ST, HG_DK)
        return jnp.broadcast_to(x3[:, idx:idx + 1, :], x3.shape).reshape(rb_rows, HG_DK)

    def run_block(rb, st, d):
        r0 = pl.multiple_of(rb * rb_rows, rb_rows)
        q = q_ref[pl.ds(r0, rb_rows), :]
        qf = _silu(q)
        f = (ff_ref if d == 0 else fb_ref)[pl.ds(r0, rb_rows), :]
        lbd = lb_ref[d:d + 1, :]
        g = lbd + (1.0 - lbd) * _sigmoid(f)
        kk = 1.0 - g
        b = jnp.log(g)
        v = v_ref[pl.ds(r0, rb_rows), :]
        vb = v.astype(BF16)
        sgn = 1 if d == 0 else -1

        def shift(x, n):
            return pltpu.roll(x, n if d == 0 else rb_rows - n, 0)

        s = 1
        while s < HG_CHUNK:
            keep = (pos >= s) if d == 0 else (pos <= HG_CHUNK - 1 - s)
            b = b + jnp.where(keep, shift(b, s), 0.0)
            s *= 2
        tot = sub_bcast(b, HG_CHUNK - 1 if d == 0 else 0)
        prev = [shift(tot, HG_CHUNK * n) for n in range(1, nsub)]
        seen = sic if d == 0 else nsub - 1 - sic
        base = sum(jnp.where(seen >= n, prev[n - 1], 0.0) for n in range(1, nsub))
        ctot = chunk_bcast(b + base, HG_ST - 1 if d == 0 else 0)

        ed = jnp.exp(b - sub_bcast(b, HG_CHUNK // 2))
        sc0 = _nt_dot((qf * ed).astype(BF16), (kk / ed).astype(BF16))
        qe = qf * jnp.exp(b)
        kd = kk * jnp.exp(tot - b)
        lhs = [qe]
        acc = None
        for n in range(1, nsub - 1):
            acc = prev[n - 1] if acc is None else acc + prev[n - 1]
            lhs.append(qe * jnp.exp(acc))
        scn = _nt_dot(jnp.concatenate(lhs, axis=0).astype(BF16), kd.astype(BF16))
        causal = (col <= row) if d == 0 else (col >= row)
        sc = jnp.where((sub_dist == 0) & causal, sc0, 0.0)
        for n in range(1, nsub):
            sc = jnp.where(sub_dist == sgn * n, scn[(n - 1) * rb_rows:n * rb_rows], sc)
        o = jnp.dot(sc.astype(BF16), vb, preferred_element_type=F32)

        qe_st = (qe * jnp.exp(base)).astype(BF16)
        kd_st = kd * jnp.exp(ctot - base - tot)
        vt = v.T.astype(BF16)
        uts = []
        for c in range(0, nch, 2):
            rhs = jnp.concatenate([jnp.where(chunk_id == c, kd_st, 0.0),
                                   jnp.where(chunk_id == c + 1, kd_st, 0.0)], axis=1).astype(BF16)
            ut2 = jnp.dot(vt, rhs, preferred_element_type=F32)
            uts += [ut2[:, :HG_DK], ut2[:, HG_DK:]]
        order = range(nch) if d == 0 else range(nch - 1, -1, -1)
        entering = [None] * nch
        for c in order:
            entering[c] = st
            tot_row = c * HG_ST + (HG_ST - 1 if d == 0 else 0)
            st = st * jnp.exp(ctot[tot_row:tot_row + 1, :]) + uts[c]
        outs = []
        for c in range(nch):
            lo = c * HG_ST
            oi = _nt_dot(qe_st[lo:lo + HG_ST], entering[c].astype(BF16))
            outs.append(o[lo:lo + HG_ST] + oi)
        return jnp.concatenate(outs, axis=0), st

    for d in range(2):
        if has_h0:
            st0 = h0_ref[0, d, 0].T
        else:
            st0 = jnp.zeros((HG_DK, HG_DK), F32)

        def body(i, st, d=d):
            rb = i if d == 0 else nrb - 1 - i
            oblk, st = run_block(rb, st, d)
            r0 = pl.multiple_of(rb * rb_rows, rb_rows)
            if d == 0:
                of_ref[pl.ds(r0, rb_rows), :] = oblk
            else:
                of_ref[pl.ds(r0, rb_rows), :] = of_ref[pl.ds(r0, rb_rows), :] + oblk
            return st

        st = lax.fori_loop(0, nrb, body, st0)
        fin_ref[0, d, 0] = st.T

    o = of_ref[...]
    o = o * lax.rsqrt(jnp.mean(o * o, axis=-1, keepdims=True) + RMS_EPS)
    y_ref[...] = (o * nw_ref[...] * _silu(g_ref[...])).astype(BF16)


def _hgrn_call(proj, lb, norm_w, h0, y_prev, fin_prev, *, L, nb, row0, n_rows, layer,
               cols=(C_QC, C_FF, C_FB, C_IC, C_GC)):
    w = lb.shape[-1]
    nh = w // HG_DK
    rblk0 = row0 // L

    def col(off):
        return pl.BlockSpec((L, HG_DK), lambda b, h: (rblk0 + b, off // HG_DK + h))

    in_specs = [col(c) for c in cols] + [
                pl.BlockSpec((2, HG_DK), lambda b, h: (0, h)),
                pl.BlockSpec((1, HG_DK), lambda b, h: (0, h))]
    args = [proj] * 5 + [lb, norm_w.reshape(1, w)]
    if h0 is not None:
        in_specs.append(pl.BlockSpec((1, 2, 1, HG_DK, HG_DK), lambda b, h: (b, 0, h, 0, 0)))
        args.append(h0)
    aliases = {}
    for out_idx, prev in enumerate((y_prev, fin_prev)):
        if prev is not None:
            in_specs.append(pl.BlockSpec(memory_space=pl.ANY))
            args.append(prev)
            aliases[len(args) - 1] = out_idx
    st_spec = pl.BlockSpec((1, None, 2, 1, HG_DK, HG_DK), lambda b, h: (b, layer, 0, h, 0, 0))
    return pl.pallas_call(
        functools.partial(_hgrn_kernel, L=L, has_h0=h0 is not None, n_aliased=len(aliases)),
        out_shape=[jax.ShapeDtypeStruct((n_rows, w), BF16),
                   jax.ShapeDtypeStruct((nb, DEPTH, 2, nh, HG_DK, HG_DK), F32)],
        grid=(nb, nh),
        in_specs=in_specs,
        out_specs=[pl.BlockSpec((L, HG_DK), lambda b, h: (rblk0 + b, h)), st_spec],
        scratch_shapes=[pltpu.VMEM((L, HG_DK), F32)],
        input_output_aliases=aliases,
        compiler_params=_cparams("arbitrary", "arbitrary"),
        name="hgrn2",
    )(*args)


def _merge_kernel(ya_ref, yb_ref, yc_ref, wa_ref, wb_ref, wc_ref, ma_ref, mb_ref, mc_ref, o_ref):
    acc = None
    for y_ref, w_ref, m_ref in ((ya_ref, wa_ref, ma_ref), (yb_ref, wb_ref, mb_ref), (yc_ref, wc_ref, mc_ref)):
        t = jnp.dot(y_ref[...], w_ref[...].astype(BF16), preferred_element_type=F32)
        t = _sigmoid(m_ref[...]) * t
        acc = t if acc is None else acc + t
    o_ref[...] = acc.astype(BF16)


def _merge_call(ya, yb, yc, w_br, proj, l):
    tm, tn = TM_MERGE, TN_MERGE
    moff = C_MA // tn
    mstep = D_MODEL // tn
    lhs = pl.BlockSpec((tm, W_BR), lambda i, j: (i, 0))

    def wspec(br):
        return pl.BlockSpec((None, None, W_BR, tn), lambda i, j: (l, br, 0, j))

    def mspec(br):
        return pl.BlockSpec((tm, tn), lambda i, j: (i, moff + br * mstep + j))

    return pl.pallas_call(
        _merge_kernel,
        out_shape=jax.ShapeDtypeStruct((N_TOK, D_MODEL), BF16),
        grid=(N_TOK // tm, D_MODEL // tn),
        in_specs=[lhs, lhs, lhs, wspec(0), wspec(1), wspec(2), mspec(0), mspec(1), mspec(2)],
        out_specs=pl.BlockSpec((tm, tn), lambda i, j: (i, j)),
        compiler_params=_cparams("arbitrary", "arbitrary"),
        name="branch_merge",
    )(ya, yb, yc, w_br, w_br, w_br, proj, proj, proj)


def _outproj_kernel(m_ref, w_ref, b_ref, xc_ref, xl_ref, gate_ref, o_ref):
    out = jnp.dot(m_ref[...], w_ref[...].astype(BF16), preferred_element_type=F32) + b_ref[...]
    gated = gate_ref[0] * out
    is_ctx = pl.program_id(0) < N_CTX // TM

    @pl.when(is_ctx)
    def _():
        o_ref[...] = DN_ALPHA * xc_ref[...] + gated

    @pl.when(jnp.logical_not(is_ctx))
    def _():
        o_ref[...] = DN_ALPHA * xl_ref[...] + gated


def _outproj_call(merged, w_out, b_out3, x_ctx, x_lat, mods3, l):
    goff = 2 * D_MODEL // TN
    ctx_map, lat_map = _split_rows(TM, 1)
    return pl.pallas_call(
        _outproj_kernel,
        out_shape=jax.ShapeDtypeStruct((N_TOK, D_MODEL), F32),
        grid=(N_TOK // TM, D_MODEL // TN),
        in_specs=[
            pl.BlockSpec((TM, D_MODEL), lambda i, j: (i, 0)),
            pl.BlockSpec((None, D_MODEL, TN), lambda i, j: (l, 0, j)),
            pl.BlockSpec((None, 1, TN), lambda i, j: (l, 0, j)),
            pl.BlockSpec((TM, TN), ctx_map),
            pl.BlockSpec((TM, TN), lat_map),
            pl.BlockSpec((1, 1, TN), lambda i, j: (_mod_row_mm(i), 0, goff + j)),
        ],
        out_specs=pl.BlockSpec((TM, TN), lambda i, j: (i, j)),
        compiler_params=_cparams("arbitrary", "arbitrary"),
        name="out_proj",
    )(merged, w_out, b_out3, x_ctx, x_lat, mods3)


def _ln_affine_kernel(r_ref, g_ref, b_ref, o_ref):
    r = r_ref[...]
    mu = jnp.mean(r, axis=-1, keepdims=True)
    rc = r - mu
    var = jnp.mean(rc * rc, axis=-1, keepdims=True)
    o_ref[...] = rc * lax.rsqrt(var + LN_EPS) * g_ref[...] + b_ref[...]


def _ln_affine_call(r, g, b, l, row0=0, n_rows=N_TOK):
    rblk0 = row0 // TM_LN
    return pl.pallas_call(
        _ln_affine_kernel,
        out_shape=jax.ShapeDtypeStruct((n_rows, D_MODEL), F32),
        grid=(n_rows // TM_LN,),
        in_specs=[
            pl.BlockSpec((TM_LN, D_MODEL), lambda i: (rblk0 + i, 0)),
            pl.BlockSpec((None, 1, D_MODEL), lambda i: (l, 0, 0)),
            pl.BlockSpec((None, 1, D_MODEL), lambda i: (l, 0, 0)),
        ],
        out_specs=pl.BlockSpec((TM_LN, D_MODEL), lambda i: (i, 0)),
        compiler_params=_cparams("arbitrary"),
        name="post_ln",
    )(r, g, b)


def kernel(x_prompt, x_sample, state_s5, state_lru, state_hgrn, c, c_ctx, w_ada, b_ada, w_in, b_in,
           s5_a_re, s5_a_im, s5_log_dt, s5_b_re, s5_b_im, s5_c_re, s5_c_im, s5_d, s5_w_glu, s5_b_glu,
           lru_conv_w, lru_conv_b, lru_w_a, lru_b_a, lru_w_x, lru_b_x, lru_lambda, hg_lb, hg_norm_w,
           w_br, w_out, b_out, ln_g, ln_b):
    lb_soft = jax.nn.softmax(hg_lb.astype(F32), axis=0)
    lb_all = jnp.cumsum(lb_soft, axis=0) - lb_soft[0]
    softplus_neg_lam = jax.nn.softplus(-lru_lambda.astype(F32))

    cvec = jnp.zeros((SUBLANE, D_MODEL), F32)
    cvec = cvec.at[0].set(c_ctx).at[1:1 + DEC_BATCH].set(c)
    mods = _mods_call(cvec, w_ada, b_ada)

    b_in3 = b_in.reshape(DEPTH, 1, IN_COLS)
    b_glu3 = s5_b_glu.reshape(DEPTH, 1, W_BR)
    b_out3 = b_out.reshape(DEPTH, 1, D_MODEL)
    ln_g3 = ln_g.reshape(DEPTH, 1, D_MODEL)
    ln_b3 = ln_b.reshape(DEPTH, 1, D_MODEL)

    x_ctx = x_prompt.reshape(N_CTX, D_MODEL)
    x_lat = x_sample.reshape(N_LAT, D_MODEL)
    st_s5, st_lru, st_hg = [], [], None
    for l in range(DEPTH):
        mods3 = mods[l, :1 + DEC_BATCH].reshape(1 + DEC_BATCH, 1, 3 * D_MODEL)
        h = _ln_mod_call(x_ctx, x_lat, mods3)
        proj = _inproj_call(h, w_in, b_in3, l)

        lam_re, lam_im, z_re, z_im = _s5_disc_call(s5_a_re[l], s5_a_im[l], s5_log_dt[l])
        bp, cp, lam = _s5_pack(lam_re, lam_im, z_re, z_im, s5_b_re[l], s5_b_im[l], s5_c_re[l], s5_c_im[l])
        ctx_rows = dict(L=SEQ, B=BATCH, nseg=1, rblk=0, n_rows=N_TOK)
        lat_rows = dict(L=SEQ, B=LAT_ROWS, nseg=LAT_SEG, rblk=N_CTX // N_LAT, n_rows=N_TOK)
        y_pre, fin_s5 = _s5_call(proj, C_UA, bp, cp, lam, s5_d[l], None, None, **ctx_rows)
        y_pre = _s5_call(proj, C_UA, bp, cp, lam, s5_d[l], _s5_pack_state(state_s5[:, l]), y_pre, **lat_rows)
        y_a = _glu_call(y_pre, s5_w_glu, b_glu3, proj, l)
        st_s5.append(_s5_unpack_state(fin_s5, BATCH))

        lru_args = (lru_conv_w[l], lru_conv_b[l], lru_w_a[l], lru_b_a[l], lru_w_x[l], lru_b_x[l],
                    softplus_neg_lam[l])
        y_b, fin_lru = _lru_call(proj, C_XB, C_GB, *lru_args, None, None, period=SEQ, **ctx_rows)
        y_b = _lru_call(proj, C_XB, C_GB, *lru_args, jnp.transpose(state_lru[:, l], (1, 0, 2)), y_b,
                        period=GRID_W, **lat_rows)
        st_lru.append(jnp.transpose(fin_lru, (1, 0, 2)))

        y_c, st_hg = _hgrn_call(proj, lb_all[l], hg_norm_w[l], None, None, st_hg,
                                L=SEQ, nb=BATCH, row0=0, n_rows=N_TOK, layer=l)
        y_c, _ = _hgrn_call(proj, lb_all[l], hg_norm_w[l], state_hgrn[:, l], y_c, None,
                            L=DEC_SEQ, nb=DEC_BATCH, row0=N_CTX, n_rows=N_TOK, layer=0)

        merged = _merge_call(y_a, y_b, y_c, w_br, proj, l)
        r = _outproj_call(merged, w_out, b_out3, x_ctx, x_lat, mods3, l)
        x_ctx = _ln_affine_call(r, ln_g3, ln_b3, l, 0, N_CTX)
        x_lat = _ln_affine_call(r, ln_g3, ln_b3, l, N_CTX, N_LAT)

    y_prompt = x_ctx.reshape(BATCH, SEQ, D_MODEL)
    y_sample = x_lat.reshape(DEC_BATCH, DEC_SEQ, D_MODEL)
    new_state_s5 = jnp.stack(st_s5, axis=1)
    new_state_lru = jnp.stack(st_lru, axis=1)
    new_state_hgrn = st_hg
    return (y_prompt, y_sample, new_state_s5, new_state_lru, new_state_hgrn)
```

```python
import functools
import math

import jax
import jax.numpy as jnp
from jax import lax
from jax.experimental import pallas as pl
from jax.experimental.pallas import tpu as pltpu

F32 = jnp.float32
BF16 = jnp.bfloat16

LANE = 128
SUBLANE = 8
VMEM_LIMIT = 56 * 1024 * 1024

D_MODEL = 4096
DEPTH = 2
BATCH, SEQ = 16, 256
DEC_BATCH, DEC_SEQ = 2, 1024
GRID_W = 64
W_BR = D_MODEL // 2
S5_GROUP = 16
S5_STATE = 64
LRU_C = 8.0
CONV_W = 4
HG_DK = 128
HG_CHUNK = 16
N_BRANCH = 3
IN_COLS = 9 * W_BR + N_BRANCH * D_MODEL
DN_ALPHA = (2 * DEPTH) ** 0.25
LN_EPS = 1e-5
RMS_EPS = 1e-6

N_CTX = BATCH * SEQ
N_LAT = DEC_BATCH * DEC_SEQ
N_TOK = N_CTX + N_LAT
LAT_SEG = DEC_SEQ // SEQ
LAT_ROWS = LAT_SEG * DEC_BATCH

C_UA, C_GA, C_XB, C_GB, C_QC, C_FF, C_FB, C_IC, C_GC = (i * W_BR for i in range(9))
C_MA = 9 * W_BR

TM = 1024
TN = 512
TM_MERGE = 1024
TN_MERGE = 256
TM_LN = 256
HG_RB = 256
HG_ST = 64
HG_HPB = 2
HG_SCB = 128
S5_PAIRS = 4


def _cparams(*sem):
    return pltpu.CompilerParams(dimension_semantics=sem, vmem_limit_bytes=VMEM_LIMIT)


def _sigmoid(x):
    return jax.nn.sigmoid(x)


def _silu(x):
    return x * jax.nn.sigmoid(x)


def _mods_kernel(c_ref, w_ref, b_ref, o_ref):
    c = c_ref[...]
    s = _silu(c).astype(BF16)
    w = w_ref[...].astype(BF16)
    o_ref[...] = jnp.dot(s, w, preferred_element_type=F32) + b_ref[...]


def _mods_call(cvec, w_ada, b_ada):
    tn = TN
    return pl.pallas_call(
        _mods_kernel,
        out_shape=jax.ShapeDtypeStruct((DEPTH, SUBLANE, 3 * D_MODEL), F32),
        grid=(DEPTH, 3 * D_MODEL // tn),
        in_specs=[
            pl.BlockSpec((SUBLANE, D_MODEL), lambda l, j: (0, 0)),
            pl.BlockSpec((None, D_MODEL, tn), lambda l, j: (l, 0, j)),
            pl.BlockSpec((None, 1, tn), lambda l, j: (l, 0, j)),
        ],
        out_specs=pl.BlockSpec((None, SUBLANE, tn), lambda l, j: (l, 0, j)),
        compiler_params=_cparams("arbitrary", "arbitrary"),
        name="adaln_mods",
    )(cvec, w_ada, b_ada.reshape(DEPTH, 1, 3 * D_MODEL))


def _mod_row_ln(i):
    n_ctx_tiles = N_CTX // TM_LN
    return jnp.where(i < n_ctx_tiles, 0, 1 + (i - n_ctx_tiles) // (DEC_SEQ // TM_LN))


def _mod_row_mm(i):
    return jnp.maximum(i - (N_CTX // TM - 1), 0)


def _ln_mod_kernel(xc_ref, xl_ref, shift_ref, scale_ref, o_ref):
    def emit(x_ref):
        x = x_ref[...]
        mu = jnp.mean(x, axis=-1, keepdims=True)
        xc = x - mu
        var = jnp.mean(xc * xc, axis=-1, keepdims=True)
        h = xc * lax.rsqrt(var + LN_EPS) * (1.0 + scale_ref[0]) + shift_ref[0]
        o_ref[...] = h.astype(BF16)

    is_ctx = pl.program_id(0) < N_CTX // TM_LN
    pl.when(is_ctx)(lambda: emit(xc_ref))
    pl.when(jnp.logical_not(is_ctx))(lambda: emit(xl_ref))


def _split_rows(tile, n_col_axes):
    n_ctx_tiles = N_CTX // tile
    if n_col_axes == 0:
        return (lambda i: (jnp.minimum(i, n_ctx_tiles - 1), 0),
                lambda i: (jnp.maximum(i - n_ctx_tiles, 0), 0))
    return (lambda i, j: (jnp.minimum(i, n_ctx_tiles - 1), j),
            lambda i, j: (jnp.maximum(i - n_ctx_tiles, 0), j))


def _ln_mod_call(x_ctx, x_lat, mods3):
    ctx_map, lat_map = _split_rows(TM_LN, 0)
    return pl.pallas_call(
        _ln_mod_kernel,
        out_shape=jax.ShapeDtypeStruct((N_TOK, D_MODEL), BF16),
        grid=(N_TOK // TM_LN,),
        in_specs=[
            pl.BlockSpec((TM_LN, D_MODEL), ctx_map),
            pl.BlockSpec((TM_LN, D_MODEL), lat_map),
            pl.BlockSpec((1, 1, D_MODEL), lambda i: (_mod_row_ln(i), 0, 0)),
            pl.BlockSpec((1, 1, D_MODEL), lambda i: (_mod_row_ln(i), 0, 1)),
        ],
        out_specs=pl.BlockSpec((TM_LN, D_MODEL), lambda i: (i, 0)),
        compiler_params=_cparams("arbitrary"),
        name="ln_modulate",
    )(x_ctx, x_lat, mods3, mods3)


def _inproj_kernel(h_ref, w_ref, b_ref, o_ref):
    w = w_ref[...].astype(BF16)
    o_ref[...] = jnp.dot(h_ref[...], w, preferred_element_type=F32) + b_ref[...]


def _inproj_call(h, w_in, b_in3, l):
    return pl.pallas_call(
        _inproj_kernel,
        out_shape=jax.ShapeDtypeStruct((N_TOK, IN_COLS), F32),
        grid=(N_TOK // TM, IN_COLS // TN),
        in_specs=[
            pl.BlockSpec((TM, D_MODEL), lambda i, j: (i, 0)),
            pl.BlockSpec((None, D_MODEL, TN), lambda i, j: (l, 0, j)),
            pl.BlockSpec((None, 1, TN), lambda i, j: (l, 0, j)),
        ],
        out_specs=pl.BlockSpec((TM, TN), lambda i, j: (i, j)),
        compiler_params=_cparams("arbitrary", "arbitrary"),
        name="in_proj",
    )(h, w_in, b_in3)


def _s5_disc_kernel(are_ref, aim_ref, ldt_ref, lre_ref, lim_ref, zre_ref, zim_ref):
    ar = jnp.minimum(are_ref[...], -1e-4)
    ai = aim_ref[...]
    dt = jnp.exp(ldt_ref[...])
    mag = jnp.exp(dt * ar)
    lam_re = mag * jnp.cos(dt * ai)
    lam_im = mag * jnp.sin(dt * ai)
    den = ar * ar + ai * ai
    lre_ref[...] = lam_re
    lim_ref[...] = lam_im
    zre_ref[...] = ((lam_re - 1.0) * ar + lam_im * ai) / den
    zim_ref[...] = (lam_im * ar - (lam_re - 1.0) * ai) / den


def _s5_disc_call(a_re, a_im, log_dt):
    g = a_re.shape[1]
    shp = (2 * g, S5_STATE)
    ldt = jnp.broadcast_to(log_dt[..., None], (2, g, S5_STATE)).reshape(shp)
    outs = pl.pallas_call(
        _s5_disc_kernel,
        out_shape=[jax.ShapeDtypeStruct(shp, F32)] * 4,
        name="s5_discretise",
    )(a_re.reshape(shp), a_im.reshape(shp), ldt)
    return [o.reshape(2, g, S5_STATE) for o in outs]


def _s5_pack(lam_re, lam_im, z_re, z_im, b_re, b_im, c_re, c_im):
    g = lam_re.shape[1]
    nblk = g // (2 * S5_PAIRS)
    pair_rows = 2 * S5_GROUP
    bz_re = z_re[..., None] * b_re[None] - z_im[..., None] * b_im[None]
    bz_im = z_re[..., None] * b_im[None] + z_im[..., None] * b_re[None]

    def pairs(t):
        t = t.reshape(t.shape[:-3] + (g // 2, 2) + t.shape[-2:])
        return t[..., 0, :, :], t[..., 1, :, :]

    def embed(t, axis):
        t = t.reshape(t.shape[:-3] + (nblk, S5_PAIRS) + t.shape[-2:])
        out = []
        for j in range(S5_PAIRS):
            pad = [(0, 0)] * (t.ndim - 1)
            pad[axis] = (j * pair_rows, LANE - (j + 1) * pair_rows)
            out.append(jnp.pad(t[..., j, :, :], pad))
        return jnp.stack(out, axis=-3)

    br0, br1 = pairs(jnp.swapaxes(bz_re, -1, -2))
    bi0, bi1 = pairs(jnp.swapaxes(bz_im, -1, -2))
    zb = jnp.zeros_like(br0)
    bp = jnp.concatenate([jnp.concatenate([br0, zb, bi0, zb], axis=-1),
                          jnp.concatenate([zb, br1, zb, bi1], axis=-1)], axis=-2)
    bp = embed(bp, -2).astype(BF16)

    cr0, cr1 = pairs(jnp.swapaxes(c_re, -1, -2))
    ci0, ci1 = pairs(jnp.swapaxes(c_im, -1, -2))
    zc = jnp.zeros_like(cr0)
    cp = jnp.concatenate([jnp.concatenate([cr0, zc], axis=-1), jnp.concatenate([zc, cr1], axis=-1),
                          jnp.concatenate([-ci0, zc], axis=-1), jnp.concatenate([zc, -ci1], axis=-1)],
                         axis=-2)
    cp = embed(cp, -1).astype(BF16)
    cp = jnp.broadcast_to(cp[None], (2,) + cp.shape)

    def pack_l(t):
        return t.reshape(2, nblk, S5_PAIRS, LANE)

    lam = jnp.stack([pack_l(lam_re), pack_l(lam_im)], axis=-2)
    return bp, cp, lam


def _s5_pack_state(h0):
    b, _, g, _, _ = h0.shape
    nblk = g // (2 * S5_PAIRS)
    t = h0.reshape(b, 2, nblk, S5_PAIRS, 2, S5_STATE, 2)
    t = jnp.transpose(t, (1, 2, 3, 0, 6, 4, 5))
    return t.reshape(2, nblk, S5_PAIRS, b, 2 * LANE)


def _s5_unpack_state(fin, b):
    nblk = fin.shape[1]
    t = fin.reshape(2, nblk, S5_PAIRS, b, 2, 2, S5_STATE)
    t = jnp.transpose(t, (3, 0, 1, 2, 5, 6, 4))
    return t.reshape(b, 2, nblk * S5_PAIRS * 2, S5_STATE, 2)


def _cmul(ar, ai, br, bi):
    return ar * br - ai * bi, ar * bi + ai * br


def _gather_tm(src_ref, dst_ref, L, B):
    def body(t, c):
        dst_ref[t] = src_ref[pl.ds(t, B, stride=L), :]
        return c
    lax.fori_loop(0, L, body, 0, unroll=8)


def _scatter_tm(src_ref, dst_ref, L, B):
    def body(t, c):
        dst_ref[pl.ds(t, B, stride=L), :] = src_ref[t]
        return c
    lax.fori_loop(0, L, body, 0, unroll=8)


def _s5_kernel(*refs, L, B, nseg):
    if nseg > 1:
        (u_ref, bp_ref, cp_ref, lam_ref, d_ref, h0_ref, _, y_ref,
         xs_ref, yacc_ref, utm_ref, fl_ref, hp_ref) = refs
    else:
        u_ref, bp_ref, cp_ref, lam_ref, d_ref, y_ref, fin_ref, xs_ref, yacc_ref, utm_ref = refs
    nb = B // nseg
    _gather_tm(u_ref, utm_ref, L, B)
    u2 = utm_ref[...].reshape(L * B, LANE)
    u2b = u2.astype(BF16)
    yacc_ref[...] = jnp.zeros((L * B, LANE), F32)

    for half in range(S5_PAIRS // 2):
        chains = [(d, jj) for d in range(2) for jj in range(2)]
        for d, jj in chains:
            x = jnp.dot(u2b, bp_ref[d, 0, 2 * half + jj], preferred_element_type=F32)
            xs_ref[d, jj] = x.reshape(L, B, 2 * LANE)
        lam = {}
        for d, jj in chains:
            lr = jnp.broadcast_to(lam_ref[d, 0, 2 * half + jj, 0:1, :], (B, LANE))
            li = jnp.broadcast_to(lam_ref[d, 0, 2 * half + jj, 1:2, :], (B, LANE))
            lam[(d, jj)] = (lr, li)

        def load_x(i):
            out = []
            for d, jj in chains:
                t = i if d == 0 else L - 1 - i
                out += [xs_ref[d, jj, t, :, 0:LANE], xs_ref[d, jj, t, :, LANE:2 * LANE]]
            return out

        nc = 2 * len(chains)

        def step(i, carry):
            nxt = load_x(jnp.minimum(i + 1, L - 1))
            out = []
            for n, (d, jj) in enumerate(chains):
                t = i if d == 0 else L - 1 - i
                hr, hi = carry[2 * n], carry[2 * n + 1]
                lr, li = lam[(d, jj)]
                pr, pi = _cmul(lr, li, hr, hi)
                nr = pr + carry[nc + 2 * n]
                ni = pi + carry[nc + 2 * n + 1]
                xs_ref[d, jj, t, :, 0:LANE] = nr
                xs_ref[d, jj, t, :, LANE:2 * LANE] = ni
                out += [nr, ni]
            return tuple(out + nxt)

        zero = jnp.zeros((B, LANE), F32)
        carry = lax.fori_loop(0, L, step, (zero,) * nc + tuple(load_x(0)), unroll=4)

        if nseg == 1:
            for n, (d, jj) in enumerate(chains):
                fin_ref[d, 0, 2 * half + jj, :, 0:LANE] = carry[2 * n]
                fin_ref[d, 0, 2 * half + jj, :, LANE:2 * LANE] = carry[2 * n + 1]
        else:
            hp = {}
            for n, (d, jj) in enumerate(chains):
                plr = lam_ref[d, 0, 2 * half + jj, 0:1, :]
                pli = lam_ref[d, 0, 2 * half + jj, 1:2, :]
                for _ in range(int(math.log2(L))):
                    plr, pli = _cmul(plr, pli, plr, pli)
                fl_ref[0] = carry[2 * n]
                fl_ref[1] = carry[2 * n + 1]
                order = list(range(nseg)) if d == 0 else list(range(nseg - 1, -1, -1))
                for bb in range(nb):
                    cr = h0_ref[d, 0, 2 * half + jj, bb:bb + 1, 0:LANE]
                    ci = h0_ref[d, 0, 2 * half + jj, bb:bb + 1, LANE:2 * LANE]
                    for k, s in enumerate(order):
                        if k > 0:
                            rp = bb * nseg + order[k - 1]
                            mr, mi = _cmul(plr, pli, cr, ci)
                            cr = fl_ref[0, rp:rp + 1, :] + mr
                            ci = fl_ref[1, rp:rp + 1, :] + mi
                        r = bb * nseg + s
                        hp_ref[n, 0, r:r + 1, :] = cr
                        hp_ref[n, 1, r:r + 1, :] = ci
                hp[(d, jj)] = (hp_ref[n, 0], hp_ref[n, 1])

            def cstep(i, pw):
                nxt = load_x(jnp.minimum(i + 1, L - 1))
                out = []
                for n, (d, jj) in enumerate(chains):
                    t = i if d == 0 else L - 1 - i
                    pr, pi = pw[2 * n], pw[2 * n + 1]
                    hr, hi = hp[(d, jj)]
                    ar, ai = _cmul(pr, pi, hr, hi)
                    xs_ref[d, jj, t, :, 0:LANE] = pw[nc + 2 * n] + ar
                    xs_ref[d, jj, t, :, LANE:2 * LANE] = pw[nc + 2 * n + 1] + ai
                    lr, li = lam[(d, jj)]
                    nr, ni = _cmul(lr, li, pr, pi)
                    out += [nr, ni]
                return tuple(out + nxt)

            pw0 = []
            for d, jj in chains:
                pw0 += list(lam[(d, jj)])
            lax.fori_loop(0, L, cstep, tuple(pw0 + load_x(0)), unroll=4)

        for d, jj in chains:
            hs = xs_ref[d, jj].reshape(L * B, 2 * LANE).astype(BF16)
            yacc_ref[...] += jnp.dot(hs, cp_ref[d, 0, 2 * half + jj], preferred_element_type=F32)

    y = u2 * d_ref[...] + yacc_ref[...]
    utm_ref[...] = jax.nn.gelu(y).reshape(L, B, LANE)
    _scatter_tm(utm_ref, y_ref, L, B)


def _s5_call(src, col0, bp, cp, lam, d_skip, h0, y_prev, *, L, B, nseg, rblk, n_rows):
    w = d_skip.shape[-1]
    nblk = w // LANE
    cblk0 = col0 // LANE
    in_specs = [
        pl.BlockSpec((L * B, LANE), lambda k: (rblk, cblk0 + k)),
        pl.BlockSpec((2, 1, S5_PAIRS, LANE, 2 * LANE), lambda k: (0, k, 0, 0, 0)),
        pl.BlockSpec((2, 1, S5_PAIRS, 2 * LANE, LANE), lambda k: (0, k, 0, 0, 0)),
        pl.BlockSpec((2, 1, S5_PAIRS, 2, LANE), lambda k: (0, k, 0, 0, 0)),
        pl.BlockSpec((1, LANE), lambda k: (0, k)),
    ]
    args = [src, bp, cp, lam, d_skip.reshape(1, w)]
    y_shape = jax.ShapeDtypeStruct((n_rows, w), F32)
    y_spec = pl.BlockSpec((L * B, LANE), lambda k: (rblk, k))
    scratch = [pltpu.VMEM((2, 2, L, B, 2 * LANE), F32), pltpu.VMEM((L * B, LANE), F32),
               pltpu.VMEM((L, B, LANE), F32)]
    aliases = {}
    if nseg > 1:
        nb = B // nseg
        in_specs += [pl.BlockSpec((2, 1, S5_PAIRS, nb, 2 * LANE), lambda k: (0, k, 0, 0, 0)),
                     pl.BlockSpec(memory_space=pl.ANY)]
        args += [h0, y_prev]
        aliases = {len(args) - 1: 0}
        out_shape, out_specs = y_shape, y_spec
        scratch += [pltpu.VMEM((2, B, LANE), F32), pltpu.VMEM((4, 2, B, LANE), F32)]
    else:
        out_shape = [y_shape, jax.ShapeDtypeStruct((2, nblk, S5_PAIRS, B, 2 * LANE), F32)]
        out_specs = [y_spec, pl.BlockSpec((2, 1, S5_PAIRS, B, 2 * LANE), lambda k: (0, k, 0, 0, 0))]
    return pl.pallas_call(
        functools.partial(_s5_kernel, L=L, B=B, nseg=nseg),
        out_shape=out_shape,
        grid=(nblk,),
        in_specs=in_specs,
        out_specs=out_specs,
        scratch_shapes=scratch,
        input_output_aliases=aliases,
        compiler_params=_cparams("arbitrary"),
        name="s5_scan_seg" if nseg > 1 else "s5_scan",
    )(*args)


def _glu_kernel(yrow_ref, ytile_ref, w_ref, b_ref, g_ref, o_ref):
    w = w_ref[...].astype(BF16)
    z = jnp.dot(yrow_ref[...].astype(BF16), w, preferred_element_type=F32) + b_ref[...]
    o_ref[...] = (ytile_ref[...] * _sigmoid(z) * _silu(g_ref[...])).astype(BF16)


def _glu_call(y, w_glu, b_glu3, proj, l):
    goff = C_GA // TN
    return pl.pallas_call(
        _glu_kernel,
        out_shape=jax.ShapeDtypeStruct((N_TOK, W_BR), BF16),
        grid=(N_TOK // TM, W_BR // TN),
        in_specs=[
            pl.BlockSpec((TM, W_BR), lambda i, j: (i, 0)),
            pl.BlockSpec((TM, TN), lambda i, j: (i, j)),
            pl.BlockSpec((None, W_BR, TN), lambda i, j: (l, 0, j)),
            pl.BlockSpec((None, 1, TN), lambda i, j: (l, 0, j)),
            pl.BlockSpec((TM, TN), lambda i, j: (i, goff + j)),
        ],
        out_specs=pl.BlockSpec((TM, TN), lambda i, j: (i, j)),
        compiler_params=_cparams("arbitrary", "arbitrary"),
        name="s5_glu",
    )(y, y, w_glu, b_glu3, proj)


def _expm1(x):
    t = jnp.tanh(0.5 * x)
    return 2.0 * t / (1.0 - t)


def _lru_kernel(*refs, L, B, nseg, period):
    if nseg > 1:
        (x_ref, g_ref, cw_ref, cb_ref, wa_ref, ba_ref, wx_ref, bx_ref, sp_ref, h0_ref, _,
         y_ref, a_ref, b_ref, ytok_ref, fl_ref, hp_ref) = refs
    else:
        (x_ref, g_ref, cw_ref, cb_ref, wa_ref, ba_ref, wx_ref, bx_ref, sp_ref,
         y_ref, fin_ref, a_ref, b_ref, ytok_ref) = refs
    nb = B // nseg
    _gather_tm(x_ref, a_ref.at[0], L, B)
    x = a_ref[0]
    pos = lax.broadcasted_iota(jnp.int32, (L, B, LANE), 0) % period
    xc = jnp.broadcast_to(cb_ref[...].reshape(1, 1, LANE), (L, B, LANE))
    for k in range(CONV_W):
        off = k - CONV_W // 2
        if off < 0:
            xs = jnp.concatenate([jnp.zeros((-off, B, LANE), F32), x[:L + off]], axis=0)
        elif off > 0:
            xs = jnp.concatenate([x[off:], jnp.zeros((off, B, LANE), F32)], axis=0)
        else:
            xs = x
        if period < L:
            xs = jnp.where((pos + off >= 0) & (pos + off < period), xs, 0.0)
        xc = xc + cw_ref[k:k + 1, :].reshape(1, 1, LANE) * xs
    xc2 = xc.reshape(L * B, LANE)
    xcb = xc2.astype(BF16)
    for d in range(2):
        r = _sigmoid(jnp.dot(xcb, wa_ref[d, 0].astype(BF16), preferred_element_type=F32)
                     + ba_ref[d:d + 1, :])
        ig = _sigmoid(jnp.dot(xcb, wx_ref[d, 0].astype(BF16), preferred_element_type=F32)
                      + bx_ref[d:d + 1, :])
        log_a = (-LRU_C) * r * sp_ref[d:d + 1, :]
        a_ref[d] = jnp.exp(log_a).reshape(L, B, LANE)
        mult = jnp.sqrt(-_expm1(2.0 * log_a))
        b_ref[d] = (mult * (ig * xc2)).reshape(L, B, LANE)

    def load_ab(i):
        out = []
        for d in range(2):
            t = i if d == 0 else L - 1 - i
            out += [a_ref[d, t], b_ref[d, t]]
        return out

    def step(i, carry):
        nxt = load_ab(jnp.minimum(i + 1, L - 1))
        out = []
        for d in range(2):
            t = i if d == 0 else L - 1 - i
            a = carry[4 + 2 * d]
            h = a * carry[2 * d] + carry[4 + 2 * d + 1]
            b_ref[d, t] = h
            if nseg > 1:
                p = a * carry[2 * d + 1]
                a_ref[d, t] = p
            else:
                p = carry[2 * d + 1]
            out += [h, p]
        return tuple(out + nxt)

    zero = jnp.zeros((B, LANE), F32)
    one = jnp.ones((B, LANE), F32)
    carry = lax.fori_loop(0, L, step, (zero, one, zero, one) + tuple(load_ab(0)), unroll=8)

    if nseg == 1:
        fin_ref[0] = carry[0]
        fin_ref[1] = carry[2]
    else:
        hps = []
        for d in range(2):
            fl_ref[0] = carry[2 * d]
            fl_ref[1] = carry[2 * d + 1]
            order = list(range(nseg)) if d == 0 else list(range(nseg - 1, -1, -1))
            for bb in range(nb):
                c = h0_ref[d, bb:bb + 1, :]
                for k, s in enumerate(order):
                    if k > 0:
                        rp = bb * nseg + order[k - 1]
                        c = fl_ref[0, rp:rp + 1, :] + fl_ref[1, rp:rp + 1, :] * c
                    r = bb * nseg + s
                    hp_ref[d, r:r + 1, :] = c
            hps.append(hp_ref[d])

        def cstep(i, c):
            nxt = load_ab(jnp.minimum(i + 1, L - 1))
            for d in range(2):
                t = i if d == 0 else L - 1 - i
                b_ref[d, t] = c[2 * d + 1] + c[2 * d] * hps[d]
            return tuple(nxt)

        lax.fori_loop(0, L, cstep, tuple(load_ab(0)), unroll=4)

    a_ref[0] = b_ref[0] + b_ref[1]
    _scatter_tm(a_ref.at[0], ytok_ref, L, B)
    y_ref[...] = (ytok_ref[...] * _silu(g_ref[...])).astype(BF16)


def _lru_call(src, col_x, col_g, conv_w, conv_b, w_a, b_a, w_x, b_x, sp, h0, y_prev,
              *, L, B, nseg, period, rblk, n_rows):
    w = conv_b.shape[-1]
    nblk = w // LANE

    def rows(col0):
        return pl.BlockSpec((L * B, LANE), lambda k: (rblk, col0 // LANE + k))

    y_spec = pl.BlockSpec((L * B, LANE), lambda k: (rblk, k))
    in_specs = [
        rows(col_x), rows(col_g),
        pl.BlockSpec((CONV_W, LANE), lambda k: (0, k)),
        pl.BlockSpec((1, LANE), lambda k: (0, k)),
        pl.BlockSpec((2, 1, LANE, LANE), lambda k: (0, k, 0, 0)),
        pl.BlockSpec((2, LANE), lambda k: (0, k)),
        pl.BlockSpec((2, 1, LANE, LANE), lambda k: (0, k, 0, 0)),
        pl.BlockSpec((2, LANE), lambda k: (0, k)),
        pl.BlockSpec((2, LANE), lambda k: (0, k)),
    ]
    args = [src, src, conv_w, conv_b.reshape(1, w), w_a, b_a, w_x, b_x, sp]
    y_shape = jax.ShapeDtypeStruct((n_rows, w), BF16)
    scratch = [pltpu.VMEM((2, L, B, LANE), F32), pltpu.VMEM((2, L, B, LANE), F32),
               pltpu.VMEM((L * B, LANE), F32)]
    aliases = {}
    if nseg > 1:
        nb = B // nseg
        in_specs += [pl.BlockSpec((2, nb, LANE), lambda k: (0, 0, k)), pl.BlockSpec(memory_space=pl.ANY)]
        args += [h0, y_prev]
        aliases = {len(args) - 1: 0}
        out_shape, out_specs = y_shape, y_spec
        scratch += [pltpu.VMEM((2, B, LANE), F32), pltpu.VMEM((2, B, LANE), F32)]
    else:
        out_shape = [y_shape, jax.ShapeDtypeStruct((2, B, w), F32)]
        out_specs = [y_spec, pl.BlockSpec((2, B, LANE), lambda k: (0, 0, k))]
    return pl.pallas_call(
        functools.partial(_lru_kernel, L=L, B=B, nseg=nseg, period=period),
        out_shape=out_shape,
        grid=(nblk,),
        in_specs=in_specs,
        out_specs=out_specs,
        scratch_shapes=scratch,
        input_output_aliases=aliases,
        compiler_params=_cparams("arbitrary"),
        name="rglru_seg" if nseg > 1 else "rglru",
    )(*args)


def _nt_dot(a, b):
    return lax.dot_general(a, b, (((1,), (1,)), ((), ())), preferred_element_type=F32)


def _hgrn_kernel(*refs, L, has_h0, n_aliased):
    q_ref, ff_ref, fb_ref, v_ref, g_ref, lb_ref, nw_ref = refs[:7]
    h0_ref = refs[7] if has_h0 else None
    y_ref, fin_ref, of_ref = refs[7 + has_h0 + n_aliased:]
    rb_rows = HG_RB
    nrb = L // rb_rows
    nsub = HG_ST // HG_CHUNK
    nch = rb_rows // HG_ST
    nscb = rb_rows // HG_SCB
    row = lax.broadcasted_iota(jnp.int32, (HG_SCB, HG_SCB), 0)
    col = lax.broadcasted_iota(jnp.int32, (HG_SCB, HG_SCB), 1)
    sub_dist = jnp.where((row // HG_ST) == (col // HG_ST), row // HG_CHUNK - col // HG_CHUNK, 2 * nsub)
    ridx = lax.broadcasted_iota(jnp.int32, (rb_rows, HG_DK), 0)
    pos = ridx % HG_CHUNK
    sic = (ridx // HG_CHUNK) % nsub
    chunk_id = ridx // HG_ST

    def sub_bcast(x, idx):
        x3 = x.reshape(rb_rows // HG_CHUNK, HG_CHUNK, HG_DK)
        return jnp.broadcast_to(x3[:, idx:idx + 1, :], x3.shape).reshape(rb_rows, HG_DK)

    def chunk_bcast(x, idx):
        x3 = x.reshape(nch, HG_ST, HG_DK)
        return jnp.broadcast_to(x3[:, idx:idx + 1, :], x3.shape).reshape(rb_rows, HG_DK)

    shared = {}

    def load_qv(rows, cs, key):
        if key is not None and key in shared:
            return shared[key]
        v = v_ref[rows, cs]
        out = (_silu(q_ref[rows, cs]), v.astype(BF16), v.T.astype(BF16))
        if key is not None:
            shared[key] = out
        return out

    def run_block(rb, st, d, hh):
        cs = slice(hh * HG_DK, (hh + 1) * HG_DK)
        if isinstance(rb, int):
            rows = slice(rb * rb_rows, (rb + 1) * rb_rows)
        else:
            rows = pl.ds(pl.multiple_of(rb * rb_rows, rb_rows), rb_rows)
        qf, vb, vt = load_qv(rows, cs, (hh, rb) if isinstance(rb, int) else None)
        f = (ff_ref if d == 0 else fb_ref)[rows, cs]
        lbd = lb_ref[d:d + 1, cs]
        g = lbd + (1.0 - lbd) * _sigmoid(f)
        kk = 1.0 - g
        b = jnp.log2(g)
        sgn = 1 if d == 0 else -1

        def shift(x, n):
            return pltpu.roll(x, n if d == 0 else rb_rows - n, 0)

        s = 1
        while s < HG_CHUNK:
            keep = (pos >= s) if d == 0 else (pos <= HG_CHUNK - 1 - s)
            b = b + jnp.where(keep, shift(b, s), 0.0)
            s *= 2
        tot = sub_bcast(b, HG_CHUNK - 1 if d == 0 else 0)
        prev = [shift(tot, HG_CHUNK * n) for n in range(1, nsub)]
        seen = sic if d == 0 else nsub - 1 - sic
        base = sum(jnp.where(seen >= n, prev[n - 1], 0.0) for n in range(1, nsub))
        ctot = chunk_bcast(b + base, HG_ST - 1 if d == 0 else 0)

        ed = jnp.exp2(b - sub_bcast(b, HG_CHUNK // 2))
        qt = (qf * ed).astype(BF16)
        kt = (kk / ed).astype(BF16)
        qe = qf * jnp.exp2(b)
        kd = kk * jnp.exp2(tot - b)
        lhs = [qe.astype(BF16)]
        acc = None
        for n in range(1, nsub - 1):
            acc = prev[n - 1] if acc is None else acc + prev[n - 1]
            lhs.append((qe * jnp.exp2(acc)).astype(BF16))
        kdb = kd.astype(BF16)
        causal = (col <= row) if d == 0 else (col >= row)
        o_parts = []
        for hb in range(nscb):
            rs = slice(hb * HG_SCB, (hb + 1) * HG_SCB)
            sc = jnp.where((sub_dist == 0) & causal, _nt_dot(qt[rs], kt[rs]), 0.0)
            scn = _nt_dot(jnp.concatenate([x[rs] for x in lhs], axis=0), kdb[rs])
            for n in range(1, nsub):
                sc = jnp.where(sub_dist == sgn * n, scn[(n - 1) * HG_SCB:n * HG_SCB], sc)
            o_parts.append(jnp.dot(sc.astype(BF16), vb[rs], preferred_element_type=F32))
        o = jnp.concatenate(o_parts, axis=0)

        qe_st = (qe * jnp.exp2(base)).astype(BF16)
        kd_st = kd * jnp.exp2(ctot - base - tot)
        uts = []
        for c in range(0, nch, 2):
            rhs = jnp.concatenate([jnp.where(chunk_id == c, kd_st, 0.0),
                                   jnp.where(chunk_id == c + 1, kd_st, 0.0)], axis=1).astype(BF16)
            ut2 = jnp.dot(vt, rhs, preferred_element_type=F32)
            uts += [ut2[:, :HG_DK], ut2[:, HG_DK:]]
        order = range(nch) if d == 0 else range(nch - 1, -1, -1)
        entering = [None] * nch
        for c in order:
            entering[c] = st
            tot_row = c * HG_ST + (HG_ST - 1 if d == 0 else 0)
            st = st * jnp.exp2(ctot[tot_row:tot_row + 1, :]) + uts[c]
        outs = []
        for c in range(nch):
            lo = c * HG_ST
            oi = _nt_dot(qe_st[lo:lo + HG_ST], entering[c].astype(BF16))
            outs.append(o[lo:lo + HG_ST] + oi)
        return jnp.concatenate(outs, axis=0), st

    streams = [(hh, d) for hh in range(HG_HPB) for d in range(2)]

    def body(i, sts):
        new = []
        for (hh, d), st in zip(streams, sts):
            rb = i if d == 0 else nrb - 1 - i
            oblk, st = run_block(rb, st, d, hh)
            cs = slice(hh * HG_DK, (hh + 1) * HG_DK)
            if isinstance(rb, int):
                of_ref[d, rb * rb_rows:(rb + 1) * rb_rows, cs] = oblk
            else:
                of_ref[d, pl.ds(pl.multiple_of(rb * rb_rows, rb_rows), rb_rows), cs] = oblk
            new.append(st)
        return tuple(new)

    if has_h0:
        sts = tuple(h0_ref[0, d, hh].T for hh, d in streams)
    else:
        sts = tuple(jnp.zeros((HG_DK, HG_DK), F32) for _ in streams)
    sts = body(0, sts) if nrb == 1 else lax.fori_loop(0, nrb, body, sts)
    for (hh, d), st in zip(streams, sts):
        fin_ref[0, d, hh] = st.T

    for hh in range(HG_HPB):
        cs = slice(hh * HG_DK, (hh + 1) * HG_DK)
        o = of_ref[0, :, cs] + of_ref[1, :, cs]
        o = o * lax.rsqrt(jnp.mean(o * o, axis=-1, keepdims=True) + RMS_EPS)
        y_ref[:, cs] = (o * nw_ref[:, cs] * _silu(g_ref[:, cs])).astype(BF16)


def _hgrn_call(proj, lb, norm_w, h0, y_prev, fin_prev, *, L, nb, row0, n_rows, layer,
               cols=(C_QC, C_FF, C_FB, C_IC, C_GC)):
    w = lb.shape[-1]
    nh = w // HG_DK
    rblk0 = row0 // L
    hw = HG_HPB * HG_DK

    def col(off):
        return pl.BlockSpec((L, hw), lambda b, h: (rblk0 + b, off // hw + h))

    in_specs = [col(c) for c in cols] + [
                pl.BlockSpec((2, hw), lambda b, h: (0, h)),
                pl.BlockSpec((1, hw), lambda b, h: (0, h))]
    args = [proj] * 5 + [lb, norm_w.reshape(1, w)]
    if h0 is not None:
        in_specs.append(pl.BlockSpec((1, 2, HG_HPB, HG_DK, HG_DK), lambda b, h: (b, 0, h, 0, 0)))
        args.append(h0)
    aliases = {}
    for out_idx, prev in enumerate((y_prev, fin_prev)):
        if prev is not None:
            in_specs.append(pl.BlockSpec(memory_space=pl.ANY))
            args.append(prev)
            aliases[len(args) - 1] = out_idx
    st_spec = pl.BlockSpec((1, None, 2, HG_HPB, HG_DK, HG_DK), lambda b, h: (b, layer, 0, h, 0, 0))
    return pl.pallas_call(
        functools.partial(_hgrn_kernel, L=L, has_h0=h0 is not None, n_aliased=len(aliases)),
        out_shape=[jax.ShapeDtypeStruct((n_rows, w), BF16),
                   jax.ShapeDtypeStruct((nb, DEPTH, 2, nh, HG_DK, HG_DK), F32)],
        grid=(nb, nh // HG_HPB),
        in_specs=in_specs,
        out_specs=[pl.BlockSpec((L, hw), lambda b, h: (rblk0 + b, h)), st_spec],
        scratch_shapes=[pltpu.VMEM((2, L, hw), F32)],
        input_output_aliases=aliases,
        compiler_params=_cparams("arbitrary", "arbitrary"),
        name="hgrn2",
    )(*args)


def _merge_kernel(ya_ref, yb_ref, yc_ref, wa_ref, wb_ref, wc_ref, ma_ref, mb_ref, mc_ref, o_ref):
    acc = None
    for y_ref, w_ref, m_ref in ((ya_ref, wa_ref, ma_ref), (yb_ref, wb_ref, mb_ref), (yc_ref, wc_ref, mc_ref)):
        t = jnp.dot(y_ref[...], w_ref[...].astype(BF16), preferred_element_type=F32)
        t = _sigmoid(m_ref[...]) * t
        acc = t if acc is None else acc + t
    o_ref[...] = acc.astype(BF16)


def _merge_call(ya, yb, yc, w_br, proj, l):
    tm, tn = TM_MERGE, TN_MERGE
    moff = C_MA // tn
    mstep = D_MODEL // tn
    lhs = pl.BlockSpec((tm, W_BR), lambda i, j: (i, 0))

    def wspec(br):
        return pl.BlockSpec((None, None, W_BR, tn), lambda i, j: (l, br, 0, j))

    def mspec(br):
        return pl.BlockSpec((tm, tn), lambda i, j: (i, moff + br * mstep + j))

    return pl.pallas_call(
        _merge_kernel,
        out_shape=jax.ShapeDtypeStruct((N_TOK, D_MODEL), BF16),
        grid=(N_TOK // tm, D_MODEL // tn),
        in_specs=[lhs, lhs, lhs, wspec(0), wspec(1), wspec(2), mspec(0), mspec(1), mspec(2)],
        out_specs=pl.BlockSpec((tm, tn), lambda i, j: (i, j)),
        compiler_params=_cparams("arbitrary", "arbitrary"),
        name="branch_merge",
    )(ya, yb, yc, w_br, w_br, w_br, proj, proj, proj)


def _outproj_kernel(m_ref, w_ref, b_ref, xc_ref, xl_ref, gate_ref, o_ref):
    out = jnp.dot(m_ref[...], w_ref[...].astype(BF16), preferred_element_type=F32) + b_ref[...]
    gated = gate_ref[0] * out
    is_ctx = pl.program_id(0) < N_CTX // TM

    @pl.when(is_ctx)
    def _():
        o_ref[...] = DN_ALPHA * xc_ref[...] + gated

    @pl.when(jnp.logical_not(is_ctx))
    def _():
        o_ref[...] = DN_ALPHA * xl_ref[...] + gated


def _outproj_call(merged, w_out, b_out3, x_ctx, x_lat, mods3, l):
    goff = 2 * D_MODEL // TN
    ctx_map, lat_map = _split_rows(TM, 1)
    return pl.pallas_call(
        _outproj_kernel,
        out_shape=jax.ShapeDtypeStruct((N_TOK, D_MODEL), F32),
        grid=(N_TOK // TM, D_MODEL // TN),
        in_specs=[
            pl.BlockSpec((TM, D_MODEL), lambda i, j: (i, 0)),
            pl.BlockSpec((None, D_MODEL, TN), lambda i, j: (l, 0, j)),
            pl.BlockSpec((None, 1, TN), lambda i, j: (l, 0, j)),
            pl.BlockSpec((TM, TN), ctx_map),
            pl.BlockSpec((TM, TN), lat_map),
            pl.BlockSpec((1, 1, TN), lambda i, j: (_mod_row_mm(i), 0, goff + j)),
        ],
        out_specs=pl.BlockSpec((TM, TN), lambda i, j: (i, j)),
        compiler_params=_cparams("arbitrary", "arbitrary"),
        name="out_proj",
    )(merged, w_out, b_out3, x_ctx, x_lat, mods3)


def _ln_affine_kernel(r_ref, g_ref, b_ref, o_ref):
    r = r_ref[...]
    mu = jnp.mean(r, axis=-1, keepdims=True)
    rc = r - mu
    var = jnp.mean(rc * rc, axis=-1, keepdims=True)
    o_ref[...] = rc * lax.rsqrt(var + LN_EPS) * g_ref[...] + b_ref[...]


def _ln_affine_call(r, g, b, l, row0=0, n_rows=N_TOK):
    rblk0 = row0 // TM_LN
    return pl.pallas_call(
        _ln_affine_kernel,
        out_shape=jax.ShapeDtypeStruct((n_rows, D_MODEL), F32),
        grid=(n_rows // TM_LN,),
        in_specs=[
            pl.BlockSpec((TM_LN, D_MODEL), lambda i: (rblk0 + i, 0)),
            pl.BlockSpec((None, 1, D_MODEL), lambda i: (l, 0, 0)),
            pl.BlockSpec((None, 1, D_MODEL), lambda i: (l, 0, 0)),
        ],
        out_specs=pl.BlockSpec((TM_LN, D_MODEL), lambda i: (i, 0)),
        compiler_params=_cparams("arbitrary"),
        name="post_ln",
    )(r, g, b)


def kernel(x_prompt, x_sample, state_s5, state_lru, state_hgrn, c, c_ctx, w_ada, b_ada, w_in, b_in,
           s5_a_re, s5_a_im, s5_log_dt, s5_b_re, s5_b_im, s5_c_re, s5_c_im, s5_d, s5_w_glu, s5_b_glu,
           lru_conv_w, lru_conv_b, lru_w_a, lru_b_a, lru_w_x, lru_b_x, lru_lambda, hg_lb, hg_norm_w,
           w_br, w_out, b_out, ln_g, ln_b):
    lb_soft = jax.nn.softmax(hg_lb.astype(F32), axis=0)
    lb_all = jnp.cumsum(lb_soft, axis=0) - lb_soft[0]
    softplus_neg_lam = jax.nn.softplus(-lru_lambda.astype(F32))

    cvec = jnp.zeros((SUBLANE, D_MODEL), F32)
    cvec = cvec.at[0].set(c_ctx).at[1:1 + DEC_BATCH].set(c)
    mods = _mods_call(cvec, w_ada, b_ada)

    b_in3 = b_in.reshape(DEPTH, 1, IN_COLS)
    b_glu3 = s5_b_glu.reshape(DEPTH, 1, W_BR)
    b_out3 = b_out.reshape(DEPTH, 1, D_MODEL)
    ln_g3 = ln_g.reshape(DEPTH, 1, D_MODEL)
    ln_b3 = ln_b.reshape(DEPTH, 1, D_MODEL)

    x_ctx = x_prompt.reshape(N_CTX, D_MODEL)
    x_lat = x_sample.reshape(N_LAT, D_MODEL)
    st_s5, st_lru, st_hg = [], [], None
    for l in range(DEPTH):
        mods3 = mods[l, :1 + DEC_BATCH].reshape(1 + DEC_BATCH, 1, 3 * D_MODEL)
        h = _ln_mod_call(x_ctx, x_lat, mods3)
        proj = _inproj_call(h, w_in, b_in3, l)

        lam_re, lam_im, z_re, z_im = _s5_disc_call(s5_a_re[l], s5_a_im[l], s5_log_dt[l])
        bp, cp, lam = _s5_pack(lam_re, lam_im, z_re, z_im, s5_b_re[l], s5_b_im[l], s5_c_re[l], s5_c_im[l])
        ctx_rows = dict(L=SEQ, B=BATCH, nseg=1, rblk=0, n_rows=N_TOK)
        lat_rows = dict(L=SEQ, B=LAT_ROWS, nseg=LAT_SEG, rblk=N_CTX // N_LAT, n_rows=N_TOK)
        y_pre, fin_s5 = _s5_call(proj, C_UA, bp, cp, lam, s5_d[l], None, None, **ctx_rows)
        y_pre = _s5_call(proj, C_UA, bp, cp, lam, s5_d[l], _s5_pack_state(state_s5[:, l]), y_pre, **lat_rows)
        y_a = _glu_call(y_pre, s5_w_glu, b_glu3, proj, l)
        st_s5.append(_s5_unpack_state(fin_s5, BATCH))

        lru_args = (lru_conv_w[l], lru_conv_b[l], lru_w_a[l], lru_b_a[l], lru_w_x[l], lru_b_x[l],
                    softplus_neg_lam[l])
        y_b, fin_lru = _lru_call(proj, C_XB, C_GB, *lru_args, None, None, period=SEQ, **ctx_rows)
        y_b = _lru_call(proj, C_XB, C_GB, *lru_args, jnp.transpose(state_lru[:, l], (1, 0, 2)), y_b,
                        period=GRID_W, **lat_rows)
        st_lru.append(jnp.transpose(fin_lru, (1, 0, 2)))

        y_c, st_hg = _hgrn_call(proj, lb_all[l], hg_norm_w[l], None, None, st_hg,
                                L=SEQ, nb=BATCH, row0=0, n_rows=N_TOK, layer=l)
        y_c, _ = _hgrn_call(proj, lb_all[l], hg_norm_w[l], state_hgrn[:, l], y_c, None,
                            L=DEC_SEQ, nb=DEC_BATCH, row0=N_CTX, n_rows=N_TOK, layer=0)

        merged = _merge_call(y_a, y_b, y_c, w_br, proj, l)
        r = _outproj_call(merged, w_out, b_out3, x_ctx, x_lat, mods3, l)
        x_ctx = _ln_affine_call(r, ln_g3, ln_b3, l, 0, N_CTX)
        x_lat = _ln_affine_call(r, ln_g3, ln_b3, l, N_CTX, N_LAT)

    y_prompt = x_ctx.reshape(BATCH, SEQ, D_MODEL)
    y_sample = x_lat.reshape(DEC_BATCH, DEC_SEQ, D_MODEL)
    new_state_s5 = jnp.stack(st_s5, axis=1)
    new_state_lru = jnp.stack(st_lru, axis=1)
    new_state_hgrn = st_hg
    return (y_prompt, y_sample, new_state_s5, new_state_lru, new_state_hgrn)
```

```python
import functools
import math

import jax
import jax.numpy as jnp
from jax import lax
from jax.experimental import pallas as pl
from jax.experimental.pallas import tpu as pltpu

F32 = jnp.float32
BF16 = jnp.bfloat16

LANE = 128
SUBLANE = 8
VMEM_LIMIT = 56 * 1024 * 1024

D_MODEL = 4096
DEPTH = 2
BATCH, SEQ = 16, 256
DEC_BATCH, DEC_SEQ = 2, 1024
GRID_W = 64
W_BR = D_MODEL // 2
S5_GROUP = 16
S5_STATE = 64
LRU_C = 8.0
CONV_W = 4
HG_DK = 128
HG_CHUNK = 16
N_BRANCH = 3
IN_COLS = 9 * W_BR + N_BRANCH * D_MODEL
DN_ALPHA = (2 * DEPTH) ** 0.25
LN_EPS = 1e-5
RMS_EPS = 1e-6

N_CTX = BATCH * SEQ
N_LAT = DEC_BATCH * DEC_SEQ
N_TOK = N_CTX + N_LAT
LAT_SEG = DEC_SEQ // SEQ
LAT_ROWS = LAT_SEG * DEC_BATCH

C_UA, C_GA, C_XB, C_GB, C_QC, C_FF, C_FB, C_IC, C_GC = (i * W_BR for i in range(9))
C_MA = 9 * W_BR

TM = 1024
TN = 512
TM_MERGE = 1024
TN_MERGE = 256
TM_LN = 256
HG_RB = 256
HG_ST = 64
HG_HPB = 2
HG_SCB = 128
S5_PAIRS = 4


def _cparams(*sem):
    return pltpu.CompilerParams(dimension_semantics=sem, vmem_limit_bytes=VMEM_LIMIT)


def _sigmoid(x):
    return 0.5 * jnp.tanh(0.5 * x) + 0.5


def _silu(x):
    return x * _sigmoid(x)


def _mods_kernel(c_ref, w_ref, b_ref, o_ref):
    c = c_ref[...]
    s = _silu(c).astype(BF16)
    w = w_ref[...].astype(BF16)
    o_ref[...] = jnp.dot(s, w, preferred_element_type=F32) + b_ref[...]


def _mods_call(cvec, w_ada, b_ada):
    tn = TN
    return pl.pallas_call(
        _mods_kernel,
        out_shape=jax.ShapeDtypeStruct((DEPTH, SUBLANE, 3 * D_MODEL), F32),
        grid=(DEPTH, 3 * D_MODEL // tn),
        in_specs=[
            pl.BlockSpec((SUBLANE, D_MODEL), lambda l, j: (0, 0)),
            pl.BlockSpec((None, D_MODEL, tn), lambda l, j: (l, 0, j)),
            pl.BlockSpec((None, 1, tn), lambda l, j: (l, 0, j)),
        ],
        out_specs=pl.BlockSpec((None, SUBLANE, tn), lambda l, j: (l, 0, j)),
        compiler_params=_cparams("arbitrary", "arbitrary"),
        name="adaln_mods",
    )(cvec, w_ada, b_ada.reshape(DEPTH, 1, 3 * D_MODEL))


def _mod_row_ln(i):
    n_ctx_tiles = N_CTX // TM_LN
    return jnp.where(i < n_ctx_tiles, 0, 1 + (i - n_ctx_tiles) // (DEC_SEQ // TM_LN))


def _mod_row_mm(i):
    return jnp.maximum(i - (N_CTX // TM - 1), 0)


def _ln_mod_kernel(xc_ref, xl_ref, shift_ref, scale_ref, o_ref):
    def emit(x_ref):
        x = x_ref[...]
        mu = jnp.mean(x, axis=-1, keepdims=True)
        xc = x - mu
        var = jnp.mean(xc * xc, axis=-1, keepdims=True)
        h = xc * lax.rsqrt(var + LN_EPS) * (1.0 + scale_ref[0]) + shift_ref[0]
        o_ref[...] = h.astype(BF16)

    is_ctx = pl.program_id(0) < N_CTX // TM_LN
    pl.when(is_ctx)(lambda: emit(xc_ref))
    pl.when(jnp.logical_not(is_ctx))(lambda: emit(xl_ref))


def _split_rows(tile, n_col_axes):
    n_ctx_tiles = N_CTX // tile
    if n_col_axes == 0:
        return (lambda i: (jnp.minimum(i, n_ctx_tiles - 1), 0),
                lambda i: (jnp.maximum(i - n_ctx_tiles, 0), 0))
    return (lambda i, j: (jnp.minimum(i, n_ctx_tiles - 1), j),
            lambda i, j: (jnp.maximum(i - n_ctx_tiles, 0), j))


def _ln_mod_call(x_ctx, x_lat, mods3):
    ctx_map, lat_map = _split_rows(TM_LN, 0)
    return pl.pallas_call(
        _ln_mod_kernel,
        out_shape=jax.ShapeDtypeStruct((N_TOK, D_MODEL), BF16),
        grid=(N_TOK // TM_LN,),
        in_specs=[
            pl.BlockSpec((TM_LN, D_MODEL), ctx_map),
            pl.BlockSpec((TM_LN, D_MODEL), lat_map),
            pl.BlockSpec((1, 1, D_MODEL), lambda i: (_mod_row_ln(i), 0, 0)),
            pl.BlockSpec((1, 1, D_MODEL), lambda i: (_mod_row_ln(i), 0, 1)),
        ],
        out_specs=pl.BlockSpec((TM_LN, D_MODEL), lambda i: (i, 0)),
        compiler_params=_cparams("arbitrary"),
        name="ln_modulate",
    )(x_ctx, x_lat, mods3, mods3)


def _inproj_kernel(h_ref, w_ref, b_ref, o_ref):
    w = w_ref[...].astype(BF16)
    o_ref[...] = jnp.dot(h_ref[...], w, preferred_element_type=F32) + b_ref[...]


def _inproj_call(h, w_in, b_in3, l):
    return pl.pallas_call(
        _inproj_kernel,
        out_shape=jax.ShapeDtypeStruct((N_TOK, IN_COLS), F32),
        grid=(N_TOK // TM, IN_COLS // TN),
        in_specs=[
            pl.BlockSpec((TM, D_MODEL), lambda i, j: (i, 0)),
            pl.BlockSpec((None, D_MODEL, TN), lambda i, j: (l, 0, j)),
            pl.BlockSpec((None, 1, TN), lambda i, j: (l, 0, j)),
        ],
        out_specs=pl.BlockSpec((TM, TN), lambda i, j: (i, j)),
        compiler_params=_cparams("arbitrary", "arbitrary"),
        name="in_proj",
    )(h, w_in, b_in3)


def _s5_disc_kernel(are_ref, aim_ref, ldt_ref, lre_ref, lim_ref, zre_ref, zim_ref):
    ar = jnp.minimum(are_ref[...], -1e-4)
    ai = aim_ref[...]
    dt = jnp.exp(ldt_ref[...])
    mag = jnp.exp(dt * ar)
    lam_re = mag * jnp.cos(dt * ai)
    lam_im = mag * jnp.sin(dt * ai)
    den = ar * ar + ai * ai
    lre_ref[...] = lam_re
    lim_ref[...] = lam_im
    zre_ref[...] = ((lam_re - 1.0) * ar + lam_im * ai) / den
    zim_ref[...] = (lam_im * ar - (lam_re - 1.0) * ai) / den


def _s5_disc_call(a_re, a_im, log_dt):
    g = a_re.shape[1]
    shp = (2 * g, S5_STATE)
    ldt = jnp.broadcast_to(log_dt[..., None], (2, g, S5_STATE)).reshape(shp)
    outs = pl.pallas_call(
        _s5_disc_kernel,
        out_shape=[jax.ShapeDtypeStruct(shp, F32)] * 4,
        name="s5_discretise",
    )(a_re.reshape(shp), a_im.reshape(shp), ldt)
    return [o.reshape(2, g, S5_STATE) for o in outs]


def _s5_pack(lam_re, lam_im, z_re, z_im, b_re, b_im, c_re, c_im):
    g = lam_re.shape[1]
    nblk = g // (2 * S5_PAIRS)
    pair_rows = 2 * S5_GROUP
    bz_re = z_re[..., None] * b_re[None] - z_im[..., None] * b_im[None]
    bz_im = z_re[..., None] * b_im[None] + z_im[..., None] * b_re[None]

    def pairs(t):
        t = t.reshape(t.shape[:-3] + (g // 2, 2) + t.shape[-2:])
        return t[..., 0, :, :], t[..., 1, :, :]

    def embed(t, axis):
        t = t.reshape(t.shape[:-3] + (nblk, S5_PAIRS) + t.shape[-2:])
        out = []
        for j in range(S5_PAIRS):
            pad = [(0, 0)] * (t.ndim - 1)
            pad[axis] = (j * pair_rows, LANE - (j + 1) * pair_rows)
            out.append(jnp.pad(t[..., j, :, :], pad))
        return jnp.stack(out, axis=-3)

    br0, br1 = pairs(jnp.swapaxes(bz_re, -1, -2))
    bi0, bi1 = pairs(jnp.swapaxes(bz_im, -1, -2))
    zb = jnp.zeros_like(br0)
    bp = jnp.concatenate([jnp.concatenate([br0, zb, bi0, zb], axis=-1),
                          jnp.concatenate([zb, br1, zb, bi1], axis=-1)], axis=-2)
    bp = embed(bp, -2).astype(BF16)

    cr0, cr1 = pairs(jnp.swapaxes(c_re, -1, -2))
    ci0, ci1 = pairs(jnp.swapaxes(c_im, -1, -2))
    zc = jnp.zeros_like(cr0)
    cp = jnp.concatenate([jnp.concatenate([cr0, zc], axis=-1), jnp.concatenate([zc, cr1], axis=-1),
                          jnp.concatenate([-ci0, zc], axis=-1), jnp.concatenate([zc, -ci1], axis=-1)],
                         axis=-2)
    cp = embed(cp, -1).astype(BF16)

    def pack_l(t):
        return t.reshape(2, nblk, S5_PAIRS, LANE)

    lam = jnp.stack([pack_l(lam_re), pack_l(lam_im)], axis=-2)
    return bp, cp, lam


def _s5_pack_state(h0):
    b, _, g, _, _ = h0.shape
    nblk = g // (2 * S5_PAIRS)
    t = h0.reshape(b, 2, nblk, S5_PAIRS, 2, S5_STATE, 2)
    t = jnp.transpose(t, (1, 2, 3, 0, 6, 4, 5))
    return t.reshape(2, nblk, S5_PAIRS, b, 2 * LANE)


def _s5_unpack_state(fin, b):
    nblk = fin.shape[1]
    t = fin.reshape(2, nblk, S5_PAIRS, b, 2, 2, S5_STATE)
    t = jnp.transpose(t, (3, 0, 1, 2, 5, 6, 4))
    return t.reshape(b, 2, nblk * S5_PAIRS * 2, S5_STATE, 2)


def _cmul(ar, ai, br, bi):
    return ar * br - ai * bi, ar * bi + ai * br


def _tm_pitch(L):
    return L + SUBLANE


def _gather_tm(src_ref, dst_ref, pad_ref, L, B):
    pitch = _tm_pitch(L)
    for r in range(B):
        pad_ref[r * pitch:r * pitch + L, :] = src_ref[r * L:(r + 1) * L, :]

    def body(t, c):
        dst_ref[t] = pad_ref[pl.ds(t, B, stride=pitch), :]
        return c
    lax.fori_loop(0, L, body, 0, unroll=8)


def _scatter_tm(src_ref, pad_ref, L, B):
    pitch = _tm_pitch(L)

    def body(t, c):
        pad_ref[pl.ds(t, B, stride=pitch), :] = src_ref[t]
        return c
    lax.fori_loop(0, L, body, 0, unroll=8)


def _s5_kernel(*refs, L, B, nseg):
    if nseg > 1:
        (u_ref, bp_ref, cp_ref, lam_ref, d_ref, h0_ref, _, y_ref,
         xs_ref, yacc_ref, utm_ref, pad_ref, fl_ref, hp_ref) = refs
    else:
        u_ref, bp_ref, cp_ref, lam_ref, d_ref, y_ref, fin_ref, xs_ref, yacc_ref, utm_ref, pad_ref = refs
    nb = B // nseg
    _gather_tm(u_ref, utm_ref, pad_ref, L, B)
    u2 = utm_ref[...].reshape(L * B, LANE)
    u2b = u2.astype(BF16)
    yacc_ref[...] = jnp.zeros((L * B, LANE), F32)

    for half in range(S5_PAIRS // 2):
        chains = [(d, jj) for d in range(2) for jj in range(2)]
        for d, jj in chains:
            x = jnp.dot(u2b, bp_ref[d, 0, 2 * half + jj], preferred_element_type=F32)
            xs_ref[d, jj] = x.reshape(L, B, 2 * LANE)
        lam = {}
        for d, jj in chains:
            lr = jnp.broadcast_to(lam_ref[d, 0, 2 * half + jj, 0:1, :], (B, LANE))
            li = jnp.broadcast_to(lam_ref[d, 0, 2 * half + jj, 1:2, :], (B, LANE))
            lam[(d, jj)] = (lr, li)

        def load_x(i):
            out = []
            for d, jj in chains:
                t = i if d == 0 else L - 1 - i
                out += [xs_ref[d, jj, t, :, 0:LANE], xs_ref[d, jj, t, :, LANE:2 * LANE]]
            return out

        nc = 2 * len(chains)

        def step(i, carry):
            nxt = load_x(jnp.minimum(i + 1, L - 1))
            out = []
            for n, (d, jj) in enumerate(chains):
                t = i if d == 0 else L - 1 - i
                hr, hi = carry[2 * n], carry[2 * n + 1]
                lr, li = lam[(d, jj)]
                pr, pi = _cmul(lr, li, hr, hi)
                nr = pr + carry[nc + 2 * n]
                ni = pi + carry[nc + 2 * n + 1]
                xs_ref[d, jj, t, :, 0:LANE] = nr
                xs_ref[d, jj, t, :, LANE:2 * LANE] = ni
                out += [nr, ni]
            return tuple(out + nxt)

        zero = jnp.zeros((B, LANE), F32)
        carry = lax.fori_loop(0, L, step, (zero,) * nc + tuple(load_x(0)), unroll=4)

        if nseg == 1:
            for n, (d, jj) in enumerate(chains):
                fin_ref[d, 0, 2 * half + jj, :, 0:LANE] = carry[2 * n]
                fin_ref[d, 0, 2 * half + jj, :, LANE:2 * LANE] = carry[2 * n + 1]
        else:
            hp = {}
            for n, (d, jj) in enumerate(chains):
                plr = lam_ref[d, 0, 2 * half + jj, 0:1, :]
                pli = lam_ref[d, 0, 2 * half + jj, 1:2, :]
                for _ in range(int(math.log2(L))):
                    plr, pli = _cmul(plr, pli, plr, pli)
                fl_ref[0] = carry[2 * n]
                fl_ref[1] = carry[2 * n + 1]
                order = list(range(nseg)) if d == 0 else list(range(nseg - 1, -1, -1))
                for bb in range(nb):
                    cr = h0_ref[d, 0, 2 * half + jj, bb:bb + 1, 0:LANE]
                    ci = h0_ref[d, 0, 2 * half + jj, bb:bb + 1, LANE:2 * LANE]
                    for k, s in enumerate(order):
                        if k > 0:
                            rp = bb * nseg + order[k - 1]
                            mr, mi = _cmul(plr, pli, cr, ci)
                            cr = fl_ref[0, rp:rp + 1, :] + mr
                            ci = fl_ref[1, rp:rp + 1, :] + mi
                        r = bb * nseg + s
                        hp_ref[n, 0, r:r + 1, :] = cr
                        hp_ref[n, 1, r:r + 1, :] = ci
                hp[(d, jj)] = (hp_ref[n, 0], hp_ref[n, 1])

            def cstep(i, pw):
                nxt = load_x(jnp.minimum(i + 1, L - 1))
                out = []
                for n, (d, jj) in enumerate(chains):
                    t = i if d == 0 else L - 1 - i
                    pr, pi = pw[2 * n], pw[2 * n + 1]
                    hr, hi = hp[(d, jj)]
                    ar, ai = _cmul(pr, pi, hr, hi)
                    xs_ref[d, jj, t, :, 0:LANE] = pw[nc + 2 * n] + ar
                    xs_ref[d, jj, t, :, LANE:2 * LANE] = pw[nc + 2 * n + 1] + ai
                    lr, li = lam[(d, jj)]
                    nr, ni = _cmul(lr, li, pr, pi)
                    out += [nr, ni]
                return tuple(out + nxt)

            pw0 = []
            for d, jj in chains:
                pw0 += list(lam[(d, jj)])
            lax.fori_loop(0, L, cstep, tuple(pw0 + load_x(0)), unroll=4)

        for jj in range(2):
            hs = (xs_ref[0, jj] + xs_ref[1, jj]).reshape(L * B, 2 * LANE).astype(BF16)
            yacc_ref[...] += jnp.dot(hs, cp_ref[0, 2 * half + jj], preferred_element_type=F32)

    y = u2 * d_ref[...] + yacc_ref[...]
    utm_ref[...] = jax.nn.gelu(y).reshape(L, B, LANE)
    _scatter_tm(utm_ref, pad_ref, L, B)
    pitch = _tm_pitch(L)
    for r in range(B):
        y_ref[r * L:(r + 1) * L, :] = pad_ref[r * pitch:r * pitch + L, :]


def _s5_call(src, col0, bp, cp, lam, d_skip, h0, y_prev, *, L, B, nseg, rblk, n_rows):
    w = d_skip.shape[-1]
    nblk = w // LANE
    cblk0 = col0 // LANE
    in_specs = [
        pl.BlockSpec((L * B, LANE), lambda k: (rblk, cblk0 + k)),
        pl.BlockSpec((2, 1, S5_PAIRS, LANE, 2 * LANE), lambda k: (0, k, 0, 0, 0)),
        pl.BlockSpec((1, S5_PAIRS, 2 * LANE, LANE), lambda k: (k, 0, 0, 0)),
        pl.BlockSpec((2, 1, S5_PAIRS, 2, LANE), lambda k: (0, k, 0, 0, 0)),
        pl.BlockSpec((1, LANE), lambda k: (0, k)),
    ]
    args = [src, bp, cp, lam, d_skip.reshape(1, w)]
    y_shape = jax.ShapeDtypeStruct((n_rows, w), F32)
    y_spec = pl.BlockSpec((L * B, LANE), lambda k: (rblk, k))
    scratch = [pltpu.VMEM((2, 2, L, B, 2 * LANE), F32), pltpu.VMEM((L * B, LANE), F32),
               pltpu.VMEM((L, B, LANE), F32), pltpu.VMEM((B * _tm_pitch(L), LANE), F32)]
    aliases = {}
    if nseg > 1:
        nb = B // nseg
        in_specs += [pl.BlockSpec((2, 1, S5_PAIRS, nb, 2 * LANE), lambda k: (0, k, 0, 0, 0)),
                     pl.BlockSpec(memory_space=pl.ANY)]
        args += [h0, y_prev]
        aliases = {len(args) - 1: 0}
        out_shape, out_specs = y_shape, y_spec
        scratch += [pltpu.VMEM((2, B, LANE), F32), pltpu.VMEM((4, 2, B, LANE), F32)]
    else:
        out_shape = [y_shape, jax.ShapeDtypeStruct((2, nblk, S5_PAIRS, B, 2 * LANE), F32)]
        out_specs = [y_spec, pl.BlockSpec((2, 1, S5_PAIRS, B, 2 * LANE), lambda k: (0, k, 0, 0, 0))]
    return pl.pallas_call(
        functools.partial(_s5_kernel, L=L, B=B, nseg=nseg),
        out_shape=out_shape,
        grid=(nblk,),
        in_specs=in_specs,
        out_specs=out_specs,
        scratch_shapes=scratch,
        input_output_aliases=aliases,
        compiler_params=_cparams("arbitrary"),
        name="s5_scan_seg" if nseg > 1 else "s5_scan",
    )(*args)


def _glu_kernel(yrow_ref, ytile_ref, w_ref, b_ref, g_ref, o_ref):
    w = w_ref[...].astype(BF16)
    z = jnp.dot(yrow_ref[...].astype(BF16), w, preferred_element_type=F32) + b_ref[...]
    o_ref[...] = (ytile_ref[...] * _sigmoid(z) * _silu(g_ref[...])).astype(BF16)


def _glu_call(y, w_glu, b_glu3, proj, l):
    goff = C_GA // TN
    return pl.pallas_call(
        _glu_kernel,
        out_shape=jax.ShapeDtypeStruct((N_TOK, W_BR), BF16),
        grid=(N_TOK // TM, W_BR // TN),
        in_specs=[
            pl.BlockSpec((TM, W_BR), lambda i, j: (i, 0)),
            pl.BlockSpec((TM, TN), lambda i, j: (i, j)),
            pl.BlockSpec((None, W_BR, TN), lambda i, j: (l, 0, j)),
            pl.BlockSpec((None, 1, TN), lambda i, j: (l, 0, j)),
            pl.BlockSpec((TM, TN), lambda i, j: (i, goff + j)),
        ],
        out_specs=pl.BlockSpec((TM, TN), lambda i, j: (i, j)),
        compiler_params=_cparams("arbitrary", "arbitrary"),
        name="s5_glu",
    )(y, y, w_glu, b_glu3, proj)


def _expm1(x):
    t = jnp.tanh(0.5 * x)
    return 2.0 * t / (1.0 - t)


def _lru_kernel(*refs, L, B, nseg, period):
    if nseg > 1:
        (x_ref, g_ref, cw_ref, cb_ref, wa_ref, ba_ref, wx_ref, bx_ref, sp_ref, h0_ref, _,
         y_ref, a_ref, b_ref, ytok_ref, fl_ref, hp_ref) = refs
    else:
        (x_ref, g_ref, cw_ref, cb_ref, wa_ref, ba_ref, wx_ref, bx_ref, sp_ref,
         y_ref, fin_ref, a_ref, b_ref, ytok_ref) = refs
    nb = B // nseg
    _gather_tm(x_ref, a_ref.at[0], ytok_ref, L, B)
    x = a_ref[0]
    pos = lax.broadcasted_iota(jnp.int32, (L, B, LANE), 0) % period
    xc = jnp.broadcast_to(cb_ref[...].reshape(1, 1, LANE), (L, B, LANE))
    for k in range(CONV_W):
        off = k - CONV_W // 2
        if off < 0:
            xs = jnp.concatenate([jnp.zeros((-off, B, LANE), F32), x[:L + off]], axis=0)
        elif off > 0:
            xs = jnp.concatenate([x[off:], jnp.zeros((off, B, LANE), F32)], axis=0)
        else:
            xs = x
        if period < L:
            xs = jnp.where((pos + off >= 0) & (pos + off < period), xs, 0.0)
        xc = xc + cw_ref[k:k + 1, :].reshape(1, 1, LANE) * xs
    xc2 = xc.reshape(L * B, LANE)
    xcb = xc2.astype(BF16)
    for d in range(2):
        r = _sigmoid(jnp.dot(xcb, wa_ref[d, 0].astype(BF16), preferred_element_type=F32)
                     + ba_ref[d:d + 1, :])
        ig = _sigmoid(jnp.dot(xcb, wx_ref[d, 0].astype(BF16), preferred_element_type=F32)
                      + bx_ref[d:d + 1, :])
        log_a = (-LRU_C) * r * sp_ref[d:d + 1, :]
        a_ref[d] = jnp.exp(log_a).reshape(L, B, LANE)
        mult = jnp.sqrt(-_expm1(2.0 * log_a))
        b_ref[d] = (mult * (ig * xc2)).reshape(L, B, LANE)

    def load_ab(i):
        out = []
        for d in range(2):
            t = i if d == 0 else L - 1 - i
            out += [a_ref[d, t], b_ref[d, t]]
        return out

    def step(i, carry):
        nxt = load_ab(jnp.minimum(i + 1, L - 1))
        out = []
        for d in range(2):
            t = i if d == 0 else L - 1 - i
            a = carry[4 + 2 * d]
            h = a * carry[2 * d] + carry[4 + 2 * d + 1]
            b_ref[d, t] = h
            if nseg > 1:
                p = a * carry[2 * d + 1]
                a_ref[d, t] = p
            else:
                p = carry[2 * d + 1]
            out += [h, p]
        return tuple(out + nxt)

    zero = jnp.zeros((B, LANE), F32)
    one = jnp.ones((B, LANE), F32)
    carry = lax.fori_loop(0, L, step, (zero, one, zero, one) + tuple(load_ab(0)), unroll=8)

    if nseg == 1:
        fin_ref[0] = carry[0]
        fin_ref[1] = carry[2]
    else:
        hps = []
        for d in range(2):
            fl_ref[0] = carry[2 * d]
            fl_ref[1] = carry[2 * d + 1]
            order = list(range(nseg)) if d == 0 else list(range(nseg - 1, -1, -1))
            for bb in range(nb):
                c = h0_ref[d, bb:bb + 1, :]
                for k, s in enumerate(order):
                    if k > 0:
                        rp = bb * nseg + order[k - 1]
                        c = fl_ref[0, rp:rp + 1, :] + fl_ref[1, rp:rp + 1, :] * c
                    r = bb * nseg + s
                    hp_ref[d, r:r + 1, :] = c
            hps.append(hp_ref[d])

        def cstep(i, c):
            nxt = load_ab(jnp.minimum(i + 1, L - 1))
            for d in range(2):
                t = i if d == 0 else L - 1 - i
                b_ref[d, t] = c[2 * d + 1] + c[2 * d] * hps[d]
            return tuple(nxt)

        lax.fori_loop(0, L, cstep, tuple(load_ab(0)), unroll=4)

    a_ref[0] = b_ref[0] + b_ref[1]
    _scatter_tm(a_ref.at[0], ytok_ref, L, B)
    pitch = _tm_pitch(L)
    for r in range(B):
        rows = slice(r * L, (r + 1) * L)
        y_ref[rows, :] = (ytok_ref[r * pitch:r * pitch + L, :] * _silu(g_ref[rows, :])).astype(BF16)


def _lru_call(src, col_x, col_g, conv_w, conv_b, w_a, b_a, w_x, b_x, sp, h0, y_prev,
              *, L, B, nseg, period, rblk, n_rows):
    w = conv_b.shape[-1]
    nblk = w // LANE

    def rows(col0):
        return pl.BlockSpec((L * B, LANE), lambda k: (rblk, col0 // LANE + k))

    y_spec = pl.BlockSpec((L * B, LANE), lambda k: (rblk, k))
    in_specs = [
        rows(col_x), rows(col_g),
        pl.BlockSpec((CONV_W, LANE), lambda k: (0, k)),
        pl.BlockSpec((1, LANE), lambda k: (0, k)),
        pl.BlockSpec((2, 1, LANE, LANE), lambda k: (0, k, 0, 0)),
        pl.BlockSpec((2, LANE), lambda k: (0, k)),
        pl.BlockSpec((2, 1, LANE, LANE), lambda k: (0, k, 0, 0)),
        pl.BlockSpec((2, LANE), lambda k: (0, k)),
        pl.BlockSpec((2, LANE), lambda k: (0, k)),
    ]
    args = [src, src, conv_w, conv_b.reshape(1, w), w_a, b_a, w_x, b_x, sp]
    y_shape = jax.ShapeDtypeStruct((n_rows, w), BF16)
    scratch = [pltpu.VMEM((2, L, B, LANE), F32), pltpu.VMEM((2, L, B, LANE), F32),
               pltpu.VMEM((B * _tm_pitch(L), LANE), F32)]
    aliases = {}
    if nseg > 1:
        nb = B // nseg
        in_specs += [pl.BlockSpec((2, nb, LANE), lambda k: (0, 0, k)), pl.BlockSpec(memory_space=pl.ANY)]
        args += [h0, y_prev]
        aliases = {len(args) - 1: 0}
        out_shape, out_specs = y_shape, y_spec
        scratch += [pltpu.VMEM((2, B, LANE), F32), pltpu.VMEM((2, B, LANE), F32)]
    else:
        out_shape = [y_shape, jax.ShapeDtypeStruct((2, B, w), F32)]
        out_specs = [y_spec, pl.BlockSpec((2, B, LANE), lambda k: (0, 0, k))]
    return pl.pallas_call(
        functools.partial(_lru_kernel, L=L, B=B, nseg=nseg, period=period),
        out_shape=out_shape,
        grid=(nblk,),
        in_specs=in_specs,
        out_specs=out_specs,
        scratch_shapes=scratch,
        input_output_aliases=aliases,
        compiler_params=_cparams("arbitrary"),
        name="rglru_seg" if nseg > 1 else "rglru",
    )(*args)


def _nt_dot(a, b):
    return lax.dot_general(a, b, (((1,), (1,)), ((), ())), preferred_element_type=F32)


def _hgrn_kernel(*refs, L, has_h0, n_aliased):
    q_ref, ff_ref, fb_ref, v_ref, g_ref, lb_ref, nw_ref = refs[:7]
    h0_ref = refs[7] if has_h0 else None
    y_ref, fin_ref, of_ref = refs[7 + has_h0 + n_aliased:]
    rb_rows = HG_RB
    nrb = L // rb_rows
    nsub = HG_ST // HG_CHUNK
    nch = rb_rows // HG_ST
    nscb = rb_rows // HG_SCB
    row = lax.broadcasted_iota(jnp.int32, (HG_SCB, HG_SCB), 0)
    col = lax.broadcasted_iota(jnp.int32, (HG_SCB, HG_SCB), 1)
    sub_dist = jnp.where((row // HG_ST) == (col // HG_ST), row // HG_CHUNK - col // HG_CHUNK, 2 * nsub)
    ridx = lax.broadcasted_iota(jnp.int32, (rb_rows, HG_DK), 0)
    pos = ridx % HG_CHUNK
    sic = (ridx // HG_CHUNK) % nsub
    chunk_id = ridx // HG_ST

    def sub_bcast(x, idx):
        x3 = x.reshape(rb_rows // HG_CHUNK, HG_CHUNK, HG_DK)
        return jnp.broadcast_to(x3[:, idx:idx + 1, :], x3.shape).reshape(rb_rows, HG_DK)

    def chunk_bcast(x, idx):
        x3 = x.reshape(nch, HG_ST, HG_DK)
        return jnp.broadcast_to(x3[:, idx:idx + 1, :], x3.shape).reshape(rb_rows, HG_DK)

    shared = {}

    def load_qv(rows, cs, key):
        if key is not None and key in shared:
            return shared[key]
        v = v_ref[rows, cs]
        out = (_silu(q_ref[rows, cs]), v.astype(BF16), v.T.astype(BF16))
        if key is not None:
            shared[key] = out
        return out

    def run_block(rb, st, d, hh):
        cs = slice(hh * HG_DK, (hh + 1) * HG_DK)
        if isinstance(rb, int):
            rows = slice(rb * rb_rows, (rb + 1) * rb_rows)
        else:
            rows = pl.ds(pl.multiple_of(rb * rb_rows, rb_rows), rb_rows)
        qf, vb, vt = load_qv(rows, cs, (hh, rb) if isinstance(rb, int) else None)
        f = (ff_ref if d == 0 else fb_ref)[rows, cs]
        lbd = lb_ref[d:d + 1, cs]
        g = lbd + (1.0 - lbd) * jax.nn.sigmoid(f)
        kk = 1.0 - g
        b = jnp.log2(g)
        sgn = 1 if d == 0 else -1

        def shift(x, n):
            return pltpu.roll(x, n if d == 0 else rb_rows - n, 0)

        s = 1
        while s < HG_CHUNK:
            keep = (pos >= s) if d == 0 else (pos <= HG_CHUNK - 1 - s)
            b = b + jnp.where(keep, shift(b, s), 0.0)
            s *= 2
        tot = sub_bcast(b, HG_CHUNK - 1 if d == 0 else 0)
        prev = [shift(tot, HG_CHUNK * n) for n in range(1, nsub)]
        seen = sic if d == 0 else nsub - 1 - sic
        base = sum(jnp.where(seen >= n, prev[n - 1], 0.0) for n in range(1, nsub))
        ctot = chunk_bcast(b + base, HG_ST - 1 if d == 0 else 0)

        ed = jnp.exp2(b - sub_bcast(b, HG_CHUNK // 2))
        qt = (qf * ed).astype(BF16)
        kt = (kk / ed).astype(BF16)
        qe = qf * jnp.exp2(b)
        kd = kk * jnp.exp2(tot - b)
        lhs = [qe.astype(BF16)]
        acc = None
        for n in range(1, nsub - 1):
            acc = prev[n - 1] if acc is None else acc + prev[n - 1]
            lhs.append((qe * jnp.exp2(acc)).astype(BF16))
        kdb = kd.astype(BF16)
        causal = (col <= row) if d == 0 else (col >= row)
        o_parts = []
        for hb in range(nscb):
            rs = slice(hb * HG_SCB, (hb + 1) * HG_SCB)
            sc = jnp.where((sub_dist == 0) & causal, _nt_dot(qt[rs], kt[rs]), 0.0)
            scn = _nt_dot(jnp.concatenate([x[rs] for x in lhs], axis=0), kdb[rs])
            for n in range(1, nsub):
                sc = jnp.where(sub_dist == sgn * n, scn[(n - 1) * HG_SCB:n * HG_SCB], sc)
            o_parts.append(jnp.dot(sc.astype(BF16), vb[rs], preferred_element_type=F32))
        o = jnp.concatenate(o_parts, axis=0)

        qe_st = (qe * jnp.exp2(base)).astype(BF16)
        kd_st = kd * jnp.exp2(ctot - base - tot)
        uts = []
        for c in range(0, nch, 2):
            rhs = jnp.concatenate([jnp.where(chunk_id == c, kd_st, 0.0),
                                   jnp.where(chunk_id == c + 1, kd_st, 0.0)], axis=1).astype(BF16)
            ut2 = jnp.dot(vt, rhs, preferred_element_type=F32)
            uts += [ut2[:, :HG_DK], ut2[:, HG_DK:]]
        order = range(nch) if d == 0 else range(nch - 1, -1, -1)
        entering = [None] * nch
        for c in order:
            entering[c] = st
            tot_row = c * HG_ST + (HG_ST - 1 if d == 0 else 0)
            st = st * jnp.exp2(ctot[tot_row:tot_row + 1, :]) + uts[c]
        outs = []
        for c in range(nch):
            lo = c * HG_ST
            oi = _nt_dot(qe_st[lo:lo + HG_ST], entering[c].astype(BF16))
            outs.append(o[lo:lo + HG_ST] + oi)
        return jnp.concatenate(outs, axis=0), st

    streams = [(hh, d) for hh in range(HG_HPB) for d in range(2)]

    def body(i, sts):
        new = []
        for (hh, d), st in zip(streams, sts):
            rb = i if d == 0 else nrb - 1 - i
            oblk, st = run_block(rb, st, d, hh)
            cs = slice(hh * HG_DK, (hh + 1) * HG_DK)
            if isinstance(rb, int):
                of_ref[d, rb * rb_rows:(rb + 1) * rb_rows, cs] = oblk
            else:
                of_ref[d, pl.ds(pl.multiple_of(rb * rb_rows, rb_rows), rb_rows), cs] = oblk
            new.append(st)
        return tuple(new)

    if has_h0:
        sts = tuple(h0_ref[0, d, hh].T for hh, d in streams)
    else:
        sts = tuple(jnp.zeros((HG_DK, HG_DK), F32) for _ in streams)
    sts = body(0, sts) if nrb == 1 else lax.fori_loop(0, nrb, body, sts)
    for (hh, d), st in zip(streams, sts):
        fin_ref[0, d, hh] = st.T

    for hh in range(HG_HPB):
        cs = slice(hh * HG_DK, (hh + 1) * HG_DK)
        o = of_ref[0, :, cs] + of_ref[1, :, cs]
        o = o * lax.rsqrt(jnp.mean(o * o, axis=-1, keepdims=True) + RMS_EPS)
        y_ref[:, cs] = (o * nw_ref[:, cs] * _silu(g_ref[:, cs])).astype(BF16)


def _hgrn_call(proj, lb, norm_w, h0, y_prev, fin_prev, *, L, nb, row0, n_rows, layer,
               cols=(C_QC, C_FF, C_FB, C_IC, C_GC)):
    w = lb.shape[-1]
    nh = w // HG_DK
    rblk0 = row0 // L
    hw = HG_HPB * HG_DK

    def col(off):
        return pl.BlockSpec((L, hw), lambda b, h: (rblk0 + b, off // hw + h))

    in_specs = [col(c) for c in cols] + [
                pl.BlockSpec((2, hw), lambda b, h: (0, h)),
                pl.BlockSpec((1, hw), lambda b, h: (0, h))]
    args = [proj] * 5 + [lb, norm_w.reshape(1, w)]
    if h0 is not None:
        in_specs.append(pl.BlockSpec((1, 2, HG_HPB, HG_DK, HG_DK), lambda b, h: (b, 0, h, 0, 0)))
        args.append(h0)
    aliases = {}
    for out_idx, prev in enumerate((y_prev, fin_prev)):
        if prev is not None:
            in_specs.append(pl.BlockSpec(memory_space=pl.ANY))
            args.append(prev)
            aliases[len(args) - 1] = out_idx
    st_spec = pl.BlockSpec((1, None, 2, HG_HPB, HG_DK, HG_DK), lambda b, h: (b, layer, 0, h, 0, 0))
    return pl.pallas_call(
        functools.partial(_hgrn_kernel, L=L, has_h0=h0 is not None, n_aliased=len(aliases)),
        out_shape=[jax.ShapeDtypeStruct((n_rows, w), BF16),
                   jax.ShapeDtypeStruct((nb, DEPTH, 2, nh, HG_DK, HG_DK), F32)],
        grid=(nb, nh // HG_HPB),
        in_specs=in_specs,
        out_specs=[pl.BlockSpec((L, hw), lambda b, h: (rblk0 + b, h)), st_spec],
        scratch_shapes=[pltpu.VMEM((2, L, hw), F32)],
        input_output_aliases=aliases,
        compiler_params=_cparams("arbitrary", "arbitrary"),
        name="hgrn2",
    )(*args)


def _merge_kernel(ya_ref, yb_ref, yc_ref, wa_ref, wb_ref, wc_ref, ma_ref, mb_ref, mc_ref, o_ref):
    acc = None
    for y_ref, w_ref, m_ref in ((ya_ref, wa_ref, ma_ref), (yb_ref, wb_ref, mb_ref), (yc_ref, wc_ref, mc_ref)):
        t = jnp.dot(y_ref[...], w_ref[...].astype(BF16), preferred_element_type=F32)
        t = _sigmoid(m_ref[...]) * t
        acc = t if acc is None else acc + t
    o_ref[...] = acc.astype(BF16)


def _merge_call(ya, yb, yc, w_br, proj, l):
    tm, tn = TM_MERGE, TN_MERGE
    moff = C_MA // tn
    mstep = D_MODEL // tn
    lhs = pl.BlockSpec((tm, W_BR), lambda i, j: (i, 0))

    def wspec(br):
        return pl.BlockSpec((None, None, W_BR, tn), lambda i, j: (l, br, 0, j))

    def mspec(br):
        return pl.BlockSpec((tm, tn), lambda i, j: (i, moff + br * mstep + j))

    return pl.pallas_call(
        _merge_kernel,
        out_shape=jax.ShapeDtypeStruct((N_TOK, D_MODEL), BF16),
        grid=(N_TOK // tm, D_MODEL // tn),
        in_specs=[lhs, lhs, lhs, wspec(0), wspec(1), wspec(2), mspec(0), mspec(1), mspec(2)],
        out_specs=pl.BlockSpec((tm, tn), lambda i, j: (i, j)),
        compiler_params=_cparams("arbitrary", "arbitrary"),
        name="branch_merge",
    )(ya, yb, yc, w_br, w_br, w_br, proj, proj, proj)


def _outproj_kernel(m_ref, w_ref, b_ref, xc_ref, xl_ref, gate_ref, o_ref):
    out = jnp.dot(m_ref[...], w_ref[...].astype(BF16), preferred_element_type=F32) + b_ref[...]
    gated = gate_ref[0] * out
    is_ctx = pl.program_id(0) < N_CTX // TM

    @pl.when(is_ctx)
    def _():
        o_ref[...] = DN_ALPHA * xc_ref[...] + gated

    @pl.when(jnp.logical_not(is_ctx))
    def _():
        o_ref[...] = DN_ALPHA * xl_ref[...] + gated


def _outproj_call(merged, w_out, b_out3, x_ctx, x_lat, mods3, l):
    goff = 2 * D_MODEL // TN
    ctx_map, lat_map = _split_rows(TM, 1)
    return pl.pallas_call(
        _outproj_kernel,
        out_shape=jax.ShapeDtypeStruct((N_TOK, D_MODEL), F32),
        grid=(N_TOK // TM, D_MODEL // TN),
        in_specs=[
            pl.BlockSpec((TM, D_MODEL), lambda i, j: (i, 0)),
            pl.BlockSpec((None, D_MODEL, TN), lambda i, j: (l, 0, j)),
            pl.BlockSpec((None, 1, TN), lambda i, j: (l, 0, j)),
            pl.BlockSpec((TM, TN), ctx_map),
            pl.BlockSpec((TM, TN), lat_map),
            pl.BlockSpec((1, 1, TN), lambda i, j: (_mod_row_mm(i), 0, goff + j)),
        ],
        out_specs=pl.BlockSpec((TM, TN), lambda i, j: (i, j)),
        compiler_params=_cparams("arbitrary", "arbitrary"),
        name="out_proj",
    )(merged, w_out, b_out3, x_ctx, x_lat, mods3)


def _ln_affine_kernel(r_ref, g_ref, b_ref, o_ref):
    r = r_ref[...]
    mu = jnp.mean(r, axis=-1, keepdims=True)
    rc = r - mu
    var = jnp.mean(rc * rc, axis=-1, keepdims=True)
    o_ref[...] = rc * lax.rsqrt(var + LN_EPS) * g_ref[...] + b_ref[...]


def _ln_affine_call(r, g, b, l, row0=0, n_rows=N_TOK):
    rblk0 = row0 // TM_LN
    return pl.pallas_call(
        _ln_affine_kernel,
        out_shape=jax.ShapeDtypeStruct((n_rows, D_MODEL), F32),
        grid=(n_rows // TM_LN,),
        in_specs=[
            pl.BlockSpec((TM_LN, D_MODEL), lambda i: (rblk0 + i, 0)),
            pl.BlockSpec((None, 1, D_MODEL), lambda i: (l, 0, 0)),
            pl.BlockSpec((None, 1, D_MODEL), lambda i: (l, 0, 0)),
        ],
        out_specs=pl.BlockSpec((TM_LN, D_MODEL), lambda i: (i, 0)),
        compiler_params=_cparams("arbitrary"),
        name="post_ln",
    )(r, g, b)


def kernel(x_prompt, x_sample, state_s5, state_lru, state_hgrn, c, c_ctx, w_ada, b_ada, w_in, b_in,
           s5_a_re, s5_a_im, s5_log_dt, s5_b_re, s5_b_im, s5_c_re, s5_c_im, s5_d, s5_w_glu, s5_b_glu,
           lru_conv_w, lru_conv_b, lru_w_a, lru_b_a, lru_w_x, lru_b_x, lru_lambda, hg_lb, hg_norm_w,
           w_br, w_out, b_out, ln_g, ln_b):
    lb_soft = jax.nn.softmax(hg_lb.astype(F32), axis=0)
    lb_all = jnp.cumsum(lb_soft, axis=0) - lb_soft[0]
    softplus_neg_lam = jax.nn.softplus(-lru_lambda.astype(F32))

    cvec = jnp.zeros((SUBLANE, D_MODEL), F32)
    cvec = cvec.at[0].set(c_ctx).at[1:1 + DEC_BATCH].set(c)
    mods = _mods_call(cvec, w_ada, b_ada)

    b_in3 = b_in.reshape(DEPTH, 1, IN_COLS)
    b_glu3 = s5_b_glu.reshape(DEPTH, 1, W_BR)
    b_out3 = b_out.reshape(DEPTH, 1, D_MODEL)
    ln_g3 = ln_g.reshape(DEPTH, 1, D_MODEL)
    ln_b3 = ln_b.reshape(DEPTH, 1, D_MODEL)

    x_ctx = x_prompt.reshape(N_CTX, D_MODEL)
    x_lat = x_sample.reshape(N_LAT, D_MODEL)
    st_s5, st_lru, st_hg = [], [], None
    for l in range(DEPTH):
        mods3 = mods[l, :1 + DEC_BATCH].reshape(1 + DEC_BATCH, 1, 3 * D_MODEL)
        h = _ln_mod_call(x_ctx, x_lat, mods3)
        proj = _inproj_call(h, w_in, b_in3, l)

        lam_re, lam_im, z_re, z_im = _s5_disc_call(s5_a_re[l], s5_a_im[l], s5_log_dt[l])
        bp, cp, lam = _s5_pack(lam_re, lam_im, z_re, z_im, s5_b_re[l], s5_b_im[l], s5_c_re[l], s5_c_im[l])
        ctx_rows = dict(L=SEQ, B=BATCH, nseg=1, rblk=0, n_rows=N_TOK)
        lat_rows = dict(L=SEQ, B=LAT_ROWS, nseg=LAT_SEG, rblk=N_CTX // N_LAT, n_rows=N_TOK)
        y_pre, fin_s5 = _s5_call(proj, C_UA, bp, cp, lam, s5_d[l], None, None, **ctx_rows)
        y_pre = _s5_call(proj, C_UA, bp, cp, lam, s5_d[l], _s5_pack_state(state_s5[:, l]), y_pre, **lat_rows)
        y_a = _glu_call(y_pre, s5_w_glu, b_glu3, proj, l)
        st_s5.append(_s5_unpack_state(fin_s5, BATCH))

        lru_args = (lru_conv_w[l], lru_conv_b[l], lru_w_a[l], lru_b_a[l], lru_w_x[l], lru_b_x[l],
                    softplus_neg_lam[l])
        y_b, fin_lru = _lru_call(proj, C_XB, C_GB, *lru_args, None, None, period=SEQ, **ctx_rows)
        y_b = _lru_call(proj, C_XB, C_GB, *lru_args, jnp.transpose(state_lru[:, l], (1, 0, 2)), y_b,
                        period=GRID_W, **lat_rows)
        st_lru.append(jnp.transpose(fin_lru, (1, 0, 2)))

        y_c, st_hg = _hgrn_call(proj, lb_all[l], hg_norm_w[l], None, None, st_hg,
                                L=SEQ, nb=BATCH, row0=0, n_rows=N_TOK, layer=l)
        y_c, _ = _hgrn_call(proj, lb_all[l], hg_norm_w[l], state_hgrn[:, l], y_c, None,
                            L=DEC_SEQ, nb=DEC_BATCH, row0=N_CTX, n_rows=N_TOK, layer=0)

        merged = _merge_call(y_a, y_b, y_c, w_br, proj, l)
        r = _outproj_call(merged, w_out, b_out3, x_ctx, x_lat, mods3, l)
        x_ctx = _ln_affine_call(r, ln_g3, ln_b3, l, 0, N_CTX)
        x_lat = _ln_affine_call(r, ln_g3, ln_b3, l, N_CTX, N_LAT)

    y_prompt = x_ctx.reshape(BATCH, SEQ, D_MODEL)
    y_sample = x_lat.reshape(DEC_BATCH, DEC_SEQ, D_MODEL)
    new_state_s5 = jnp.stack(st_s5, axis=1)
    new_state_lru = jnp.stack(st_lru, axis=1)
    new_state_hgrn = st_hg
    return (y_prompt, y_sample, new_state_s5, new_state_lru, new_state_hgrn)
```

```python
import functools
import math

import jax
import jax.numpy as jnp
from jax import lax
from jax.experimental import pallas as pl
from jax.experimental.pallas import tpu as pltpu

F32 = jnp.float32
BF16 = jnp.bfloat16

LANE = 128
SUBLANE = 8
VMEM_LIMIT = 56 * 1024 * 1024

D_MODEL = 4096
DEPTH = 2
BATCH, SEQ = 16, 256
DEC_BATCH, DEC_SEQ = 2, 1024
GRID_W = 64
W_BR = D_MODEL // 2
S5_GROUP = 16
S5_STATE = 64
LRU_C = 8.0
CONV_W = 4
HG_DK = 128
HG_CHUNK = 16
N_BRANCH = 3
IN_COLS = 9 * W_BR + N_BRANCH * D_MODEL
DN_ALPHA = (2 * DEPTH) ** 0.25
LN_EPS = 1e-5
RMS_EPS = 1e-6

N_CTX = BATCH * SEQ
N_LAT = DEC_BATCH * DEC_SEQ
N_TOK = N_CTX + N_LAT
LAT_SEG = 8
LAT_LEN = DEC_SEQ // LAT_SEG
LAT_ROWS = LAT_SEG * DEC_BATCH

C_UA, C_GA, C_XB, C_GB, C_QC, C_FF, C_FB, C_IC, C_GC = (i * W_BR for i in range(9))
C_MA = 9 * W_BR

TM = 1024
TN = 512
TM_IN = 2048
TM_MERGE = 1024
TN_MERGE = 256
TM_LN = 256
HG_RB = 256
HG_ST = 64
HG_HPB = 2
HG_SCB = 128
S5_PAIRS = 4


def _cparams(*sem):
    return pltpu.CompilerParams(dimension_semantics=sem, vmem_limit_bytes=VMEM_LIMIT)


def _sigmoid(x):
    return 0.5 * jnp.tanh(0.5 * x) + 0.5


def _silu(x):
    return x * _sigmoid(x)


def _mods_kernel(c_ref, w_ref, b_ref, o_ref):
    c = c_ref[...]
    s = _silu(c).astype(BF16)
    w = w_ref[...].astype(BF16)
    o_ref[...] = jnp.dot(s, w, preferred_element_type=F32) + b_ref[...]


def _mods_call(cvec, w_ada, b_ada):
    tn = TN
    return pl.pallas_call(
        _mods_kernel,
        out_shape=jax.ShapeDtypeStruct((DEPTH, SUBLANE, 3 * D_MODEL), F32),
        grid=(DEPTH, 3 * D_MODEL // tn),
        in_specs=[
            pl.BlockSpec((SUBLANE, D_MODEL), lambda l, j: (0, 0)),
            pl.BlockSpec((None, D_MODEL, tn), lambda l, j: (l, 0, j)),
            pl.BlockSpec((None, 1, tn), lambda l, j: (l, 0, j)),
        ],
        out_specs=pl.BlockSpec((None, SUBLANE, tn), lambda l, j: (l, 0, j)),
        compiler_params=_cparams("arbitrary", "arbitrary"),
        name="adaln_mods",
    )(cvec, w_ada, b_ada.reshape(DEPTH, 1, 3 * D_MODEL))


def _mod_row_ln(i):
    n_ctx_tiles = N_CTX // TM_LN
    return jnp.where(i < n_ctx_tiles, 0, 1 + (i - n_ctx_tiles) // (DEC_SEQ // TM_LN))


def _mod_row_mm(i):
    return jnp.maximum(i - (N_CTX // TM - 1), 0)


def _ln_mod_kernel(xc_ref, xl_ref, shift_ref, scale_ref, o_ref):
    def emit(x_ref):
        x = x_ref[...]
        mu = jnp.mean(x, axis=-1, keepdims=True)
        xc = x - mu
        var = jnp.mean(xc * xc, axis=-1, keepdims=True)
        h = xc * lax.rsqrt(var + LN_EPS) * (1.0 + scale_ref[0]) + shift_ref[0]
        o_ref[...] = h.astype(BF16)

    is_ctx = pl.program_id(0) < N_CTX // TM_LN
    pl.when(is_ctx)(lambda: emit(xc_ref))
    pl.when(jnp.logical_not(is_ctx))(lambda: emit(xl_ref))


def _split_rows(tile, n_col_axes):
    n_ctx_tiles = N_CTX // tile
    if n_col_axes == 0:
        return (lambda i: (jnp.minimum(i, n_ctx_tiles - 1), 0),
                lambda i: (jnp.maximum(i - n_ctx_tiles, 0), 0))
    return (lambda i, j: (jnp.minimum(i, n_ctx_tiles - 1), j),
            lambda i, j: (jnp.maximum(i - n_ctx_tiles, 0), j))


def _ln_mod_call(x_ctx, x_lat, mods3):
    ctx_map, lat_map = _split_rows(TM_LN, 0)
    return pl.pallas_call(
        _ln_mod_kernel,
        out_shape=jax.ShapeDtypeStruct((N_TOK, D_MODEL), BF16),
        grid=(N_TOK // TM_LN,),
        in_specs=[
            pl.BlockSpec((TM_LN, D_MODEL), ctx_map),
            pl.BlockSpec((TM_LN, D_MODEL), lat_map),
            pl.BlockSpec((1, 1, D_MODEL), lambda i: (_mod_row_ln(i), 0, 0)),
            pl.BlockSpec((1, 1, D_MODEL), lambda i: (_mod_row_ln(i), 0, 1)),
        ],
        out_specs=pl.BlockSpec((TM_LN, D_MODEL), lambda i: (i, 0)),
        compiler_params=_cparams("arbitrary"),
        name="ln_modulate",
    )(x_ctx, x_lat, mods3, mods3)


def _inproj_kernel(h_ref, w_ref, b_ref, o_ref):
    w = w_ref[...].astype(BF16)
    o_ref[...] = jnp.dot(h_ref[...], w, preferred_element_type=F32) + b_ref[...]


def _inproj_call(h, w_in, b_in3, l):
    return pl.pallas_call(
        _inproj_kernel,
        out_shape=jax.ShapeDtypeStruct((N_TOK, IN_COLS), F32),
        grid=(N_TOK // TM_IN, IN_COLS // TN),
        in_specs=[
            pl.BlockSpec((TM_IN, D_MODEL), lambda i, j: (i, 0), pipeline_mode=pl.Buffered(1)),
            pl.BlockSpec((None, D_MODEL, TN), lambda i, j: (l, 0, j)),
            pl.BlockSpec((None, 1, TN), lambda i, j: (l, 0, j)),
        ],
        out_specs=pl.BlockSpec((TM_IN, TN), lambda i, j: (i, j)),
        compiler_params=_cparams("arbitrary", "arbitrary"),
        name="in_proj",
    )(h, w_in, b_in3)


def _s5_disc_kernel(are_ref, aim_ref, ldt_ref, lre_ref, lim_ref, zre_ref, zim_ref):
    ar = jnp.minimum(are_ref[...], -1e-4)
    ai = aim_ref[...]
    dt = jnp.exp(ldt_ref[...])
    mag = jnp.exp(dt * ar)
    lam_re = mag * jnp.cos(dt * ai)
    lam_im = mag * jnp.sin(dt * ai)
    den = ar * ar + ai * ai
    lre_ref[...] = lam_re
    lim_ref[...] = lam_im
    zre_ref[...] = ((lam_re - 1.0) * ar + lam_im * ai) / den
    zim_ref[...] = (lam_im * ar - (lam_re - 1.0) * ai) / den


def _s5_disc_call(a_re, a_im, log_dt):
    g = a_re.shape[1]
    shp = (2 * g, S5_STATE)
    ldt = jnp.broadcast_to(log_dt[..., None], (2, g, S5_STATE)).reshape(shp)
    outs = pl.pallas_call(
        _s5_disc_kernel,
        out_shape=[jax.ShapeDtypeStruct(shp, F32)] * 4,
        name="s5_discretise",
    )(a_re.reshape(shp), a_im.reshape(shp), ldt)
    return [o.reshape(2, g, S5_STATE) for o in outs]


def _s5_pack(lam_re, lam_im, z_re, z_im, b_re, b_im, c_re, c_im):
    g = lam_re.shape[1]
    nblk = g // (2 * S5_PAIRS)
    pair_rows = 2 * S5_GROUP
    bz_re = z_re[..., None] * b_re[None] - z_im[..., None] * b_im[None]
    bz_im = z_re[..., None] * b_im[None] + z_im[..., None] * b_re[None]

    def pairs(t):
        t = t.reshape(t.shape[:-3] + (g // 2, 2) + t.shape[-2:])
        return t[..., 0, :, :], t[..., 1, :, :]

    def embed(t, axis):
        t = t.reshape(t.shape[:-3] + (nblk, S5_PAIRS) + t.shape[-2:])
        out = []
        for j in range(S5_PAIRS):
            pad = [(0, 0)] * (t.ndim - 1)
            pad[axis] = (j * pair_rows, LANE - (j + 1) * pair_rows)
            out.append(jnp.pad(t[..., j, :, :], pad))
        return jnp.stack(out, axis=-3)

    br0, br1 = pairs(jnp.swapaxes(bz_re, -1, -2))
    bi0, bi1 = pairs(jnp.swapaxes(bz_im, -1, -2))
    zb = jnp.zeros_like(br0)
    bp = jnp.concatenate([jnp.concatenate([br0, zb, bi0, zb], axis=-1),
                          jnp.concatenate([zb, br1, zb, bi1], axis=-1)], axis=-2)
    bp = embed(bp, -2).astype(BF16)

    cr0, cr1 = pairs(jnp.swapaxes(c_re, -1, -2))
    ci0, ci1 = pairs(jnp.swapaxes(c_im, -1, -2))
    zc = jnp.zeros_like(cr0)
    cp = jnp.concatenate([jnp.concatenate([cr0, zc], axis=-1), jnp.concatenate([zc, cr1], axis=-1),
                          jnp.concatenate([-ci0, zc], axis=-1), jnp.concatenate([zc, -ci1], axis=-1)],
                         axis=-2)
    cp = embed(cp, -1).astype(BF16)

    def pack_l(t):
        return t.reshape(2, nblk, S5_PAIRS, LANE)

    lam = jnp.stack([pack_l(lam_re), pack_l(lam_im)], axis=-2)
    return bp, cp, lam


def _s5_pack_state(h0):
    b, _, g, _, _ = h0.shape
    nblk = g // (2 * S5_PAIRS)
    t = h0.reshape(b, 2, nblk, S5_PAIRS, 2, S5_STATE, 2)
    t = jnp.transpose(t, (1, 2, 3, 0, 6, 4, 5))
    return t.reshape(2, nblk, S5_PAIRS, b, 2 * LANE)


def _s5_unpack_state(fin, b):
    nblk = fin.shape[1]
    t = fin.reshape(2, nblk, S5_PAIRS, b, 2, 2, S5_STATE)
    t = jnp.transpose(t, (3, 0, 1, 2, 5, 6, 4))
    return t.reshape(b, 2, nblk * S5_PAIRS * 2, S5_STATE, 2)


def _cmul(ar, ai, br, bi):
    return ar * br - ai * bi, ar * bi + ai * br


def _tm_pitch(L):
    return L + SUBLANE


def _gather_tm(src_ref, dst_ref, pad_ref, L, B):
    pitch = _tm_pitch(L)
    for r in range(B):
        pad_ref[r * pitch:r * pitch + L, :] = src_ref[r * L:(r + 1) * L, :]

    def body(t, c):
        dst_ref[t] = pad_ref[pl.ds(t, B, stride=pitch), :]
        return c
    lax.fori_loop(0, L, body, 0, unroll=8)


def _scatter_tm(src_ref, pad_ref, L, B):
    pitch = _tm_pitch(L)

    def body(t, c):
        pad_ref[pl.ds(t, B, stride=pitch), :] = src_ref[t]
        return c
    lax.fori_loop(0, L, body, 0, unroll=8)


def _s5_kernel(*refs, L, B, nseg):
    if nseg > 1:
        (u_ref, bp_ref, cp_ref, lam_ref, d_ref, h0_ref, _, y_ref,
         xs_ref, yacc_ref, utm_ref, pad_ref, fl_ref, hp_ref) = refs
    else:
        u_ref, bp_ref, cp_ref, lam_ref, d_ref, y_ref, fin_ref, xs_ref, yacc_ref, utm_ref, pad_ref = refs
    nb = B // nseg
    _gather_tm(u_ref, utm_ref, pad_ref, L, B)
    u2 = utm_ref[...].reshape(L * B, LANE)
    u2b = u2.astype(BF16)
    yacc_ref[...] = jnp.zeros((L * B, LANE), F32)

    for half in range(S5_PAIRS // 2):
        chains = [(d, jj) for d in range(2) for jj in range(2)]
        for d, jj in chains:
            x = jnp.dot(u2b, bp_ref[d, 0, 2 * half + jj], preferred_element_type=F32)
            xs_ref[d, jj] = x.reshape(L, B, 2 * LANE)
        lam = {}
        for d, jj in chains:
            lr = jnp.broadcast_to(lam_ref[d, 0, 2 * half + jj, 0:1, :], (B, LANE))
            li = jnp.broadcast_to(lam_ref[d, 0, 2 * half + jj, 1:2, :], (B, LANE))
            lam[(d, jj)] = (lr, li)

        def load_x(i):
            out = []
            for d, jj in chains:
                t = i if d == 0 else L - 1 - i
                out += [xs_ref[d, jj, t, :, 0:LANE], xs_ref[d, jj, t, :, LANE:2 * LANE]]
            return out

        nc = 2 * len(chains)

        def step(i, carry):
            nxt = load_x(jnp.minimum(i + 1, L - 1))
            out = []
            for n, (d, jj) in enumerate(chains):
                t = i if d == 0 else L - 1 - i
                hr, hi = carry[2 * n], carry[2 * n + 1]
                lr, li = lam[(d, jj)]
                pr, pi = _cmul(lr, li, hr, hi)
                nr = pr + carry[nc + 2 * n]
                ni = pi + carry[nc + 2 * n + 1]
                xs_ref[d, jj, t, :, 0:LANE] = nr
                xs_ref[d, jj, t, :, LANE:2 * LANE] = ni
                out += [nr, ni]
            return tuple(out + nxt)

        zero = jnp.zeros((B, LANE), F32)
        carry = lax.fori_loop(0, L, step, (zero,) * nc + tuple(load_x(0)), unroll=4)

        if nseg == 1:
            for n, (d, jj) in enumerate(chains):
                fin_ref[d, 0, 2 * half + jj, :, 0:LANE] = carry[2 * n]
                fin_ref[d, 0, 2 * half + jj, :, LANE:2 * LANE] = carry[2 * n + 1]
        else:
            hp = {}
            for n, (d, jj) in enumerate(chains):
                plr = lam_ref[d, 0, 2 * half + jj, 0:1, :]
                pli = lam_ref[d, 0, 2 * half + jj, 1:2, :]
                for _ in range(int(math.log2(L))):
                    plr, pli = _cmul(plr, pli, plr, pli)
                fl_ref[0] = carry[2 * n]
                fl_ref[1] = carry[2 * n + 1]
                order = list(range(nseg)) if d == 0 else list(range(nseg - 1, -1, -1))
                for bb in range(nb):
                    cr = h0_ref[d, 0, 2 * half + jj, bb:bb + 1, 0:LANE]
                    ci = h0_ref[d, 0, 2 * half + jj, bb:bb + 1, LANE:2 * LANE]
                    for k, s in enumerate(order):
                        if k > 0:
                            rp = bb * nseg + order[k - 1]
                            mr, mi = _cmul(plr, pli, cr, ci)
                            cr = fl_ref[0, rp:rp + 1, :] + mr
                            ci = fl_ref[1, rp:rp + 1, :] + mi
                        r = bb * nseg + s
                        hp_ref[n, 0, r:r + 1, :] = cr
                        hp_ref[n, 1, r:r + 1, :] = ci
                hp[(d, jj)] = (hp_ref[n, 0], hp_ref[n, 1])

            def cstep(i, cw):
                nxt = load_x(jnp.minimum(i + 1, L - 1))
                out = []
                for n, (d, jj) in enumerate(chains):
                    t = i if d == 0 else L - 1 - i
                    lr, li = lam[(d, jj)]
                    nr, ni = _cmul(lr, li, cw[2 * n], cw[2 * n + 1])
                    xs_ref[d, jj, t, :, 0:LANE] = cw[nc + 2 * n] + nr
                    xs_ref[d, jj, t, :, LANE:2 * LANE] = cw[nc + 2 * n + 1] + ni
                    out += [nr, ni]
                return tuple(out + nxt)

            cw0 = []
            for d, jj in chains:
                cw0 += list(hp[(d, jj)])
            lax.fori_loop(0, L, cstep, tuple(cw0 + load_x(0)), unroll=4)

        for jj in range(2):
            hs = (xs_ref[0, jj] + xs_ref[1, jj]).reshape(L * B, 2 * LANE).astype(BF16)
            yacc_ref[...] += jnp.dot(hs, cp_ref[0, 2 * half + jj], preferred_element_type=F32)

    y = u2 * d_ref[...] + yacc_ref[...]
    utm_ref[...] = jax.nn.gelu(y).reshape(L, B, LANE)
    _scatter_tm(utm_ref, pad_ref, L, B)
    pitch = _tm_pitch(L)
    for r in range(B):
        y_ref[r * L:(r + 1) * L, :] = pad_ref[r * pitch:r * pitch + L, :]


def _s5_call(src, col0, bp, cp, lam, d_skip, h0, y_prev, *, L, B, nseg, rblk, n_rows):
    w = d_skip.shape[-1]
    nblk = w // LANE
    cblk0 = col0 // LANE
    in_specs = [
        pl.BlockSpec((L * B, LANE), lambda k: (rblk, cblk0 + k)),
        pl.BlockSpec((2, 1, S5_PAIRS, LANE, 2 * LANE), lambda k: (0, k, 0, 0, 0)),
        pl.BlockSpec((1, S5_PAIRS, 2 * LANE, LANE), lambda k: (k, 0, 0, 0)),
        pl.BlockSpec((2, 1, S5_PAIRS, 2, LANE), lambda k: (0, k, 0, 0, 0)),
        pl.BlockSpec((1, LANE), lambda k: (0, k)),
    ]
    args = [src, bp, cp, lam, d_skip.reshape(1, w)]
    y_shape = jax.ShapeDtypeStruct((n_rows, w), F32)
    y_spec = pl.BlockSpec((L * B, LANE), lambda k: (rblk, k))
    scratch = [pltpu.VMEM((2, 2, L, B, 2 * LANE), F32), pltpu.VMEM((L * B, LANE), F32),
               pltpu.VMEM((L, B, LANE), F32), pltpu.VMEM((B * _tm_pitch(L), LANE), F32)]
    aliases = {}
    if nseg > 1:
        nb = B // nseg
        in_specs += [pl.BlockSpec((2, 1, S5_PAIRS, nb, 2 * LANE), lambda k: (0, k, 0, 0, 0)),
                     pl.BlockSpec(memory_space=pl.ANY)]
        args += [h0, y_prev]
        aliases = {len(args) - 1: 0}
        out_shape, out_specs = y_shape, y_spec
        scratch += [pltpu.VMEM((2, B, LANE), F32), pltpu.VMEM((4, 2, B, LANE), F32)]
    else:
        out_shape = [y_shape, jax.ShapeDtypeStruct((2, nblk, S5_PAIRS, B, 2 * LANE), F32)]
        out_specs = [y_spec, pl.BlockSpec((2, 1, S5_PAIRS, B, 2 * LANE), lambda k: (0, k, 0, 0, 0))]
    return pl.pallas_call(
        functools.partial(_s5_kernel, L=L, B=B, nseg=nseg),
        out_shape=out_shape,
        grid=(nblk,),
        in_specs=in_specs,
        out_specs=out_specs,
        scratch_shapes=scratch,
        input_output_aliases=aliases,
        compiler_params=_cparams("arbitrary"),
        name="s5_scan_seg" if nseg > 1 else "s5_scan",
    )(*args)


def _glu_kernel(yrow_ref, ytile_ref, w_ref, b_ref, g_ref, o_ref):
    w = w_ref[...].astype(BF16)
    z = jnp.dot(yrow_ref[...].astype(BF16), w, preferred_element_type=F32) + b_ref[...]
    o_ref[...] = (ytile_ref[...] * _sigmoid(z) * _silu(g_ref[...])).astype(BF16)


def _glu_call(y, w_glu, b_glu3, proj, l):
    goff = C_GA // TN
    return pl.pallas_call(
        _glu_kernel,
        out_shape=jax.ShapeDtypeStruct((N_TOK, W_BR), BF16),
        grid=(N_TOK // TM, W_BR // TN),
        in_specs=[
            pl.BlockSpec((TM, W_BR), lambda i, j: (i, 0)),
            pl.BlockSpec((TM, TN), lambda i, j: (i, j)),
            pl.BlockSpec((None, W_BR, TN), lambda i, j: (l, 0, j)),
            pl.BlockSpec((None, 1, TN), lambda i, j: (l, 0, j)),
            pl.BlockSpec((TM, TN), lambda i, j: (i, goff + j)),
        ],
        out_specs=pl.BlockSpec((TM, TN), lambda i, j: (i, j)),
        compiler_params=_cparams("arbitrary", "arbitrary"),
        name="s5_glu",
    )(y, y, w_glu, b_glu3, proj)


def _expm1(x):
    t = jnp.tanh(0.5 * x)
    return 2.0 * t / (1.0 - t)


def _lru_kernel(*refs, L, B, nseg, period):
    if nseg > 1:
        (x_ref, g_ref, cw_ref, cb_ref, wa_ref, ba_ref, wx_ref, bx_ref, sp_ref, h0_ref, _,
         y_ref, a_ref, b_ref, ytok_ref, fl_ref, hp_ref) = refs
    else:
        (x_ref, g_ref, cw_ref, cb_ref, wa_ref, ba_ref, wx_ref, bx_ref, sp_ref,
         y_ref, fin_ref, a_ref, b_ref, ytok_ref) = refs
    nb = B // nseg
    _gather_tm(x_ref, a_ref.at[0], ytok_ref, L, B)
    x = a_ref[0]
    pos = lax.broadcasted_iota(jnp.int32, (L, B, LANE), 0) % period
    xc = jnp.broadcast_to(cb_ref[...].reshape(1, 1, LANE), (L, B, LANE))
    for k in range(CONV_W):
        off = k - CONV_W // 2
        if off < 0:
            xs = jnp.concatenate([jnp.zeros((-off, B, LANE), F32), x[:L + off]], axis=0)
        elif off > 0:
            xs = jnp.concatenate([x[off:], jnp.zeros((off, B, LANE), F32)], axis=0)
        else:
            xs = x
        if period < L:
            xs = jnp.where((pos + off >= 0) & (pos + off < period), xs, 0.0)
        xc = xc + cw_ref[k:k + 1, :].reshape(1, 1, LANE) * xs
    xc2 = xc.reshape(L * B, LANE)
    xcb = xc2.astype(BF16)
    for d in range(2):
        r = _sigmoid(jnp.dot(xcb, wa_ref[d, 0].astype(BF16), preferred_element_type=F32)
                     + ba_ref[d:d + 1, :])
        ig = _sigmoid(jnp.dot(xcb, wx_ref[d, 0].astype(BF16), preferred_element_type=F32)
                      + bx_ref[d:d + 1, :])
        log_a = (-LRU_C) * r * sp_ref[d:d + 1, :]
        a_ref[d] = jnp.exp(log_a).reshape(L, B, LANE)
        mult = jnp.sqrt(-_expm1(2.0 * log_a))
        b_ref[d] = (mult * (ig * xc2)).reshape(L, B, LANE)

    def load_ab(i):
        out = []
        for d in range(2):
            t = i if d == 0 else L - 1 - i
            out += [a_ref[d, t], b_ref[d, t]]
        return out

    def step(i, carry):
        nxt = load_ab(jnp.minimum(i + 1, L - 1))
        out = []
        for d in range(2):
            t = i if d == 0 else L - 1 - i
            a = carry[4 + 2 * d]
            h = a * carry[2 * d] + carry[4 + 2 * d + 1]
            b_ref[d, t] = h
            if nseg > 1:
                p = a * carry[2 * d + 1]
                a_ref[d, t] = p
            else:
                p = carry[2 * d + 1]
            out += [h, p]
        return tuple(out + nxt)

    zero = jnp.zeros((B, LANE), F32)
    one = jnp.ones((B, LANE), F32)
    carry = lax.fori_loop(0, L, step, (zero, one, zero, one) + tuple(load_ab(0)), unroll=8)

    if nseg == 1:
        fin_ref[0] = carry[0]
        fin_ref[1] = carry[2]
    else:
        hps = []
        for d in range(2):
            fl_ref[0] = carry[2 * d]
            fl_ref[1] = carry[2 * d + 1]
            order = list(range(nseg)) if d == 0 else list(range(nseg - 1, -1, -1))
            for bb in range(nb):
                c = h0_ref[d, bb:bb + 1, :]
                for k, s in enumerate(order):
                    if k > 0:
                        rp = bb * nseg + order[k - 1]
                        c = fl_ref[0, rp:rp + 1, :] + fl_ref[1, rp:rp + 1, :] * c
                    r = bb * nseg + s
                    hp_ref[d, r:r + 1, :] = c
            hps.append(hp_ref[d])

        def cstep(i, c):
            nxt = load_ab(jnp.minimum(i + 1, L - 1))
            for d in range(2):
                t = i if d == 0 else L - 1 - i
                b_ref[d, t] = c[2 * d + 1] + c[2 * d] * hps[d]
            return tuple(nxt)

        lax.fori_loop(0, L, cstep, tuple(load_ab(0)), unroll=4)

    a_ref[0] = b_ref[0] + b_ref[1]
    _scatter_tm(a_ref.at[0], ytok_ref, L, B)
    pitch = _tm_pitch(L)
    for r in range(B):
        rows = slice(r * L, (r + 1) * L)
        y_ref[rows, :] = (ytok_ref[r * pitch:r * pitch + L, :] * _silu(g_ref[rows, :])).astype(BF16)


def _lru_call(src, col_x, col_g, conv_w, conv_b, w_a, b_a, w_x, b_x, sp, h0, y_prev,
              *, L, B, nseg, period, rblk, n_rows):
    w = conv_b.shape[-1]
    nblk = w // LANE

    def rows(col0):
        return pl.BlockSpec((L * B, LANE), lambda k: (rblk, col0 // LANE + k))

    y_spec = pl.BlockSpec((L * B, LANE), lambda k: (rblk, k))
    in_specs = [
        rows(col_x), rows(col_g),
        pl.BlockSpec((CONV_W, LANE), lambda k: (0, k)),
        pl.BlockSpec((1, LANE), lambda k: (0, k)),
        pl.BlockSpec((2, 1, LANE, LANE), lambda k: (0, k, 0, 0)),
        pl.BlockSpec((2, LANE), lambda k: (0, k)),
        pl.BlockSpec((2, 1, LANE, LANE), lambda k: (0, k, 0, 0)),
        pl.BlockSpec((2, LANE), lambda k: (0, k)),
        pl.BlockSpec((2, LANE), lambda k: (0, k)),
    ]
    args = [src, src, conv_w, conv_b.reshape(1, w), w_a, b_a, w_x, b_x, sp]
    y_shape = jax.ShapeDtypeStruct((n_rows, w), BF16)
    scratch = [pltpu.VMEM((2, L, B, LANE), F32), pltpu.VMEM((2, L, B, LANE), F32),
               pltpu.VMEM((B * _tm_pitch(L), LANE), F32)]
    aliases = {}
    if nseg > 1:
        nb = B // nseg
        in_specs += [pl.BlockSpec((2, nb, LANE), lambda k: (0, 0, k)), pl.BlockSpec(memory_space=pl.ANY)]
        args += [h0, y_prev]
        aliases = {len(args) - 1: 0}
        out_shape, out_specs = y_shape, y_spec
        scratch += [pltpu.VMEM((2, B, LANE), F32), pltpu.VMEM((2, B, LANE), F32)]
    else:
        out_shape = [y_shape, jax.ShapeDtypeStruct((2, B, w), F32)]
        out_specs = [y_spec, pl.BlockSpec((2, B, LANE), lambda k: (0, 0, k))]
    return pl.pallas_call(
        functools.partial(_lru_kernel, L=L, B=B, nseg=nseg, period=period),
        out_shape=out_shape,
        grid=(nblk,),
        in_specs=in_specs,
        out_specs=out_specs,
        scratch_shapes=scratch,
        input_output_aliases=aliases,
        compiler_params=_cparams("arbitrary"),
        name="rglru_seg" if nseg > 1 else "rglru",
    )(*args)


def _nt_dot(a, b):
    return lax.dot_general(a, b, (((1,), (1,)), ((), ())), preferred_element_type=F32)


def _hgrn_kernel(*refs, L, has_h0, n_aliased):
    q_ref, ff_ref, fb_ref, v_ref, g_ref, lb_ref, nw_ref = refs[:7]
    h0_ref = refs[7] if has_h0 else None
    y_ref, fin_ref, of_ref = refs[7 + has_h0 + n_aliased:]
    rb_rows = HG_RB
    nrb = L // rb_rows
    nsub = HG_ST // HG_CHUNK
    nch = rb_rows // HG_ST
    nscb = rb_rows // HG_SCB
    row = lax.broadcasted_iota(jnp.int32, (HG_SCB, HG_SCB), 0)
    col = lax.broadcasted_iota(jnp.int32, (HG_SCB, HG_SCB), 1)
    sub_dist = jnp.where((row // HG_ST) == (col // HG_ST), row // HG_CHUNK - col // HG_CHUNK, 2 * nsub)
    ridx = lax.broadcasted_iota(jnp.int32, (rb_rows, HG_DK), 0)
    pos = ridx % HG_CHUNK
    sic = (ridx // HG_CHUNK) % nsub
    chunk_id = ridx // HG_ST

    def sub_bcast(x, idx):
        x3 = x.reshape(rb_rows // HG_CHUNK, HG_CHUNK, HG_DK)
        return jnp.broadcast_to(x3[:, idx:idx + 1, :], x3.shape).reshape(rb_rows, HG_DK)

    def chunk_bcast(x, idx):
        x3 = x.reshape(nch, HG_ST, HG_DK)
        return jnp.broadcast_to(x3[:, idx:idx + 1, :], x3.shape).reshape(rb_rows, HG_DK)

    shared = {}

    def load_qv(rows, cs, key):
        if key is not None and key in shared:
            return shared[key]
        v = v_ref[rows, cs]
        out = (_silu(q_ref[rows, cs]), v.astype(BF16), v.T.astype(BF16))
        if key is not None:
            shared[key] = out
        return out

    def run_block(rb, st, d, hh):
        cs = slice(hh * HG_DK, (hh + 1) * HG_DK)
        if isinstance(rb, int):
            rows = slice(rb * rb_rows, (rb + 1) * rb_rows)
        else:
            rows = pl.ds(pl.multiple_of(rb * rb_rows, rb_rows), rb_rows)
        qf, vb, vt = load_qv(rows, cs, (hh, rb) if isinstance(rb, int) else None)
        f = (ff_ref if d == 0 else fb_ref)[rows, cs]
        lbd = lb_ref[d:d + 1, cs]
        g = lbd + (1.0 - lbd) * jax.nn.sigmoid(f)
        kk = 1.0 - g
        b = jnp.log2(g)
        sgn = 1 if d == 0 else -1

        def shift(x, n):
            return pltpu.roll(x, n if d == 0 else rb_rows - n, 0)

        s = 1
        while s < HG_CHUNK:
            keep = (pos >= s) if d == 0 else (pos <= HG_CHUNK - 1 - s)
            b = b + jnp.where(keep, shift(b, s), 0.0)
            s *= 2
        tot = sub_bcast(b, HG_CHUNK - 1 if d == 0 else 0)
        prev = [shift(tot, HG_CHUNK * n) for n in range(1, nsub)]
        seen = sic if d == 0 else nsub - 1 - sic
        base = sum(jnp.where(seen >= n, prev[n - 1], 0.0) for n in range(1, nsub))
        ctot = chunk_bcast(b + base, HG_ST - 1 if d == 0 else 0)

        ed = jnp.exp2(b - sub_bcast(b, HG_CHUNK // 2))
        qt = (qf * ed).astype(BF16)
        kt = (kk / ed).astype(BF16)
        qe = qf * jnp.exp2(b)
        kd = kk * jnp.exp2(tot - b)
        lhs = [qe.astype(BF16)]
        acc = None
        for n in range(1, nsub - 1):
            acc = prev[n - 1] if acc is None else acc + prev[n - 1]
            lhs.append((qe * jnp.exp2(acc)).astype(BF16))
        kdb = kd.astype(BF16)
        causal = (col <= row) if d == 0 else (col >= row)
        o_parts = []
        for hb in range(nscb):
            rs = slice(hb * HG_SCB, (hb + 1) * HG_SCB)
            sc = jnp.where((sub_dist == 0) & causal, _nt_dot(qt[rs], kt[rs]), 0.0)
            scn = _nt_dot(jnp.concatenate([x[rs] for x in lhs], axis=0), kdb[rs])
            for n in range(1, nsub):
                sc = jnp.where(sub_dist == sgn * n, scn[(n - 1) * HG_SCB:n * HG_SCB], sc)
            o_parts.append(jnp.dot(sc.astype(BF16), vb[rs], preferred_element_type=F32))
        o = jnp.concatenate(o_parts, axis=0)

        qe_st = (qe * jnp.exp2(base)).astype(BF16)
        kd_st = kd * jnp.exp2(ctot - base - tot)
        uts = []
        for c in range(0, nch, 2):
            rhs = jnp.concatenate([jnp.where(chunk_id == c, kd_st, 0.0),
                                   jnp.where(chunk_id == c + 1, kd_st, 0.0)], axis=1).astype(BF16)
            ut2 = jnp.dot(vt, rhs, preferred_element_type=F32)
            uts += [ut2[:, :HG_DK], ut2[:, HG_DK:]]
        order = range(nch) if d == 0 else range(nch - 1, -1, -1)
        entering = [None] * nch
        for c in order:
            entering[c] = st
            tot_row = c * HG_ST + (HG_ST - 1 if d == 0 else 0)
            st = st * jnp.exp2(ctot[tot_row:tot_row + 1, :]) + uts[c]
        outs = []
        for c in range(nch):
            lo = c * HG_ST
            oi = _nt_dot(qe_st[lo:lo + HG_ST], entering[c].astype(BF16))
            outs.append(o[lo:lo + HG_ST] + oi)
        return jnp.concatenate(outs, axis=0), st

    streams = [(hh, d) for hh in range(HG_HPB) for d in range(2)]

    def body(i, sts):
        new = []
        for (hh, d), st in zip(streams, sts):
            rb = i if d == 0 else nrb - 1 - i
            oblk, st = run_block(rb, st, d, hh)
            cs = slice(hh * HG_DK, (hh + 1) * HG_DK)
            if isinstance(rb, int):
                of_ref[d, rb * rb_rows:(rb + 1) * rb_rows, cs] = oblk
            else:
                of_ref[d, pl.ds(pl.multiple_of(rb * rb_rows, rb_rows), rb_rows), cs] = oblk
            new.append(st)
        return tuple(new)

    if has_h0:
        sts = tuple(h0_ref[0, d, hh].T for hh, d in streams)
    else:
        sts = tuple(jnp.zeros((HG_DK, HG_DK), F32) for _ in streams)
    sts = body(0, sts) if nrb == 1 else lax.fori_loop(0, nrb, body, sts)
    for (hh, d), st in zip(streams, sts):
        fin_ref[0, d, hh] = st.T

    for hh in range(HG_HPB):
        cs = slice(hh * HG_DK, (hh + 1) * HG_DK)
        o = of_ref[0, :, cs] + of_ref[1, :, cs]
        o = o * lax.rsqrt(jnp.mean(o * o, axis=-1, keepdims=True) + RMS_EPS)
        y_ref[:, cs] = (o * nw_ref[:, cs] * _silu(g_ref[:, cs])).astype(BF16)


def _hgrn_call(proj, lb, norm_w, h0, y_prev, fin_prev, *, L, nb, row0, n_rows, layer,
               cols=(C_QC, C_FF, C_FB, C_IC, C_GC)):
    w = lb.shape[-1]
    nh = w // HG_DK
    rblk0 = row0 // L
    hw = HG_HPB * HG_DK

    def col(off):
        return pl.BlockSpec((L, hw), lambda b, h: (rblk0 + b, off // hw + h))

    in_specs = [col(c) for c in cols] + [
                pl.BlockSpec((2, hw), lambda b, h: (0, h)),
                pl.BlockSpec((1, hw), lambda b, h: (0, h))]
    args = [proj] * 5 + [lb, norm_w.reshape(1, w)]
    if h0 is not None:
        in_specs.append(pl.BlockSpec((1, 2, HG_HPB, HG_DK, HG_DK), lambda b, h: (b, 0, h, 0, 0)))
        args.append(h0)
    aliases = {}
    for out_idx, prev in enumerate((y_prev, fin_prev)):
        if prev is not None:
            in_specs.append(pl.BlockSpec(memory_space=pl.ANY))
            args.append(prev)
            aliases[len(args) - 1] = out_idx
    st_spec = pl.BlockSpec((1, None, 2, HG_HPB, HG_DK, HG_DK), lambda b, h: (b, layer, 0, h, 0, 0))
    return pl.pallas_call(
        functools.partial(_hgrn_kernel, L=L, has_h0=h0 is not None, n_aliased=len(aliases)),
        out_shape=[jax.ShapeDtypeStruct((n_rows, w), BF16),
                   jax.ShapeDtypeStruct((nb, DEPTH, 2, nh, HG_DK, HG_DK), F32)],
        grid=(nb, nh // HG_HPB),
        in_specs=in_specs,
        out_specs=[pl.BlockSpec((L, hw), lambda b, h: (rblk0 + b, h)), st_spec],
        scratch_shapes=[pltpu.VMEM((2, L, hw), F32)],
        input_output_aliases=aliases,
        compiler_params=_cparams("arbitrary", "arbitrary"),
        name="hgrn2",
    )(*args)


def _merge_kernel(ya_ref, yb_ref, yc_ref, wa_ref, wb_ref, wc_ref, ma_ref, mb_ref, mc_ref, o_ref):
    acc = None
    for y_ref, w_ref, m_ref in ((ya_ref, wa_ref, ma_ref), (yb_ref, wb_ref, mb_ref), (yc_ref, wc_ref, mc_ref)):
        t = jnp.dot(y_ref[...], w_ref[...].astype(BF16), preferred_element_type=F32)
        t = _sigmoid(m_ref[...]) * t
        acc = t if acc is None else acc + t
    o_ref[...] = acc.astype(BF16)


def _merge_call(ya, yb, yc, w_br, proj, l):
    tm, tn = TM_MERGE, TN_MERGE
    moff = C_MA // tn
    mstep = D_MODEL // tn
    lhs = pl.BlockSpec((tm, W_BR), lambda i, j: (i, 0))

    def wspec(br):
        return pl.BlockSpec((None, None, W_BR, tn), lambda i, j: (l, br, 0, j))

    def mspec(br):
        return pl.BlockSpec((tm, tn), lambda i, j: (i, moff + br * mstep + j))

    return pl.pallas_call(
        _merge_kernel,
        out_shape=jax.ShapeDtypeStruct((N_TOK, D_MODEL), BF16),
        grid=(N_TOK // tm, D_MODEL // tn),
        in_specs=[lhs, lhs, lhs, wspec(0), wspec(1), wspec(2), mspec(0), mspec(1), mspec(2)],
        out_specs=pl.BlockSpec((tm, tn), lambda i, j: (i, j)),
        compiler_params=_cparams("arbitrary", "arbitrary"),
        name="branch_merge",
    )(ya, yb, yc, w_br, w_br, w_br, proj, proj, proj)


def _outproj_kernel(m_ref, w_ref, b_ref, xc_ref, xl_ref, gate_ref, o_ref):
    out = jnp.dot(m_ref[...], w_ref[...].astype(BF16), preferred_element_type=F32) + b_ref[...]
    gated = gate_ref[0] * out
    is_ctx = pl.program_id(0) < N_CTX // TM

    @pl.when(is_ctx)
    def _():
        o_ref[...] = DN_ALPHA * xc_ref[...] + gated

    @pl.when(jnp.logical_not(is_ctx))
    def _():
        o_ref[...] = DN_ALPHA * xl_ref[...] + gated


def _outproj_call(merged, w_out, b_out3, x_ctx, x_lat, mods3, l):
    goff = 2 * D_MODEL // TN
    ctx_map, lat_map = _split_rows(TM, 1)
    return pl.pallas_call(
        _outproj_kernel,
        out_shape=jax.ShapeDtypeStruct((N_TOK, D_MODEL), F32),
        grid=(N_TOK // TM, D_MODEL // TN),
        in_specs=[
            pl.BlockSpec((TM, D_MODEL), lambda i, j: (i, 0)),
            pl.BlockSpec((None, D_MODEL, TN), lambda i, j: (l, 0, j)),
            pl.BlockSpec((None, 1, TN), lambda i, j: (l, 0, j)),
            pl.BlockSpec((TM, TN), ctx_map),
            pl.BlockSpec((TM, TN), lat_map),
            pl.BlockSpec((1, 1, TN), lambda i, j: (_mod_row_mm(i), 0, goff + j)),
        ],
        out_specs=pl.BlockSpec((TM, TN), lambda i, j: (i, j)),
        compiler_params=_cparams("arbitrary", "arbitrary"),
        name="out_proj",
    )(merged, w_out, b_out3, x_ctx, x_lat, mods3)


def _ln_affine_kernel(r_ref, g_ref, b_ref, o_ref):
    r = r_ref[...]
    mu = jnp.mean(r, axis=-1, keepdims=True)
    rc = r - mu
    var = jnp.mean(rc * rc, axis=-1, keepdims=True)
    o_ref[...] = rc * lax.rsqrt(var + LN_EPS) * g_ref[...] + b_ref[...]


def _ln_affine_call(r, g, b, l, row0=0, n_rows=N_TOK):
    rblk0 = row0 // TM_LN
    return pl.pallas_call(
        _ln_affine_kernel,
        out_shape=jax.ShapeDtypeStruct((n_rows, D_MODEL), F32),
        grid=(n_rows // TM_LN,),
        in_specs=[
            pl.BlockSpec((TM_LN, D_MODEL), lambda i: (rblk0 + i, 0)),
            pl.BlockSpec((None, 1, D_MODEL), lambda i: (l, 0, 0)),
            pl.BlockSpec((None, 1, D_MODEL), lambda i: (l, 0, 0)),
        ],
        out_specs=pl.BlockSpec((TM_LN, D_MODEL), lambda i: (i, 0)),
        compiler_params=_cparams("arbitrary"),
        name="post_ln",
    )(r, g, b)


def kernel(x_prompt, x_sample, state_s5, state_lru, state_hgrn, c, c_ctx, w_ada, b_ada, w_in, b_in,
           s5_a_re, s5_a_im, s5_log_dt, s5_b_re, s5_b_im, s5_c_re, s5_c_im, s5_d, s5_w_glu, s5_b_glu,
           lru_conv_w, lru_conv_b, lru_w_a, lru_b_a, lru_w_x, lru_b_x, lru_lambda, hg_lb, hg_norm_w,
           w_br, w_out, b_out, ln_g, ln_b):
    lb_soft = jax.nn.softmax(hg_lb.astype(F32), axis=0)
    lb_all = jnp.cumsum(lb_soft, axis=0) - lb_soft[0]
    softplus_neg_lam = jax.nn.softplus(-lru_lambda.astype(F32))

    cvec = jnp.zeros((SUBLANE, D_MODEL), F32)
    cvec = cvec.at[0].set(c_ctx).at[1:1 + DEC_BATCH].set(c)
    mods = _mods_call(cvec, w_ada, b_ada)

    b_in3 = b_in.reshape(DEPTH, 1, IN_COLS)
    b_glu3 = s5_b_glu.reshape(DEPTH, 1, W_BR)
    b_out3 = b_out.reshape(DEPTH, 1, D_MODEL)
    ln_g3 = ln_g.reshape(DEPTH, 1, D_MODEL)
    ln_b3 = ln_b.reshape(DEPTH, 1, D_MODEL)

    x_ctx = x_prompt.reshape(N_CTX, D_MODEL)
    x_lat = x_sample.reshape(N_LAT, D_MODEL)
    st_s5, st_lru, st_hg = [], [], None
    for l in range(DEPTH):
        mods3 = mods[l, :1 + DEC_BATCH].reshape(1 + DEC_BATCH, 1, 3 * D_MODEL)
        h = _ln_mod_call(x_ctx, x_lat, mods3)
        proj = _inproj_call(h, w_in, b_in3, l)

        lam_re, lam_im, z_re, z_im = _s5_disc_call(s5_a_re[l], s5_a_im[l], s5_log_dt[l])
        bp, cp, lam = _s5_pack(lam_re, lam_im, z_re, z_im, s5_b_re[l], s5_b_im[l], s5_c_re[l], s5_c_im[l])
        ctx_rows = dict(L=SEQ, B=BATCH, nseg=1, rblk=0, n_rows=N_TOK)
        lat_rows = dict(L=LAT_LEN, B=LAT_ROWS, nseg=LAT_SEG, rblk=N_CTX // N_LAT, n_rows=N_TOK)
        y_pre, fin_s5 = _s5_call(proj, C_UA, bp, cp, lam, s5_d[l], None, None, **ctx_rows)
        y_pre = _s5_call(proj, C_UA, bp, cp, lam, s5_d[l], _s5_pack_state(state_s5[:, l]), y_pre, **lat_rows)
        y_a = _glu_call(y_pre, s5_w_glu, b_glu3, proj, l)
        st_s5.append(_s5_unpack_state(fin_s5, BATCH))

        lru_args = (lru_conv_w[l], lru_conv_b[l], lru_w_a[l], lru_b_a[l], lru_w_x[l], lru_b_x[l],
                    softplus_neg_lam[l])
        y_b, fin_lru = _lru_call(proj, C_XB, C_GB, *lru_args, None, None, period=SEQ, **ctx_rows)
        y_b = _lru_call(proj, C_XB, C_GB, *lru_args, jnp.transpose(state_lru[:, l], (1, 0, 2)), y_b,
                        period=GRID_W, **lat_rows)
        st_lru.append(jnp.transpose(fin_lru, (1, 0, 2)))

        y_c, st_hg = _hgrn_call(proj, lb_all[l], hg_norm_w[l], None, None, st_hg,
                                L=SEQ, nb=BATCH, row0=0, n_rows=N_TOK, layer=l)
        y_c, _ = _hgrn_call(proj, lb_all[l], hg_norm_w[l], state_hgrn[:, l], y_c, None,
                            L=DEC_SEQ, nb=DEC_BATCH, row0=N_CTX, n_rows=N_TOK, layer=0)

        merged = _merge_call(y_a, y_b, y_c, w_br, proj, l)
        r = _outproj_call(merged, w_out, b_out3, x_ctx, x_lat, mods3, l)
        x_ctx = _ln_affine_call(r, ln_g3, ln_b3, l, 0, N_CTX)
        x_lat = _ln_affine_call(r, ln_g3, ln_b3, l, N_CTX, N_LAT)

    y_prompt = x_ctx.reshape(BATCH, SEQ, D_MODEL)
    y_sample = x_lat.reshape(DEC_BATCH, DEC_SEQ, D_MODEL)
    new_state_s5 = jnp.stack(st_s5, axis=1)
    new_state_lru = jnp.stack(st_lru, axis=1)
    new_state_hgrn = st_hg
    return (y_prompt, y_sample, new_state_s5, new_state_lru, new_state_hgrn)
```

```python
import functools
import math

import jax
import jax.numpy as jnp
from jax import lax
from jax.experimental import pallas as pl
from jax.experimental.pallas import tpu as pltpu

F32 = jnp.float32
BF16 = jnp.bfloat16

LANE = 128
SUBLANE = 8
VMEM_LIMIT = 56 * 1024 * 1024

D_MODEL = 4096
DEPTH = 2
BATCH, SEQ = 16, 256
DEC_BATCH, DEC_SEQ = 2, 1024
GRID_W = 64
W_BR = D_MODEL // 2
S5_GROUP = 16
S5_STATE = 64
LRU_C = 8.0
CONV_W = 4
HG_DK = 128
HG_CHUNK = 16
N_BRANCH = 3
IN_COLS = 9 * W_BR + N_BRANCH * D_MODEL
DN_ALPHA = (2 * DEPTH) ** 0.25
LN_EPS = 1e-5
RMS_EPS = 1e-6

N_CTX = BATCH * SEQ
N_LAT = DEC_BATCH * DEC_SEQ
N_TOK = N_CTX + N_LAT
LAT_SEG = 8
LAT_LEN = DEC_SEQ // LAT_SEG
LAT_ROWS = LAT_SEG * DEC_BATCH

C_UA, C_GA, C_XB, C_GB, C_QC, C_FF, C_FB, C_IC, C_GC = (i * W_BR for i in range(9))
C_MA = 9 * W_BR

TM = 1024
TN = 512
TM_IN = 2048
TM_MERGE = 1024
TN_MERGE = 256
TM_LN = 256
HG_RB = 256
HG_ST = 64
HG_HPB = 2
HG_SCB = 128
S5_PAIRS = 4


def _cparams(*sem):
    return pltpu.CompilerParams(dimension_semantics=sem, vmem_limit_bytes=VMEM_LIMIT)


def _sigmoid(x):
    return 0.5 * jnp.tanh(0.5 * x) + 0.5


def _silu(x):
    return x * _sigmoid(x)


def _mods_kernel(c_ref, w_ref, b_ref, o_ref):
    c = c_ref[...]
    s = _silu(c).astype(BF16)
    w = w_ref[...].astype(BF16)
    o_ref[...] = jnp.dot(s, w, preferred_element_type=F32) + b_ref[...]


def _mods_call(cvec, w_ada, b_ada):
    tn = TN
    return pl.pallas_call(
        _mods_kernel,
        out_shape=jax.ShapeDtypeStruct((DEPTH, SUBLANE, 3 * D_MODEL), F32),
        grid=(DEPTH, 3 * D_MODEL // tn),
        in_specs=[
            pl.BlockSpec((SUBLANE, D_MODEL), lambda l, j: (0, 0)),
            pl.BlockSpec((None, D_MODEL, tn), lambda l, j: (l, 0, j)),
            pl.BlockSpec((None, 1, tn), lambda l, j: (l, 0, j)),
        ],
        out_specs=pl.BlockSpec((None, SUBLANE, tn), lambda l, j: (l, 0, j)),
        compiler_params=_cparams("arbitrary", "arbitrary"),
        name="adaln_mods",
    )(cvec, w_ada, b_ada.reshape(DEPTH, 1, 3 * D_MODEL))


def _mod_row_ln(i):
    n_ctx_tiles = N_CTX // TM_LN
    return jnp.where(i < n_ctx_tiles, 0, 1 + (i - n_ctx_tiles) // (DEC_SEQ // TM_LN))


def _mod_row_mm(i):
    return jnp.maximum(i - (N_CTX // TM - 1), 0)


def _ln_mod_kernel(xc_ref, xl_ref, shift_ref, scale_ref, o_ref):
    def emit(x_ref):
        x = x_ref[...]
        mu = jnp.mean(x, axis=-1, keepdims=True)
        xc = x - mu
        var = jnp.mean(xc * xc, axis=-1, keepdims=True)
        h = xc * lax.rsqrt(var + LN_EPS) * (1.0 + scale_ref[0]) + shift_ref[0]
        o_ref[...] = h.astype(BF16)

    is_ctx = pl.program_id(0) < N_CTX // TM_LN
    pl.when(is_ctx)(lambda: emit(xc_ref))
    pl.when(jnp.logical_not(is_ctx))(lambda: emit(xl_ref))


def _split_rows(tile, n_col_axes):
    n_ctx_tiles = N_CTX // tile
    if n_col_axes == 0:
        return (lambda i: (jnp.minimum(i, n_ctx_tiles - 1), 0),
                lambda i: (jnp.maximum(i - n_ctx_tiles, 0), 0))
    return (lambda i, j: (jnp.minimum(i, n_ctx_tiles - 1), jnp.where(i < n_ctx_tiles, j, 0)),
            lambda i, j: (jnp.maximum(i - n_ctx_tiles, 0), jnp.where(i < n_ctx_tiles, 0, j)))


def _ln_mod_call(x_ctx, x_lat, mods3):
    ctx_map, lat_map = _split_rows(TM_LN, 0)
    return pl.pallas_call(
        _ln_mod_kernel,
        out_shape=jax.ShapeDtypeStruct((N_TOK, D_MODEL), BF16),
        grid=(N_TOK // TM_LN,),
        in_specs=[
            pl.BlockSpec((TM_LN, D_MODEL), ctx_map),
            pl.BlockSpec((TM_LN, D_MODEL), lat_map),
            pl.BlockSpec((1, 1, D_MODEL), lambda i: (_mod_row_ln(i), 0, 0)),
            pl.BlockSpec((1, 1, D_MODEL), lambda i: (_mod_row_ln(i), 0, 1)),
        ],
        out_specs=pl.BlockSpec((TM_LN, D_MODEL), lambda i: (i, 0)),
        compiler_params=_cparams("arbitrary"),
        name="ln_modulate",
    )(x_ctx, x_lat, mods3, mods3)


def _inproj_kernel(h_ref, w_ref, b_ref, o_ref):
    w = w_ref[...].astype(BF16)
    o_ref[...] = jnp.dot(h_ref[...], w, preferred_element_type=F32) + b_ref[...]


def _inproj_call(h, w_in, b_in3, l):
    return pl.pallas_call(
        _inproj_kernel,
        out_shape=jax.ShapeDtypeStruct((N_TOK, IN_COLS), F32),
        grid=(N_TOK // TM_IN, IN_COLS // TN),
        in_specs=[
            pl.BlockSpec((TM_IN, D_MODEL), lambda i, j: (i, 0), pipeline_mode=pl.Buffered(1)),
            pl.BlockSpec((None, D_MODEL, TN), lambda i, j: (l, 0, j)),
            pl.BlockSpec((None, 1, TN), lambda i, j: (l, 0, j)),
        ],
        out_specs=pl.BlockSpec((TM_IN, TN), lambda i, j: (i, j)),
        compiler_params=_cparams("arbitrary", "arbitrary"),
        name="in_proj",
    )(h, w_in, b_in3)


def _s5_disc_kernel(are_ref, aim_ref, ldt_ref, lre_ref, lim_ref, zre_ref, zim_ref):
    ar = jnp.minimum(are_ref[...], -1e-4)
    ai = aim_ref[...]
    dt = jnp.exp(ldt_ref[...])
    mag = jnp.exp(dt * ar)
    lam_re = mag * jnp.cos(dt * ai)
    lam_im = mag * jnp.sin(dt * ai)
    den = ar * ar + ai * ai
    lre_ref[...] = lam_re
    lim_ref[...] = lam_im
    zre_ref[...] = ((lam_re - 1.0) * ar + lam_im * ai) / den
    zim_ref[...] = (lam_im * ar - (lam_re - 1.0) * ai) / den


def _s5_disc_call(a_re, a_im, log_dt):
    g = a_re.shape[1]
    shp = (2 * g, S5_STATE)
    ldt = jnp.broadcast_to(log_dt[..., None], (2, g, S5_STATE)).reshape(shp)
    outs = pl.pallas_call(
        _s5_disc_kernel,
        out_shape=[jax.ShapeDtypeStruct(shp, F32)] * 4,
        name="s5_discretise",
    )(a_re.reshape(shp), a_im.reshape(shp), ldt)
    return [o.reshape(2, g, S5_STATE) for o in outs]


def _s5_pack(lam_re, lam_im, z_re, z_im, b_re, b_im, c_re, c_im):
    g = lam_re.shape[1]
    nblk = g // (2 * S5_PAIRS)
    pair_rows = 2 * S5_GROUP
    bz_re = z_re[..., None] * b_re[None] - z_im[..., None] * b_im[None]
    bz_im = z_re[..., None] * b_im[None] + z_im[..., None] * b_re[None]

    def pairs(t):
        t = t.reshape(t.shape[:-3] + (g // 2, 2) + t.shape[-2:])
        return t[..., 0, :, :], t[..., 1, :, :]

    def embed(t, axis):
        t = t.reshape(t.shape[:-3] + (nblk, S5_PAIRS) + t.shape[-2:])
        out = []
        for j in range(S5_PAIRS):
            pad = [(0, 0)] * (t.ndim - 1)
            pad[axis] = (j * pair_rows, LANE - (j + 1) * pair_rows)
            out.append(jnp.pad(t[..., j, :, :], pad))
        return jnp.stack(out, axis=-3)

    br0, br1 = pairs(jnp.swapaxes(bz_re, -1, -2))
    bi0, bi1 = pairs(jnp.swapaxes(bz_im, -1, -2))
    zb = jnp.zeros_like(br0)
    bp = jnp.concatenate([jnp.concatenate([br0, zb, bi0, zb], axis=-1),
                          jnp.concatenate([zb, br1, zb, bi1], axis=-1)], axis=-2)
    bp = embed(bp, -2).astype(BF16)

    cr0, cr1 = pairs(jnp.swapaxes(c_re, -1, -2))
    ci0, ci1 = pairs(jnp.swapaxes(c_im, -1, -2))
    zc = jnp.zeros_like(cr0)
    cp = jnp.concatenate([jnp.concatenate([cr0, zc], axis=-1), jnp.concatenate([zc, cr1], axis=-1),
                          jnp.concatenate([-ci0, zc], axis=-1), jnp.concatenate([zc, -ci1], axis=-1)],
                         axis=-2)
    cp = embed(cp, -1).astype(BF16)

    def pack_l(t):
        return t.reshape(2, nblk, S5_PAIRS, LANE)

    lam = jnp.stack([pack_l(lam_re), pack_l(lam_im)], axis=-2)
    return bp, cp, lam


def _s5_pack_state(h0):
    b, _, g, _, _ = h0.shape
    nblk = g // (2 * S5_PAIRS)
    t = h0.reshape(b, 2, nblk, S5_PAIRS, 2, S5_STATE, 2)
    t = jnp.transpose(t, (1, 2, 3, 0, 6, 4, 5))
    return t.reshape(2, nblk, S5_PAIRS, b, 2 * LANE)


def _s5_unpack_state(fin, b):
    nblk = fin.shape[1]
    t = fin.reshape(2, nblk, S5_PAIRS, b, 2, 2, S5_STATE)
    t = jnp.transpose(t, (3, 0, 1, 2, 5, 6, 4))
    return t.reshape(b, 2, nblk * S5_PAIRS * 2, S5_STATE, 2)


def _cmul(ar, ai, br, bi):
    return ar * br - ai * bi, ar * bi + ai * br


def _tm_pitch(L):
    return L + SUBLANE


def _gather_tm(src_ref, dst_ref, pad_ref, L, B):
    pitch = _tm_pitch(L)
    for r in range(B):
        pad_ref[r * pitch:r * pitch + L, :] = src_ref[r * L:(r + 1) * L, :]

    def body(t, c):
        dst_ref[t] = pad_ref[pl.ds(t, B, stride=pitch), :]
        return c
    lax.fori_loop(0, L, body, 0, unroll=8)


def _scatter_tm(src_ref, pad_ref, L, B):
    pitch = _tm_pitch(L)

    def body(t, c):
        pad_ref[pl.ds(t, B, stride=pitch), :] = src_ref[t]
        return c
    lax.fori_loop(0, L, body, 0, unroll=8)


def _s5_kernel(*refs, L, B, nseg):
    if nseg > 1:
        (u_ref, bp_ref, cp_ref, lam_ref, d_ref, h0_ref, _, y_ref,
         xs_ref, yacc_ref, utm_ref, pad_ref, fl_ref, hp_ref) = refs
    else:
        u_ref, bp_ref, cp_ref, lam_ref, d_ref, y_ref, fin_ref, xs_ref, yacc_ref, utm_ref, pad_ref = refs
    nb = B // nseg
    _gather_tm(u_ref, utm_ref, pad_ref, L, B)
    u2 = utm_ref[...].reshape(L * B, LANE)
    u2b = u2.astype(BF16)
    yacc_ref[...] = jnp.zeros((L * B, LANE), F32)

    for half in range(S5_PAIRS // 2):
        chains = [(d, jj) for d in range(2) for jj in range(2)]
        for d, jj in chains:
            x = jnp.dot(u2b, bp_ref[d, 0, 2 * half + jj], preferred_element_type=F32)
            xs_ref[d, jj] = x.reshape(L, B, 2 * LANE)
        lam = {}
        for d, jj in chains:
            lr = jnp.broadcast_to(lam_ref[d, 0, 2 * half + jj, 0:1, :], (B, LANE))
            li = jnp.broadcast_to(lam_ref[d, 0, 2 * half + jj, 1:2, :], (B, LANE))
            lam[(d, jj)] = (lr, li)

        def load_x(i):
            out = []
            for d, jj in chains:
                t = i if d == 0 else L - 1 - i
                out += [xs_ref[d, jj, t, :, 0:LANE], xs_ref[d, jj, t, :, LANE:2 * LANE]]
            return out

        nc = 2 * len(chains)

        def step(i, carry):
            nxt = load_x(jnp.minimum(i + 1, L - 1))
            out = []
            for n, (d, jj) in enumerate(chains):
                t = i if d == 0 else L - 1 - i
                hr, hi = carry[2 * n], carry[2 * n + 1]
                lr, li = lam[(d, jj)]
                pr, pi = _cmul(lr, li, hr, hi)
                nr = pr + carry[nc + 2 * n]
                ni = pi + carry[nc + 2 * n + 1]
                xs_ref[d, jj, t, :, 0:LANE] = nr
                xs_ref[d, jj, t, :, LANE:2 * LANE] = ni
                out += [nr, ni]
            return tuple(out + nxt)

        zero = jnp.zeros((B, LANE), F32)
        carry = lax.fori_loop(0, L, step, (zero,) * nc + tuple(load_x(0)), unroll=4)

        if nseg == 1:
            for n, (d, jj) in enumerate(chains):
                fin_ref[d, 0, 2 * half + jj, :, 0:LANE] = carry[2 * n]
                fin_ref[d, 0, 2 * half + jj, :, LANE:2 * LANE] = carry[2 * n + 1]
        else:
            hp = {}
            for n, (d, jj) in enumerate(chains):
                plr = lam_ref[d, 0, 2 * half + jj, 0:1, :]
                pli = lam_ref[d, 0, 2 * half + jj, 1:2, :]
                for _ in range(int(math.log2(L))):
                    plr, pli = _cmul(plr, pli, plr, pli)
                fl_ref[0] = carry[2 * n]
                fl_ref[1] = carry[2 * n + 1]
                order = list(range(nseg)) if d == 0 else list(range(nseg - 1, -1, -1))
                for bb in range(nb):
                    cr = h0_ref[d, 0, 2 * half + jj, bb:bb + 1, 0:LANE]
                    ci = h0_ref[d, 0, 2 * half + jj, bb:bb + 1, LANE:2 * LANE]
                    for k, s in enumerate(order):
                        if k > 0:
                            rp = bb * nseg + order[k - 1]
                            mr, mi = _cmul(plr, pli, cr, ci)
                            cr = fl_ref[0, rp:rp + 1, :] + mr
                            ci = fl_ref[1, rp:rp + 1, :] + mi
                        r = bb * nseg + s
                        hp_ref[n, 0, r:r + 1, :] = cr
                        hp_ref[n, 1, r:r + 1, :] = ci
                hp[(d, jj)] = (hp_ref[n, 0], hp_ref[n, 1])

            def cstep(i, cw):
                nxt = load_x(jnp.minimum(i + 1, L - 1))
                out = []
                for n, (d, jj) in enumerate(chains):
                    t = i if d == 0 else L - 1 - i
                    lr, li = lam[(d, jj)]
                    nr, ni = _cmul(lr, li, cw[2 * n], cw[2 * n + 1])
                    xs_ref[d, jj, t, :, 0:LANE] = cw[nc + 2 * n] + nr
                    xs_ref[d, jj, t, :, LANE:2 * LANE] = cw[nc + 2 * n + 1] + ni
                    out += [nr, ni]
                return tuple(out + nxt)

            cw0 = []
            for d, jj in chains:
                cw0 += list(hp[(d, jj)])
            lax.fori_loop(0, L, cstep, tuple(cw0 + load_x(0)), unroll=4)

        for jj in range(2):
            hs = (xs_ref[0, jj] + xs_ref[1, jj]).reshape(L * B, 2 * LANE).astype(BF16)
            yacc_ref[...] += jnp.dot(hs, cp_ref[0, 2 * half + jj], preferred_element_type=F32)

    y = u2 * d_ref[...] + yacc_ref[...]
    utm_ref[...] = jax.nn.gelu(y).reshape(L, B, LANE)
    _scatter_tm(utm_ref, pad_ref, L, B)
    pitch = _tm_pitch(L)
    for r in range(B):
        y_ref[r * L:(r + 1) * L, :] = pad_ref[r * pitch:r * pitch + L, :]


def _s5_call(src, col0, bp, cp, lam, d_skip, h0, y_prev, *, L, B, nseg, rblk, n_rows):
    w = d_skip.shape[-1]
    nblk = w // LANE
    cblk0 = col0 // LANE
    in_specs = [
        pl.BlockSpec((L * B, LANE), lambda k: (rblk, cblk0 + k)),
        pl.BlockSpec((2, 1, S5_PAIRS, LANE, 2 * LANE), lambda k: (0, k, 0, 0, 0)),
        pl.BlockSpec((1, S5_PAIRS, 2 * LANE, LANE), lambda k: (k, 0, 0, 0)),
        pl.BlockSpec((2, 1, S5_PAIRS, 2, LANE), lambda k: (0, k, 0, 0, 0)),
        pl.BlockSpec((1, LANE), lambda k: (0, k)),
    ]
    args = [src, bp, cp, lam, d_skip.reshape(1, w)]
    y_shape = jax.ShapeDtypeStruct((n_rows, w), F32)
    y_spec = pl.BlockSpec((L * B, LANE), lambda k: (rblk, k))
    scratch = [pltpu.VMEM((2, 2, L, B, 2 * LANE), F32), pltpu.VMEM((L * B, LANE), F32),
               pltpu.VMEM((L, B, LANE), F32), pltpu.VMEM((B * _tm_pitch(L), LANE), F32)]
    aliases = {}
    if nseg > 1:
        nb = B // nseg
        in_specs += [pl.BlockSpec((2, 1, S5_PAIRS, nb, 2 * LANE), lambda k: (0, k, 0, 0, 0)),
                     pl.BlockSpec(memory_space=pl.ANY)]
        args += [h0, y_prev]
        aliases = {len(args) - 1: 0}
        out_shape, out_specs = y_shape, y_spec
        scratch += [pltpu.VMEM((2, B, LANE), F32), pltpu.VMEM((4, 2, B, LANE), F32)]
    else:
        out_shape = [y_shape, jax.ShapeDtypeStruct((2, nblk, S5_PAIRS, B, 2 * LANE), F32)]
        out_specs = [y_spec, pl.BlockSpec((2, 1, S5_PAIRS, B, 2 * LANE), lambda k: (0, k, 0, 0, 0))]
    return pl.pallas_call(
        functools.partial(_s5_kernel, L=L, B=B, nseg=nseg),
        out_shape=out_shape,
        grid=(nblk,),
        in_specs=in_specs,
        out_specs=out_specs,
        scratch_shapes=scratch,
        input_output_aliases=aliases,
        compiler_params=_cparams("arbitrary"),
        name="s5_scan_seg" if nseg > 1 else "s5_scan",
    )(*args)


def _glu_kernel(yrow_ref, w_ref, b_ref, g_ref, o_ref):
    w = w_ref[...].astype(BF16)
    z = jnp.dot(yrow_ref[...].astype(BF16), w, preferred_element_type=F32) + b_ref[...]
    col0 = pl.multiple_of(pl.program_id(1) * TN, TN)
    ytile = yrow_ref[:, pl.ds(col0, TN)]
    o_ref[...] = (ytile * _sigmoid(z) * _silu(g_ref[...])).astype(BF16)


def _glu_call(y, w_glu, b_glu3, proj, l):
    goff = C_GA // TN
    return pl.pallas_call(
        _glu_kernel,
        out_shape=jax.ShapeDtypeStruct((N_TOK, W_BR), BF16),
        grid=(N_TOK // TM, W_BR // TN),
        in_specs=[
            pl.BlockSpec((TM, W_BR), lambda i, j: (i, 0)),
            pl.BlockSpec((None, W_BR, TN), lambda i, j: (l, 0, j)),
            pl.BlockSpec((None, 1, TN), lambda i, j: (l, 0, j)),
            pl.BlockSpec((TM, TN), lambda i, j: (i, goff + j)),
        ],
        out_specs=pl.BlockSpec((TM, TN), lambda i, j: (i, j)),
        compiler_params=_cparams("arbitrary", "arbitrary"),
        name="s5_glu",
    )(y, w_glu, b_glu3, proj)


def _expm1(x):
    t = jnp.tanh(0.5 * x)
    return 2.0 * t / (1.0 - t)


def _lru_kernel(*refs, L, B, nseg, period):
    if nseg > 1:
        (x_ref, g_ref, cw_ref, cb_ref, wa_ref, ba_ref, wx_ref, bx_ref, sp_ref, h0_ref, _,
         y_ref, a_ref, b_ref, ytok_ref, fl_ref, hp_ref) = refs
    else:
        (x_ref, g_ref, cw_ref, cb_ref, wa_ref, ba_ref, wx_ref, bx_ref, sp_ref,
         y_ref, fin_ref, a_ref, b_ref, ytok_ref) = refs
    nb = B // nseg
    _gather_tm(x_ref, a_ref.at[0], ytok_ref, L, B)
    x = a_ref[0]
    pos = lax.broadcasted_iota(jnp.int32, (L, B, LANE), 0) % period
    xc = jnp.broadcast_to(cb_ref[...].reshape(1, 1, LANE), (L, B, LANE))
    for k in range(CONV_W):
        off = k - CONV_W // 2
        if off < 0:
            xs = jnp.concatenate([jnp.zeros((-off, B, LANE), F32), x[:L + off]], axis=0)
        elif off > 0:
            xs = jnp.concatenate([x[off:], jnp.zeros((off, B, LANE), F32)], axis=0)
        else:
            xs = x
        if period < L:
            xs = jnp.where((pos + off >= 0) & (pos + off < period), xs, 0.0)
        xc = xc + cw_ref[k:k + 1, :].reshape(1, 1, LANE) * xs
    xc2 = xc.reshape(L * B, LANE)
    xcb = xc2.astype(BF16)
    for d in range(2):
        r = _sigmoid(jnp.dot(xcb, wa_ref[d, 0].astype(BF16), preferred_element_type=F32)
                     + ba_ref[d:d + 1, :])
        ig = _sigmoid(jnp.dot(xcb, wx_ref[d, 0].astype(BF16), preferred_element_type=F32)
                      + bx_ref[d:d + 1, :])
        log_a = (-LRU_C) * r * sp_ref[d:d + 1, :]
        a_ref[d] = jnp.exp(log_a).reshape(L, B, LANE)
        m2 = -_expm1(2.0 * log_a)
        mult = jnp.where(m2 > 0.0, m2 * lax.rsqrt(m2), 0.0)
        b_ref[d] = (mult * (ig * xc2)).reshape(L, B, LANE)

    def load_ab(i):
        out = []
        for d in range(2):
            t = i if d == 0 else L - 1 - i
            out += [a_ref[d, t], b_ref[d, t]]
        return out

    def step(i, carry):
        nxt = load_ab(jnp.minimum(i + 1, L - 1))
        out = []
        for d in range(2):
            t = i if d == 0 else L - 1 - i
            a = carry[4 + 2 * d]
            h = a * carry[2 * d] + carry[4 + 2 * d + 1]
            b_ref[d, t] = h
            if nseg > 1:
                p = a * carry[2 * d + 1]
                a_ref[d, t] = p
            else:
                p = carry[2 * d + 1]
            out += [h, p]
        return tuple(out + nxt)

    zero = jnp.zeros((B, LANE), F32)
    one = jnp.ones((B, LANE), F32)
    carry = lax.fori_loop(0, L, step, (zero, one, zero, one) + tuple(load_ab(0)), unroll=8)

    if nseg == 1:
        fin_ref[0] = carry[0]
        fin_ref[1] = carry[2]
    else:
        hps = []
        for d in range(2):
            fl_ref[0] = carry[2 * d]
            fl_ref[1] = carry[2 * d + 1]
            order = list(range(nseg)) if d == 0 else list(range(nseg - 1, -1, -1))
            for bb in range(nb):
                c = h0_ref[d, bb:bb + 1, :]
                for k, s in enumerate(order):
                    if k > 0:
                        rp = bb * nseg + order[k - 1]
                        c = fl_ref[0, rp:rp + 1, :] + fl_ref[1, rp:rp + 1, :] * c
                    r = bb * nseg + s
                    hp_ref[d, r:r + 1, :] = c
            hps.append(hp_ref[d])

        def cstep(i, c):
            nxt = load_ab(jnp.minimum(i + 1, L - 1))
            for d in range(2):
                t = i if d == 0 else L - 1 - i
                b_ref[d, t] = c[2 * d + 1] + c[2 * d] * hps[d]
            return tuple(nxt)

        lax.fori_loop(0, L, cstep, tuple(load_ab(0)), unroll=4)

    a_ref[0] = b_ref[0] + b_ref[1]
    _scatter_tm(a_ref.at[0], ytok_ref, L, B)
    pitch = _tm_pitch(L)
    for r in range(B):
        rows = slice(r * L, (r + 1) * L)
        y_ref[rows, :] = (ytok_ref[r * pitch:r * pitch + L, :] * _silu(g_ref[rows, :])).astype(BF16)


def _lru_call(src, col_x, col_g, conv_w, conv_b, w_a, b_a, w_x, b_x, sp, h0, y_prev,
              *, L, B, nseg, period, rblk, n_rows):
    w = conv_b.shape[-1]
    nblk = w // LANE

    def rows(col0):
        return pl.BlockSpec((L * B, LANE), lambda k: (rblk, col0 // LANE + k))

    y_spec = pl.BlockSpec((L * B, LANE), lambda k: (rblk, k))
    in_specs = [
        rows(col_x), rows(col_g),
        pl.BlockSpec((CONV_W, LANE), lambda k: (0, k)),
        pl.BlockSpec((1, LANE), lambda k: (0, k)),
        pl.BlockSpec((2, 1, LANE, LANE), lambda k: (0, k, 0, 0)),
        pl.BlockSpec((2, LANE), lambda k: (0, k)),
        pl.BlockSpec((2, 1, LANE, LANE), lambda k: (0, k, 0, 0)),
        pl.BlockSpec((2, LANE), lambda k: (0, k)),
        pl.BlockSpec((2, LANE), lambda k: (0, k)),
    ]
    args = [src, src, conv_w, conv_b.reshape(1, w), w_a, b_a, w_x, b_x, sp]
    y_shape = jax.ShapeDtypeStruct((n_rows, w), BF16)
    scratch = [pltpu.VMEM((2, L, B, LANE), F32), pltpu.VMEM((2, L, B, LANE), F32),
               pltpu.VMEM((B * _tm_pitch(L), LANE), F32)]
    aliases = {}
    if nseg > 1:
        nb = B // nseg
        in_specs += [pl.BlockSpec((2, nb, LANE), lambda k: (0, 0, k)), pl.BlockSpec(memory_space=pl.ANY)]
        args += [h0, y_prev]
        aliases = {len(args) - 1: 0}
        out_shape, out_specs = y_shape, y_spec
        scratch += [pltpu.VMEM((2, B, LANE), F32), pltpu.VMEM((2, B, LANE), F32)]
    else:
        out_shape = [y_shape, jax.ShapeDtypeStruct((2, B, w), F32)]
        out_specs = [y_spec, pl.BlockSpec((2, B, LANE), lambda k: (0, 0, k))]
    return pl.pallas_call(
        functools.partial(_lru_kernel, L=L, B=B, nseg=nseg, period=period),
        out_shape=out_shape,
        grid=(nblk,),
        in_specs=in_specs,
        out_specs=out_specs,
        scratch_shapes=scratch,
        input_output_aliases=aliases,
        compiler_params=_cparams("arbitrary"),
        name="rglru_seg" if nseg > 1 else "rglru",
    )(*args)


def _nt_dot(a, b):
    return lax.dot_general(a, b, (((1,), (1,)), ((), ())), preferred_element_type=F32)


def _hgrn_kernel(*refs, L, has_h0, n_aliased):
    q_ref, ff_ref, fb_ref, v_ref, g_ref, lb_ref, nw_ref = refs[:7]
    h0_ref = refs[7] if has_h0 else None
    y_ref, fin_ref, of_ref = refs[7 + has_h0 + n_aliased:]
    rb_rows = HG_RB
    nrb = L // rb_rows
    nsub = HG_ST // HG_CHUNK
    nch = rb_rows // HG_ST
    nscb = rb_rows // HG_SCB
    row = lax.broadcasted_iota(jnp.int32, (HG_SCB, HG_SCB), 0)
    col = lax.broadcasted_iota(jnp.int32, (HG_SCB, HG_SCB), 1)
    sub_dist = jnp.where((row // HG_ST) == (col // HG_ST), row // HG_CHUNK - col // HG_CHUNK, 2 * nsub)
    ridx = lax.broadcasted_iota(jnp.int32, (rb_rows, HG_DK), 0)
    pos = ridx % HG_CHUNK
    sic = (ridx // HG_CHUNK) % nsub
    chunk_id = ridx // HG_ST

    def sub_bcast(x, idx):
        x3 = x.reshape(rb_rows // HG_CHUNK, HG_CHUNK, HG_DK)
        return jnp.broadcast_to(x3[:, idx:idx + 1, :], x3.shape).reshape(rb_rows, HG_DK)

    def chunk_bcast(x, idx):
        x3 = x.reshape(nch, HG_ST, HG_DK)
        return jnp.broadcast_to(x3[:, idx:idx + 1, :], x3.shape).reshape(rb_rows, HG_DK)

    shared = {}

    def load_qv(rows, cs, key):
        if key is not None and key in shared:
            return shared[key]
        v = v_ref[rows, cs]
        out = (_silu(q_ref[rows, cs]), v.astype(BF16), v.T.astype(BF16))
        if key is not None:
            shared[key] = out
        return out

    def run_block(rb, st, d, hh):
        cs = slice(hh * HG_DK, (hh + 1) * HG_DK)
        if isinstance(rb, int):
            rows = slice(rb * rb_rows, (rb + 1) * rb_rows)
        else:
            rows = pl.ds(pl.multiple_of(rb * rb_rows, rb_rows), rb_rows)
        qf, vb, vt = load_qv(rows, cs, (hh, rb) if isinstance(rb, int) else None)
        f = (ff_ref if d == 0 else fb_ref)[rows, cs]
        lbd = lb_ref[d:d + 1, cs]
        g = lbd + (1.0 - lbd) * jax.nn.sigmoid(f)
        kk = 1.0 - g
        b = jnp.log2(g)
        sgn = 1 if d == 0 else -1

        def shift(x, n):
            return pltpu.roll(x, n if d == 0 else rb_rows - n, 0)

        s = 1
        while s < HG_CHUNK:
            keep = (pos >= s) if d == 0 else (pos <= HG_CHUNK - 1 - s)
            b = b + jnp.where(keep, shift(b, s), 0.0)
            s *= 2
        tot = sub_bcast(b, HG_CHUNK - 1 if d == 0 else 0)
        prev = [shift(tot, HG_CHUNK * n) for n in range(1, nsub)]
        seen = sic if d == 0 else nsub - 1 - sic
        base = sum(jnp.where(seen >= n, prev[n - 1], 0.0) for n in range(1, nsub))
        ctot = chunk_bcast(b + base, HG_ST - 1 if d == 0 else 0)

        ed = jnp.exp2(b - sub_bcast(b, HG_CHUNK // 2))
        qt = (qf * ed).astype(BF16)
        kt = (kk / ed).astype(BF16)
        qe = qf * jnp.exp2(b)
        kd = kk * jnp.exp2(tot - b)
        lhs = [qe.astype(BF16)]
        acc = None
        for n in range(1, nsub - 1):
            acc = prev[n - 1] if acc is None else acc + prev[n - 1]
            lhs.append((qe * jnp.exp2(acc)).astype(BF16))
        kdb = kd.astype(BF16)
        causal = (col <= row) if d == 0 else (col >= row)
        o_parts = []
        for hb in range(nscb):
            rs = slice(hb * HG_SCB, (hb + 1) * HG_SCB)
            sc = jnp.where((sub_dist == 0) & causal, _nt_dot(qt[rs], kt[rs]), 0.0)
            scn = _nt_dot(jnp.concatenate([x[rs] for x in lhs], axis=0), kdb[rs])
            for n in range(1, nsub):
                sc = jnp.where(sub_dist == sgn * n, scn[(n - 1) * HG_SCB:n * HG_SCB], sc)
            o_parts.append(jnp.dot(sc.astype(BF16), vb[rs], preferred_element_type=F32))
        o = jnp.concatenate(o_parts, axis=0)

        qe_st = (qe * jnp.exp2(base)).astype(BF16)
        kd_st = kd * jnp.exp2(ctot - base - tot)
        uts = []
        kd_stb = kd_st.astype(BF16)
        zero_b = jnp.zeros_like(kd_stb)
        for c in range(0, nch, 2):
            rhs = jnp.concatenate([jnp.where(chunk_id == c, kd_stb, zero_b),
                                   jnp.where(chunk_id == c + 1, kd_stb, zero_b)], axis=1)
            ut2 = jnp.dot(vt, rhs, preferred_element_type=F32)
            uts += [ut2[:, :HG_DK], ut2[:, HG_DK:]]
        order = range(nch) if d == 0 else range(nch - 1, -1, -1)
        entering = [None] * nch
        for c in order:
            entering[c] = st
            tot_row = c * HG_ST + (HG_ST - 1 if d == 0 else 0)
            st = st * jnp.exp2(ctot[tot_row:tot_row + 1, :]) + uts[c]
        outs = []
        for c in range(nch):
            lo = c * HG_ST
            oi = _nt_dot(qe_st[lo:lo + HG_ST], entering[c].astype(BF16))
            outs.append(o[lo:lo + HG_ST] + oi)
        return jnp.concatenate(outs, axis=0), st

    streams = [(hh, d) for hh in range(HG_HPB) for d in range(2)]

    def body(i, sts):
        new = []
        for (hh, d), st in zip(streams, sts):
            rb = i if d == 0 else nrb - 1 - i
            oblk, st = run_block(rb, st, d, hh)
            cs = slice(hh * HG_DK, (hh + 1) * HG_DK)
            if isinstance(rb, int):
                of_ref[d, rb * rb_rows:(rb + 1) * rb_rows, cs] = oblk
            else:
                of_ref[d, pl.ds(pl.multiple_of(rb * rb_rows, rb_rows), rb_rows), cs] = oblk
            new.append(st)
        return tuple(new)

    if has_h0:
        sts = tuple(h0_ref[0, d, hh].T for hh, d in streams)
    else:
        sts = tuple(jnp.zeros((HG_DK, HG_DK), F32) for _ in streams)
    sts = body(0, sts) if nrb == 1 else lax.fori_loop(0, nrb, body, sts)
    for (hh, d), st in zip(streams, sts):
        fin_ref[0, d, hh] = st.T

    for hh in range(HG_HPB):
        cs = slice(hh * HG_DK, (hh + 1) * HG_DK)
        o = of_ref[0, :, cs] + of_ref[1, :, cs]
        o = o * lax.rsqrt(jnp.mean(o * o, axis=-1, keepdims=True) + RMS_EPS)
        y_ref[:, cs] = (o * nw_ref[:, cs] * _silu(g_ref[:, cs])).astype(BF16)


def _hgrn_call(proj, lb, norm_w, h0, y_prev, fin_prev, *, L, nb, row0, n_rows, layer,
               cols=(C_QC, C_FF, C_FB, C_IC, C_GC)):
    w = lb.shape[-1]
    nh = w // HG_DK
    rblk0 = row0 // L
    hw = HG_HPB * HG_DK

    def col(off):
        return pl.BlockSpec((L, hw), lambda b, h: (rblk0 + b, off // hw + h))

    in_specs = [col(c) for c in cols] + [
                pl.BlockSpec((2, hw), lambda b, h: (0, h)),
                pl.BlockSpec((1, hw), lambda b, h: (0, h))]
    args = [proj] * 5 + [lb, norm_w.reshape(1, w)]
    if h0 is not None:
        in_specs.append(pl.BlockSpec((1, 2, HG_HPB, HG_DK, HG_DK), lambda b, h: (b, 0, h, 0, 0)))
        args.append(h0)
    aliases = {}
    for out_idx, prev in enumerate((y_prev, fin_prev)):
        if prev is not None:
            in_specs.append(pl.BlockSpec(memory_space=pl.ANY))
            args.append(prev)
            aliases[len(args) - 1] = out_idx
    st_spec = pl.BlockSpec((1, None, 2, HG_HPB, HG_DK, HG_DK), lambda b, h: (b, layer, 0, h, 0, 0))
    return pl.pallas_call(
        functools.partial(_hgrn_kernel, L=L, has_h0=h0 is not None, n_aliased=len(aliases)),
        out_shape=[jax.ShapeDtypeStruct((n_rows, w), BF16),
                   jax.ShapeDtypeStruct((nb, DEPTH, 2, nh, HG_DK, HG_DK), F32)],
        grid=(nb, nh // HG_HPB),
        in_specs=in_specs,
        out_specs=[pl.BlockSpec((L, hw), lambda b, h: (rblk0 + b, h)), st_spec],
        scratch_shapes=[pltpu.VMEM((2, L, hw), F32)],
        input_output_aliases=aliases,
        compiler_params=_cparams("arbitrary", "arbitrary"),
        name="hgrn2",
    )(*args)


def _merge_kernel(ya_ref, yb_ref, yc_ref, wa_ref, wb_ref, wc_ref, ma_ref, mb_ref, mc_ref, o_ref):
    acc = None
    for y_ref, w_ref, m_ref in ((ya_ref, wa_ref, ma_ref), (yb_ref, wb_ref, mb_ref), (yc_ref, wc_ref, mc_ref)):
        t = jnp.dot(y_ref[...], w_ref[...].astype(BF16), preferred_element_type=F32)
        t = _sigmoid(m_ref[...]) * t
        acc = t if acc is None else acc + t
    o_ref[...] = acc.astype(BF16)


def _merge_call(ya, yb, yc, w_br, proj, l):
    tm, tn = TM_MERGE, TN_MERGE
    moff = C_MA // tn
    mstep = D_MODEL // tn
    lhs = pl.BlockSpec((tm, W_BR), lambda i, j: (i, 0))

    def wspec(br):
        return pl.BlockSpec((None, None, W_BR, tn), lambda i, j: (l, br, 0, j))

    def mspec(br):
        return pl.BlockSpec((tm, tn), lambda i, j: (i, moff + br * mstep + j))

    return pl.pallas_call(
        _merge_kernel,
        out_shape=jax.ShapeDtypeStruct((N_TOK, D_MODEL), BF16),
        grid=(N_TOK // tm, D_MODEL // tn),
        in_specs=[lhs, lhs, lhs, wspec(0), wspec(1), wspec(2), mspec(0), mspec(1), mspec(2)],
        out_specs=pl.BlockSpec((tm, tn), lambda i, j: (i, j)),
        compiler_params=_cparams("arbitrary", "arbitrary"),
        name="branch_merge",
    )(ya, yb, yc, w_br, w_br, w_br, proj, proj, proj)


def _outproj_kernel(m_ref, w_ref, b_ref, xc_ref, xl_ref, gate_ref, o_ref):
    out = jnp.dot(m_ref[...], w_ref[...].astype(BF16), preferred_element_type=F32) + b_ref[...]
    gated = gate_ref[0] * out
    is_ctx = pl.program_id(0) < N_CTX // TM

    @pl.when(is_ctx)
    def _():
        o_ref[...] = DN_ALPHA * xc_ref[...] + gated

    @pl.when(jnp.logical_not(is_ctx))
    def _():
        o_ref[...] = DN_ALPHA * xl_ref[...] + gated


def _outproj_call(merged, w_out, b_out3, x_ctx, x_lat, mods3, l):
    goff = 2 * D_MODEL // TN
    ctx_map, lat_map = _split_rows(TM, 1)
    return pl.pallas_call(
        _outproj_kernel,
        out_shape=jax.ShapeDtypeStruct((N_TOK, D_MODEL), F32),
        grid=(N_TOK // TM, D_MODEL // TN),
        in_specs=[
            pl.BlockSpec((TM, D_MODEL), lambda i, j: (i, 0)),
            pl.BlockSpec((None, D_MODEL, TN), lambda i, j: (l, 0, j)),
            pl.BlockSpec((None, 1, TN), lambda i, j: (l, 0, j)),
            pl.BlockSpec((TM, TN), ctx_map),
            pl.BlockSpec((TM, TN), lat_map),
            pl.BlockSpec((1, 1, TN), lambda i, j: (_mod_row_mm(i), 0, goff + j)),
        ],
        out_specs=pl.BlockSpec((TM, TN), lambda i, j: (i, j)),
        compiler_params=_cparams("arbitrary", "arbitrary"),
        name="out_proj",
    )(merged, w_out, b_out3, x_ctx, x_lat, mods3)


def _ln_affine_kernel(r_ref, g_ref, b_ref, o_ref):
    r = r_ref[...]
    mu = jnp.mean(r, axis=-1, keepdims=True)
    rc = r - mu
    var = jnp.mean(rc * rc, axis=-1, keepdims=True)
    o_ref[...] = rc * lax.rsqrt(var + LN_EPS) * g_ref[...] + b_ref[...]


def _ln_affine_call(r, g, b, l, row0=0, n_rows=N_TOK):
    rblk0 = row0 // TM_LN
    return pl.pallas_call(
        _ln_affine_kernel,
        out_shape=jax.ShapeDtypeStruct((n_rows, D_MODEL), F32),
        grid=(n_rows // TM_LN,),
        in_specs=[
            pl.BlockSpec((TM_LN, D_MODEL), lambda i: (rblk0 + i, 0)),
            pl.BlockSpec((None, 1, D_MODEL), lambda i: (l, 0, 0)),
            pl.BlockSpec((None, 1, D_MODEL), lambda i: (l, 0, 0)),
        ],
        out_specs=pl.BlockSpec((TM_LN, D_MODEL), lambda i: (i, 0)),
        compiler_params=_cparams("arbitrary"),
        name="post_ln",
    )(r, g, b)


def kernel(x_prompt, x_sample, state_s5, state_lru, state_hgrn, c, c_ctx, w_ada, b_ada, w_in, b_in,
           s5_a_re, s5_a_im, s5_log_dt, s5_b_re, s5_b_im, s5_c_re, s5_c_im, s5_d, s5_w_glu, s5_b_glu,
           lru_conv_w, lru_conv_b, lru_w_a, lru_b_a, lru_w_x, lru_b_x, lru_lambda, hg_lb, hg_norm_w,
           w_br, w_out, b_out, ln_g, ln_b):
    lb_soft = jax.nn.softmax(hg_lb.astype(F32), axis=0)
    lb_all = jnp.cumsum(lb_soft, axis=0) - lb_soft[0]
    softplus_neg_lam = jax.nn.softplus(-lru_lambda.astype(F32))

    cvec = jnp.zeros((SUBLANE, D_MODEL), F32)
    cvec = cvec.at[0].set(c_ctx).at[1:1 + DEC_BATCH].set(c)
    mods = _mods_call(cvec, w_ada, b_ada)

    b_in3 = b_in.reshape(DEPTH, 1, IN_COLS)
    b_glu3 = s5_b_glu.reshape(DEPTH, 1, W_BR)
    b_out3 = b_out.reshape(DEPTH, 1, D_MODEL)
    ln_g3 = ln_g.reshape(DEPTH, 1, D_MODEL)
    ln_b3 = ln_b.reshape(DEPTH, 1, D_MODEL)

    x_ctx = x_prompt.reshape(N_CTX, D_MODEL)
    x_lat = x_sample.reshape(N_LAT, D_MODEL)
    st_s5, st_lru, st_hg = [], [], None
    for l in range(DEPTH):
        mods3 = mods[l, :1 + DEC_BATCH].reshape(1 + DEC_BATCH, 1, 3 * D_MODEL)
        h = _ln_mod_call(x_ctx, x_lat, mods3)
        proj = _inproj_call(h, w_in, b_in3, l)

        lam_re, lam_im, z_re, z_im = _s5_disc_call(s5_a_re[l], s5_a_im[l], s5_log_dt[l])
        bp, cp, lam = _s5_pack(lam_re, lam_im, z_re, z_im, s5_b_re[l], s5_b_im[l], s5_c_re[l], s5_c_im[l])
        ctx_rows = dict(L=SEQ, B=BATCH, nseg=1, rblk=0, n_rows=N_TOK)
        lat_rows = dict(L=LAT_LEN, B=LAT_ROWS, nseg=LAT_SEG, rblk=N_CTX // N_LAT, n_rows=N_TOK)
        y_pre, fin_s5 = _s5_call(proj, C_UA, bp, cp, lam, s5_d[l], None, None, **ctx_rows)
        y_pre = _s5_call(proj, C_UA, bp, cp, lam, s5_d[l], _s5_pack_state(state_s5[:, l]), y_pre, **lat_rows)
        y_a = _glu_call(y_pre, s5_w_glu, b_glu3, proj, l)
        st_s5.append(_s5_unpack_state(fin_s5, BATCH))

        lru_args = (lru_conv_w[l], lru_conv_b[l], lru_w_a[l], lru_b_a[l], lru_w_x[l], lru_b_x[l],
                    softplus_neg_lam[l])
        y_b, fin_lru = _lru_call(proj, C_XB, C_GB, *lru_args, None, None, period=SEQ, **ctx_rows)
        y_b = _lru_call(proj, C_XB, C_GB, *lru_args, jnp.transpose(state_lru[:, l], (1, 0, 2)), y_b,
                        period=GRID_W, **lat_rows)
        st_lru.append(jnp.transpose(fin_lru, (1, 0, 2)))

        y_c, st_hg = _hgrn_call(proj, lb_all[l], hg_norm_w[l], None, None, st_hg,
                                L=SEQ, nb=BATCH, row0=0, n_rows=N_TOK, layer=l)
        y_c, _ = _hgrn_call(proj, lb_all[l], hg_norm_w[l], state_hgrn[:, l], y_c, None,
                            L=DEC_SEQ, nb=DEC_BATCH, row0=N_CTX, n_rows=N_TOK, layer=0)

        merged = _merge_call(y_a, y_b, y_c, w_br, proj, l)
        r = _outproj_call(merged, w_out, b_out3, x_ctx, x_lat, mods3, l)
        x_ctx = _ln_affine_call(r, ln_g3, ln_b3, l, 0, N_CTX)
        x_lat = _ln_affine_call(r, ln_g3, ln_b3, l, N_CTX, N_LAT)

    y_prompt = x_ctx.reshape(BATCH, SEQ, D_MODEL)
    y_sample = x_lat.reshape(DEC_BATCH, DEC_SEQ, D_MODEL)
    new_state_s5 = jnp.stack(st_s5, axis=1)
    new_state_lru = jnp.stack(st_lru, axis=1)
    new_state_hgrn = st_hg
    return (y_prompt, y_sample, new_state_s5, new_state_lru, new_state_hgrn)
```

```python
import functools
import math

import jax
import jax.numpy as jnp
from jax import lax
from jax.experimental import pallas as pl
from jax.experimental.pallas import tpu as pltpu

F32 = jnp.float32
BF16 = jnp.bfloat16

LANE = 128
SUBLANE = 8
VMEM_LIMIT = 56 * 1024 * 1024

D_MODEL = 4096
DEPTH = 2
BATCH, SEQ = 16, 256
DEC_BATCH, DEC_SEQ = 2, 1024
GRID_W = 64
W_BR = D_MODEL // 2
S5_GROUP = 16
S5_STATE = 64
LRU_C = 8.0
CONV_W = 4
HG_DK = 128
HG_CHUNK = 16
N_BRANCH = 3
IN_COLS = 9 * W_BR + N_BRANCH * D_MODEL
DN_ALPHA = (2 * DEPTH) ** 0.25
LN_EPS = 1e-5
RMS_EPS = 1e-6

N_CTX = BATCH * SEQ
N_LAT = DEC_BATCH * DEC_SEQ
N_TOK = N_CTX + N_LAT
LAT_SEG = 8
LAT_LEN = DEC_SEQ // LAT_SEG
LAT_ROWS = LAT_SEG * DEC_BATCH

C_UA, C_GA, C_XB, C_GB, C_QC, C_FF, C_FB, C_IC, C_GC = (i * W_BR for i in range(9))
C_MA = 9 * W_BR

TM = 1024
TN = 512
TM_IN = 2048
TM_MERGE = 1024
TN_MERGE = 256
TM_LN = 512
HG_RB = 256
HG_ST = 64
HG_HPB = 2
HG_SCB = 128
S5_PAIRS = 4


def _cparams(*sem):
    return pltpu.CompilerParams(dimension_semantics=sem, vmem_limit_bytes=VMEM_LIMIT)


def _sigmoid(x):
    return 0.5 * jnp.tanh(0.5 * x) + 0.5


def _silu(x):
    return x * _sigmoid(x)


def _mods_kernel(c_ref, w_ref, b_ref, o_ref):
    c = c_ref[...]
    s = _silu(c).astype(BF16)
    w = w_ref[...].astype(BF16)
    o_ref[...] = jnp.dot(s, w, preferred_element_type=F32) + b_ref[...]


def _mods_call(cvec, w_ada, b_ada):
    tn = TN
    return pl.pallas_call(
        _mods_kernel,
        out_shape=jax.ShapeDtypeStruct((DEPTH, SUBLANE, 3 * D_MODEL), F32),
        grid=(DEPTH, 3 * D_MODEL // tn),
        in_specs=[
            pl.BlockSpec((SUBLANE, D_MODEL), lambda l, j: (0, 0)),
            pl.BlockSpec((None, D_MODEL, tn), lambda l, j: (l, 0, j)),
            pl.BlockSpec((None, 1, tn), lambda l, j: (l, 0, j)),
        ],
        out_specs=pl.BlockSpec((None, SUBLANE, tn), lambda l, j: (l, 0, j)),
        compiler_params=_cparams("arbitrary", "arbitrary"),
        name="adaln_mods",
    )(cvec, w_ada, b_ada.reshape(DEPTH, 1, 3 * D_MODEL))


def _mod_row_ln(i):
    n_ctx_tiles = N_CTX // TM_LN
    return jnp.where(i < n_ctx_tiles, 0, 1 + (i - n_ctx_tiles) // (DEC_SEQ // TM_LN))


def _mod_row_mm(i):
    return jnp.maximum(i - (N_CTX // TM - 1), 0)


def _ln_mod_kernel(xc_ref, xl_ref, shift_ref, scale_ref, o_ref):
    def emit(x_ref):
        x = x_ref[...]
        mu = jnp.mean(x, axis=-1, keepdims=True)
        xc = x - mu
        var = jnp.mean(xc * xc, axis=-1, keepdims=True)
        h = xc * lax.rsqrt(var + LN_EPS) * (1.0 + scale_ref[0]) + shift_ref[0]
        o_ref[...] = h.astype(BF16)

    is_ctx = pl.program_id(0) < N_CTX // TM_LN
    pl.when(is_ctx)(lambda: emit(xc_ref))
    pl.when(jnp.logical_not(is_ctx))(lambda: emit(xl_ref))


def _split_rows(tile, n_col_axes):
    n_ctx_tiles = N_CTX // tile
    if n_col_axes == 0:
        return (lambda i: (jnp.minimum(i, n_ctx_tiles - 1), 0),
                lambda i: (jnp.maximum(i - n_ctx_tiles, 0), 0))
    return (lambda i, j: (jnp.minimum(i, n_ctx_tiles - 1), jnp.where(i < n_ctx_tiles, j, 0)),
            lambda i, j: (jnp.maximum(i - n_ctx_tiles, 0), jnp.where(i < n_ctx_tiles, 0, j)))


def _ln_mod_call(x_ctx, x_lat, mods3):
    ctx_map, lat_map = _split_rows(TM_LN, 0)
    return pl.pallas_call(
        _ln_mod_kernel,
        out_shape=jax.ShapeDtypeStruct((N_TOK, D_MODEL), BF16),
        grid=(N_TOK // TM_LN,),
        in_specs=[
            pl.BlockSpec((TM_LN, D_MODEL), ctx_map),
            pl.BlockSpec((TM_LN, D_MODEL), lat_map),
            pl.BlockSpec((1, 1, D_MODEL), lambda i: (_mod_row_ln(i), 0, 0)),
            pl.BlockSpec((1, 1, D_MODEL), lambda i: (_mod_row_ln(i), 0, 1)),
        ],
        out_specs=pl.BlockSpec((TM_LN, D_MODEL), lambda i: (i, 0)),
        compiler_params=_cparams("arbitrary"),
        name="ln_modulate",
    )(x_ctx, x_lat, mods3, mods3)


def _inproj_kernel(h_ref, w_ref, b_ref, o_ref):
    w = w_ref[...].astype(BF16)
    o_ref[...] = jnp.dot(h_ref[...], w, preferred_element_type=F32) + b_ref[...]


def _inproj_call(h, w_in, b_in3, l):
    return pl.pallas_call(
        _inproj_kernel,
        out_shape=jax.ShapeDtypeStruct((N_TOK, IN_COLS), F32),
        grid=(N_TOK // TM_IN, IN_COLS // TN),
        in_specs=[
            pl.BlockSpec((TM_IN, D_MODEL), lambda i, j: (i, 0), pipeline_mode=pl.Buffered(1)),
            pl.BlockSpec((None, D_MODEL, TN), lambda i, j: (l, 0, j)),
            pl.BlockSpec((None, 1, TN), lambda i, j: (l, 0, j)),
        ],
        out_specs=pl.BlockSpec((TM_IN, TN), lambda i, j: (i, j)),
        compiler_params=_cparams("arbitrary", "arbitrary"),
        name="in_proj",
    )(h, w_in, b_in3)


def _s5_disc_kernel(are_ref, aim_ref, ldt_ref, lre_ref, lim_ref, zre_ref, zim_ref):
    ar = jnp.minimum(are_ref[...], -1e-4)
    ai = aim_ref[...]
    dt = jnp.exp(ldt_ref[...])
    mag = jnp.exp(dt * ar)
    lam_re = mag * jnp.cos(dt * ai)
    lam_im = mag * jnp.sin(dt * ai)
    den = ar * ar + ai * ai
    lre_ref[...] = lam_re
    lim_ref[...] = lam_im
    zre_ref[...] = ((lam_re - 1.0) * ar + lam_im * ai) / den
    zim_ref[...] = (lam_im * ar - (lam_re - 1.0) * ai) / den


def _s5_disc_call(a_re, a_im, log_dt):
    g = a_re.shape[1]
    shp = (2 * g, S5_STATE)
    ldt = jnp.broadcast_to(log_dt[..., None], (2, g, S5_STATE)).reshape(shp)
    outs = pl.pallas_call(
        _s5_disc_kernel,
        out_shape=[jax.ShapeDtypeStruct(shp, F32)] * 4,
        name="s5_discretise",
    )(a_re.reshape(shp), a_im.reshape(shp), ldt)
    return [o.reshape(2, g, S5_STATE) for o in outs]


def _s5_pack(lam_re, lam_im, z_re, z_im, b_re, b_im, c_re, c_im):
    g = lam_re.shape[1]
    nblk = g // (2 * S5_PAIRS)
    pair_rows = 2 * S5_GROUP
    bz_re = z_re[..., None] * b_re[None] - z_im[..., None] * b_im[None]
    bz_im = z_re[..., None] * b_im[None] + z_im[..., None] * b_re[None]

    def pairs(t):
        t = t.reshape(t.shape[:-3] + (g // 2, 2) + t.shape[-2:])
        return t[..., 0, :, :], t[..., 1, :, :]

    def embed(t, axis):
        t = t.reshape(t.shape[:-3] + (nblk, S5_PAIRS) + t.shape[-2:])
        out = []
        for j in range(S5_PAIRS):
            pad = [(0, 0)] * (t.ndim - 1)
            pad[axis] = (j * pair_rows, LANE - (j + 1) * pair_rows)
            out.append(jnp.pad(t[..., j, :, :], pad))
        return jnp.stack(out, axis=-3)

    br0, br1 = pairs(jnp.swapaxes(bz_re, -1, -2))
    bi0, bi1 = pairs(jnp.swapaxes(bz_im, -1, -2))
    zb = jnp.zeros_like(br0)
    bp = jnp.concatenate([jnp.concatenate([br0, zb, bi0, zb], axis=-1),
                          jnp.concatenate([zb, br1, zb, bi1], axis=-1)], axis=-2)
    bp = embed(bp, -2).astype(BF16)

    cr0, cr1 = pairs(jnp.swapaxes(c_re, -1, -2))
    ci0, ci1 = pairs(jnp.swapaxes(c_im, -1, -2))
    zc = jnp.zeros_like(cr0)
    cp = jnp.concatenate([jnp.concatenate([cr0, zc], axis=-1), jnp.concatenate([zc, cr1], axis=-1),
                          jnp.concatenate([-ci0, zc], axis=-1), jnp.concatenate([zc, -ci1], axis=-1)],
                         axis=-2)
    cp = embed(cp, -1).astype(BF16)

    def pack_l(t):
        return t.reshape(2, nblk, S5_PAIRS, LANE)

    lam = jnp.stack([pack_l(lam_re), pack_l(lam_im)], axis=-2)
    return bp, cp, lam


def _s5_pack_state(h0):
    b, _, g, _, _ = h0.shape
    nblk = g // (2 * S5_PAIRS)
    t = h0.reshape(b, 2, nblk, S5_PAIRS, 2, S5_STATE, 2)
    t = jnp.transpose(t, (1, 2, 3, 0, 6, 4, 5))
    return t.reshape(2, nblk, S5_PAIRS, b, 2 * LANE)


def _s5_unpack_state(fin, b):
    nblk = fin.shape[1]
    t = fin.reshape(2, nblk, S5_PAIRS, b, 2, 2, S5_STATE)
    t = jnp.transpose(t, (3, 0, 1, 2, 5, 6, 4))
    return t.reshape(b, 2, nblk * S5_PAIRS * 2, S5_STATE, 2)


def _cmul(ar, ai, br, bi):
    return ar * br - ai * bi, ar * bi + ai * br


def _tm_pitch(L):
    return L + SUBLANE


def _gather_tm(src_ref, dst_ref, pad_ref, L, B):
    pitch = _tm_pitch(L)
    for r in range(B):
        pad_ref[r * pitch:r * pitch + L, :] = src_ref[r * L:(r + 1) * L, :]

    def body(t, c):
        dst_ref[t] = pad_ref[pl.ds(t, B, stride=pitch), :]
        return c
    lax.fori_loop(0, L, body, 0, unroll=8)


def _scatter_tm(src_ref, pad_ref, L, B):
    pitch = _tm_pitch(L)

    def body(t, c):
        pad_ref[pl.ds(t, B, stride=pitch), :] = src_ref[t]
        return c
    lax.fori_loop(0, L, body, 0, unroll=8)


def _s5_kernel(*refs, L, B, nseg):
    if nseg > 1:
        (u_ref, bp_ref, cp_ref, lam_ref, d_ref, h0_ref, _, y_ref,
         xs_ref, yacc_ref, utm_ref, pad_ref, fl_ref, hp_ref) = refs
    else:
        u_ref, bp_ref, cp_ref, lam_ref, d_ref, y_ref, fin_ref, xs_ref, yacc_ref, utm_ref, pad_ref = refs
    nb = B // nseg
    _gather_tm(u_ref, utm_ref, pad_ref, L, B)
    u2 = utm_ref[...].reshape(L * B, LANE)
    u2b = u2.astype(BF16)
    yacc_ref[...] = jnp.zeros((L * B, LANE), F32)

    for half in range(S5_PAIRS // 2):
        chains = [(d, jj) for d in range(2) for jj in range(2)]
        for d, jj in chains:
            x = jnp.dot(u2b, bp_ref[d, 0, 2 * half + jj], preferred_element_type=F32)
            xs_ref[d, jj] = x.reshape(L, B, 2 * LANE)
        lam = {}
        for d, jj in chains:
            lr = jnp.broadcast_to(lam_ref[d, 0, 2 * half + jj, 0:1, :], (B, LANE))
            li = jnp.broadcast_to(lam_ref[d, 0, 2 * half + jj, 1:2, :], (B, LANE))
            lam[(d, jj)] = (lr, li)

        def load_x(i):
            out = []
            for d, jj in chains:
                t = i if d == 0 else L - 1 - i
                out += [xs_ref[d, jj, t, :, 0:LANE], xs_ref[d, jj, t, :, LANE:2 * LANE]]
            return out

        nc = 2 * len(chains)

        def step(i, carry):
            nxt = load_x(jnp.minimum(i + 1, L - 1))
            out = []
            for n, (d, jj) in enumerate(chains):
                t = i if d == 0 else L - 1 - i
                hr, hi = carry[2 * n], carry[2 * n + 1]
                lr, li = lam[(d, jj)]
                pr, pi = _cmul(lr, li, hr, hi)
                nr = pr + carry[nc + 2 * n]
                ni = pi + carry[nc + 2 * n + 1]
                xs_ref[d, jj, t, :, 0:LANE] = nr
                xs_ref[d, jj, t, :, LANE:2 * LANE] = ni
                out += [nr, ni]
            return tuple(out + nxt)

        zero = jnp.zeros((B, LANE), F32)
        carry = lax.fori_loop(0, L, step, (zero,) * nc + tuple(load_x(0)), unroll=4)

        if nseg == 1:
            for n, (d, jj) in enumerate(chains):
                fin_ref[d, 0, 2 * half + jj, :, 0:LANE] = carry[2 * n]
                fin_ref[d, 0, 2 * half + jj, :, LANE:2 * LANE] = carry[2 * n + 1]
        else:
            hp = {}
            for n, (d, jj) in enumerate(chains):
                plr = lam_ref[d, 0, 2 * half + jj, 0:1, :]
                pli = lam_ref[d, 0, 2 * half + jj, 1:2, :]
                for _ in range(int(math.log2(L))):
                    plr, pli = _cmul(plr, pli, plr, pli)
                fl_ref[0] = carry[2 * n]
                fl_ref[1] = carry[2 * n + 1]
                order = list(range(nseg)) if d == 0 else list(range(nseg - 1, -1, -1))
                for bb in range(nb):
                    cr = h0_ref[d, 0, 2 * half + jj, bb:bb + 1, 0:LANE]
                    ci = h0_ref[d, 0, 2 * half + jj, bb:bb + 1, LANE:2 * LANE]
                    for k, s in enumerate(order):
                        if k > 0:
                            rp = bb * nseg + order[k - 1]
                            mr, mi = _cmul(plr, pli, cr, ci)
                            cr = fl_ref[0, rp:rp + 1, :] + mr
                            ci = fl_ref[1, rp:rp + 1, :] + mi
                        r = bb * nseg + s
                        hp_ref[n, 0, r:r + 1, :] = cr
                        hp_ref[n, 1, r:r + 1, :] = ci
                hp[(d, jj)] = (hp_ref[n, 0], hp_ref[n, 1])

            def cstep(i, cw):
                nxt = load_x(jnp.minimum(i + 1, L - 1))
                out = []
                for n, (d, jj) in enumerate(chains):
                    t = i if d == 0 else L - 1 - i
                    lr, li = lam[(d, jj)]
                    nr, ni = _cmul(lr, li, cw[2 * n], cw[2 * n + 1])
                    xs_ref[d, jj, t, :, 0:LANE] = cw[nc + 2 * n] + nr
                    xs_ref[d, jj, t, :, LANE:2 * LANE] = cw[nc + 2 * n + 1] + ni
                    out += [nr, ni]
                return tuple(out + nxt)

            cw0 = []
            for d, jj in chains:
                cw0 += list(hp[(d, jj)])
            lax.fori_loop(0, L, cstep, tuple(cw0 + load_x(0)), unroll=4)

        for jj in range(2):
            hs = (xs_ref[0, jj] + xs_ref[1, jj]).reshape(L * B, 2 * LANE).astype(BF16)
            yacc_ref[...] += jnp.dot(hs, cp_ref[0, 2 * half + jj], preferred_element_type=F32)

    y = u2 * d_ref[...] + yacc_ref[...]
    utm_ref[...] = jax.nn.gelu(y).reshape(L, B, LANE)
    _scatter_tm(utm_ref, pad_ref, L, B)
    pitch = _tm_pitch(L)
    for r in range(B):
        y_ref[r * L:(r + 1) * L, :] = pad_ref[r * pitch:r * pitch + L, :]


def _s5_call(src, col0, bp, cp, lam, d_skip, h0, y_prev, *, L, B, nseg, rblk, n_rows):
    w = d_skip.shape[-1]
    nblk = w // LANE
    cblk0 = col0 // LANE
    in_specs = [
        pl.BlockSpec((L * B, LANE), lambda k: (rblk, cblk0 + k)),
        pl.BlockSpec((2, 1, S5_PAIRS, LANE, 2 * LANE), lambda k: (0, k, 0, 0, 0)),
        pl.BlockSpec((1, S5_PAIRS, 2 * LANE, LANE), lambda k: (k, 0, 0, 0)),
        pl.BlockSpec((2, 1, S5_PAIRS, 2, LANE), lambda k: (0, k, 0, 0, 0)),
        pl.BlockSpec((1, LANE), lambda k: (0, k)),
    ]
    args = [src, bp, cp, lam, d_skip.reshape(1, w)]
    y_shape = jax.ShapeDtypeStruct((n_rows, w), F32)
    y_spec = pl.BlockSpec((L * B, LANE), lambda k: (rblk, k))
    scratch = [pltpu.VMEM((2, 2, L, B, 2 * LANE), F32), pltpu.VMEM((L * B, LANE), F32),
               pltpu.VMEM((L, B, LANE), F32), pltpu.VMEM((B * _tm_pitch(L), LANE), F32)]
    aliases = {}
    if nseg > 1:
        nb = B // nseg
        in_specs += [pl.BlockSpec((2, 1, S5_PAIRS, nb, 2 * LANE), lambda k: (0, k, 0, 0, 0)),
                     pl.BlockSpec(memory_space=pl.ANY)]
        args += [h0, y_prev]
        aliases = {len(args) - 1: 0}
        out_shape, out_specs = y_shape, y_spec
        scratch += [pltpu.VMEM((2, B, LANE), F32), pltpu.VMEM((4, 2, B, LANE), F32)]
    else:
        out_shape = [y_shape, jax.ShapeDtypeStruct((2, nblk, S5_PAIRS, B, 2 * LANE), F32)]
        out_specs = [y_spec, pl.BlockSpec((2, 1, S5_PAIRS, B, 2 * LANE), lambda k: (0, k, 0, 0, 0))]
    return pl.pallas_call(
        functools.partial(_s5_kernel, L=L, B=B, nseg=nseg),
        out_shape=out_shape,
        grid=(nblk,),
        in_specs=in_specs,
        out_specs=out_specs,
        scratch_shapes=scratch,
        input_output_aliases=aliases,
        compiler_params=_cparams("arbitrary"),
        name="s5_scan_seg" if nseg > 1 else "s5_scan",
    )(*args)


def _glu_kernel(yrow_ref, w_ref, b_ref, g_ref, o_ref):
    w = w_ref[...].astype(BF16)
    z = jnp.dot(yrow_ref[...].astype(BF16), w, preferred_element_type=F32) + b_ref[...]
    col0 = pl.multiple_of(pl.program_id(1) * TN, TN)
    ytile = yrow_ref[:, pl.ds(col0, TN)]
    o_ref[...] = (ytile * _sigmoid(z) * _silu(g_ref[...])).astype(BF16)


def _glu_call(y, w_glu, b_glu3, proj, l):
    goff = C_GA // TN
    return pl.pallas_call(
        _glu_kernel,
        out_shape=jax.ShapeDtypeStruct((N_TOK, W_BR), BF16),
        grid=(N_TOK // TM, W_BR // TN),
        in_specs=[
            pl.BlockSpec((TM, W_BR), lambda i, j: (i, 0)),
            pl.BlockSpec((None, W_BR, TN), lambda i, j: (l, 0, j)),
            pl.BlockSpec((None, 1, TN), lambda i, j: (l, 0, j)),
            pl.BlockSpec((TM, TN), lambda i, j: (i, goff + j)),
        ],
        out_specs=pl.BlockSpec((TM, TN), lambda i, j: (i, j)),
        compiler_params=_cparams("arbitrary", "arbitrary"),
        name="s5_glu",
    )(y, w_glu, b_glu3, proj)


def _expm1(x):
    t = jnp.tanh(0.5 * x)
    return 2.0 * t / (1.0 - t)


def _lru_kernel(*refs, L, B, nseg, period):
    if nseg > 1:
        (x_ref, g_ref, cw_ref, cb_ref, wa_ref, ba_ref, wx_ref, bx_ref, sp_ref, h0_ref, _,
         y_ref, a_ref, b_ref, ytok_ref, fl_ref, hp_ref) = refs
    else:
        (x_ref, g_ref, cw_ref, cb_ref, wa_ref, ba_ref, wx_ref, bx_ref, sp_ref,
         y_ref, fin_ref, a_ref, b_ref, ytok_ref) = refs
    nb = B // nseg
    _gather_tm(x_ref, a_ref.at[0], ytok_ref, L, B)
    x = a_ref[0]
    pos = lax.broadcasted_iota(jnp.int32, (L, B, LANE), 0) % period
    xc = jnp.broadcast_to(cb_ref[...].reshape(1, 1, LANE), (L, B, LANE))
    for k in range(CONV_W):
        off = k - CONV_W // 2
        if off < 0:
            xs = jnp.concatenate([jnp.zeros((-off, B, LANE), F32), x[:L + off]], axis=0)
        elif off > 0:
            xs = jnp.concatenate([x[off:], jnp.zeros((off, B, LANE), F32)], axis=0)
        else:
            xs = x
        if period < L:
            xs = jnp.where((pos + off >= 0) & (pos + off < period), xs, 0.0)
        xc = xc + cw_ref[k:k + 1, :].reshape(1, 1, LANE) * xs
    xc2 = xc.reshape(L * B, LANE)
    xcb = xc2.astype(BF16)
    for d in range(2):
        r = _sigmoid(jnp.dot(xcb, wa_ref[d, 0].astype(BF16), preferred_element_type=F32)
                     + ba_ref[d:d + 1, :])
        ig = _sigmoid(jnp.dot(xcb, wx_ref[d, 0].astype(BF16), preferred_element_type=F32)
                      + bx_ref[d:d + 1, :])
        log_a = (-LRU_C) * r * sp_ref[d:d + 1, :]
        a_ref[d] = jnp.exp(log_a).reshape(L, B, LANE)
        m2 = -_expm1(2.0 * log_a)
        mult = jnp.where(m2 > 0.0, m2 * lax.rsqrt(m2), 0.0)
        b_ref[d] = (mult * (ig * xc2)).reshape(L, B, LANE)

    def load_ab(i):
        out = []
        for d in range(2):
            t = i if d == 0 else L - 1 - i
            out += [a_ref[d, t], b_ref[d, t]]
        return out

    def step(i, carry):
        nxt = load_ab(jnp.minimum(i + 1, L - 1))
        out = []
        for d in range(2):
            t = i if d == 0 else L - 1 - i
            a = carry[4 + 2 * d]
            h = a * carry[2 * d] + carry[4 + 2 * d + 1]
            b_ref[d, t] = h
            if nseg > 1:
                p = a * carry[2 * d + 1]
                a_ref[d, t] = p
            else:
                p = carry[2 * d + 1]
            out += [h, p]
        return tuple(out + nxt)

    zero = jnp.zeros((B, LANE), F32)
    one = jnp.ones((B, LANE), F32)
    carry = lax.fori_loop(0, L, step, (zero, one, zero, one) + tuple(load_ab(0)), unroll=8)

    if nseg == 1:
        fin_ref[0] = carry[0]
        fin_ref[1] = carry[2]
    else:
        hps = []
        for d in range(2):
            fl_ref[0] = carry[2 * d]
            fl_ref[1] = carry[2 * d + 1]
            order = list(range(nseg)) if d == 0 else list(range(nseg - 1, -1, -1))
            for bb in range(nb):
                c = h0_ref[d, bb:bb + 1, :]
                for k, s in enumerate(order):
                    if k > 0:
                        rp = bb * nseg + order[k - 1]
                        c = fl_ref[0, rp:rp + 1, :] + fl_ref[1, rp:rp + 1, :] * c
                    r = bb * nseg + s
                    hp_ref[d, r:r + 1, :] = c
            hps.append(hp_ref[d])

        def cstep(i, c):
            nxt = load_ab(jnp.minimum(i + 1, L - 1))
            for d in range(2):
                t = i if d == 0 else L - 1 - i
                b_ref[d, t] = c[2 * d + 1] + c[2 * d] * hps[d]
            return tuple(nxt)

        lax.fori_loop(0, L, cstep, tuple(load_ab(0)), unroll=4)

    a_ref[0] = b_ref[0] + b_ref[1]
    _scatter_tm(a_ref.at[0], ytok_ref, L, B)
    pitch = _tm_pitch(L)
    for r in range(B):
        rows = slice(r * L, (r + 1) * L)
        y_ref[rows, :] = (ytok_ref[r * pitch:r * pitch + L, :] * _silu(g_ref[rows, :])).astype(BF16)


def _lru_call(src, col_x, col_g, conv_w, conv_b, w_a, b_a, w_x, b_x, sp, h0, y_prev,
              *, L, B, nseg, period, rblk, n_rows):
    w = conv_b.shape[-1]
    nblk = w // LANE

    def rows(col0):
        return pl.BlockSpec((L * B, LANE), lambda k: (rblk, col0 // LANE + k))

    y_spec = pl.BlockSpec((L * B, LANE), lambda k: (rblk, k))
    in_specs = [
        rows(col_x), rows(col_g),
        pl.BlockSpec((CONV_W, LANE), lambda k: (0, k)),
        pl.BlockSpec((1, LANE), lambda k: (0, k)),
        pl.BlockSpec((2, 1, LANE, LANE), lambda k: (0, k, 0, 0)),
        pl.BlockSpec((2, LANE), lambda k: (0, k)),
        pl.BlockSpec((2, 1, LANE, LANE), lambda k: (0, k, 0, 0)),
        pl.BlockSpec((2, LANE), lambda k: (0, k)),
        pl.BlockSpec((2, LANE), lambda k: (0, k)),
    ]
    args = [src, src, conv_w, conv_b.reshape(1, w), w_a, b_a, w_x, b_x, sp]
    y_shape = jax.ShapeDtypeStruct((n_rows, w), BF16)
    scratch = [pltpu.VMEM((2, L, B, LANE), F32), pltpu.VMEM((2, L, B, LANE), F32),
               pltpu.VMEM((B * _tm_pitch(L), LANE), F32)]
    aliases = {}
    if nseg > 1:
        nb = B // nseg
        in_specs += [pl.BlockSpec((2, nb, LANE), lambda k: (0, 0, k)), pl.BlockSpec(memory_space=pl.ANY)]
        args += [h0, y_prev]
        aliases = {len(args) - 1: 0}
        out_shape, out_specs = y_shape, y_spec
        scratch += [pltpu.VMEM((2, B, LANE), F32), pltpu.VMEM((2, B, LANE), F32)]
    else:
        out_shape = [y_shape, jax.ShapeDtypeStruct((2, B, w), F32)]
        out_specs = [y_spec, pl.BlockSpec((2, B, LANE), lambda k: (0, 0, k))]
    return pl.pallas_call(
        functools.partial(_lru_kernel, L=L, B=B, nseg=nseg, period=period),
        out_shape=out_shape,
        grid=(nblk,),
        in_specs=in_specs,
        out_specs=out_specs,
        scratch_shapes=scratch,
        input_output_aliases=aliases,
        compiler_params=_cparams("arbitrary"),
        name="rglru_seg" if nseg > 1 else "rglru",
    )(*args)


def _nt_dot(a, b):
    return lax.dot_general(a, b, (((1,), (1,)), ((), ())), preferred_element_type=F32)


def _hgrn_kernel(*refs, L, has_h0, n_aliased):
    q_ref, ff_ref, fb_ref, v_ref, g_ref, lb_ref, nw_ref = refs[:7]
    h0_ref = refs[7] if has_h0 else None
    y_ref, fin_ref, of_ref = refs[7 + has_h0 + n_aliased:]
    rb_rows = HG_RB
    nrb = L // rb_rows
    nsub = HG_ST // HG_CHUNK
    nch = rb_rows // HG_ST
    nscb = rb_rows // HG_SCB
    row = lax.broadcasted_iota(jnp.int32, (HG_SCB, HG_SCB), 0)
    col = lax.broadcasted_iota(jnp.int32, (HG_SCB, HG_SCB), 1)
    sub_dist = jnp.where((row // HG_ST) == (col // HG_ST), row // HG_CHUNK - col // HG_CHUNK, 2 * nsub)
    ridx = lax.broadcasted_iota(jnp.int32, (rb_rows, HG_DK), 0)
    pos = ridx % HG_CHUNK
    sic = (ridx // HG_CHUNK) % nsub
    chunk_id = ridx // HG_ST

    def sub_bcast(x, idx):
        x3 = x.reshape(rb_rows // HG_CHUNK, HG_CHUNK, HG_DK)
        return jnp.broadcast_to(x3[:, idx:idx + 1, :], x3.shape).reshape(rb_rows, HG_DK)

    def chunk_bcast(x, idx):
        x3 = x.reshape(nch, HG_ST, HG_DK)
        return jnp.broadcast_to(x3[:, idx:idx + 1, :], x3.shape).reshape(rb_rows, HG_DK)

    shared = {}

    def load_qv(rows, cs, key):
        if key is not None and key in shared:
            return shared[key]
        v = v_ref[rows, cs]
        out = (_silu(q_ref[rows, cs]), v.astype(BF16), v.T.astype(BF16))
        if key is not None:
            shared[key] = out
        return out

    def run_block(rb, st, d, hh):
        cs = slice(hh * HG_DK, (hh + 1) * HG_DK)
        if isinstance(rb, int):
            rows = slice(rb * rb_rows, (rb + 1) * rb_rows)
        else:
            rows = pl.ds(pl.multiple_of(rb * rb_rows, rb_rows), rb_rows)
        qf, vb, vt = load_qv(rows, cs, (hh, rb) if isinstance(rb, int) else None)
        f = (ff_ref if d == 0 else fb_ref)[rows, cs]
        lbd = lb_ref[d:d + 1, cs]
        g = lbd + (1.0 - lbd) * jax.nn.sigmoid(f)
        kk = 1.0 - g
        b = jnp.log2(g)
        sgn = 1 if d == 0 else -1

        def shift(x, n):
            return pltpu.roll(x, n if d == 0 else rb_rows - n, 0)

        s = 1
        while s < HG_CHUNK:
            keep = (pos >= s) if d == 0 else (pos <= HG_CHUNK - 1 - s)
            b = b + jnp.where(keep, shift(b, s), 0.0)
            s *= 2
        tot = sub_bcast(b, HG_CHUNK - 1 if d == 0 else 0)
        prev = [shift(tot, HG_CHUNK * n) for n in range(1, nsub)]
        seen = sic if d == 0 else nsub - 1 - sic
        base = sum(jnp.where(seen >= n, prev[n - 1], 0.0) for n in range(1, nsub))
        ctot = chunk_bcast(b + base, HG_ST - 1 if d == 0 else 0)

        ed = jnp.exp2(b - sub_bcast(b, HG_CHUNK // 2))
        qt = (qf * ed).astype(BF16)
        kt = (kk / ed).astype(BF16)
        qe = qf * jnp.exp2(b)
        kd = kk * jnp.exp2(tot - b)
        lhs = [qe.astype(BF16)]
        acc = None
        for n in range(1, nsub - 1):
            acc = prev[n - 1] if acc is None else acc + prev[n - 1]
            lhs.append((qe * jnp.exp2(acc)).astype(BF16))
        kdb = kd.astype(BF16)
        causal = (col <= row) if d == 0 else (col >= row)
        o_parts = []
        for hb in range(nscb):
            rs = slice(hb * HG_SCB, (hb + 1) * HG_SCB)
            sc = jnp.where((sub_dist == 0) & causal, _nt_dot(qt[rs], kt[rs]), 0.0)
            scn = _nt_dot(jnp.concatenate([x[rs] for x in lhs], axis=0), kdb[rs])
            for n in range(1, nsub):
                sc = jnp.where(sub_dist == sgn * n, scn[(n - 1) * HG_SCB:n * HG_SCB], sc)
            o_parts.append(jnp.dot(sc.astype(BF16), vb[rs], preferred_element_type=F32))
        o = jnp.concatenate(o_parts, axis=0)

        qe_st = (qe * jnp.exp2(base)).astype(BF16)
        kd_st = kd * jnp.exp2(ctot - base - tot)
        uts = []
        for c in range(0, nch, 2):
            rhs = jnp.concatenate([jnp.where(chunk_id == c, kd_st, 0.0),
                                   jnp.where(chunk_id == c + 1, kd_st, 0.0)], axis=1).astype(BF16)
            ut2 = jnp.dot(vt, rhs, preferred_element_type=F32)
            uts += [ut2[:, :HG_DK], ut2[:, HG_DK:]]
        order = range(nch) if d == 0 else range(nch - 1, -1, -1)
        entering = [None] * nch
        for c in order:
            entering[c] = st
            tot_row = c * HG_ST + (HG_ST - 1 if d == 0 else 0)
            st = st * jnp.exp2(ctot[tot_row:tot_row + 1, :]) + uts[c]
        outs = []
        for c in range(nch):
            lo = c * HG_ST
            oi = _nt_dot(qe_st[lo:lo + HG_ST], entering[c].astype(BF16))
            outs.append(o[lo:lo + HG_ST] + oi)
        return jnp.concatenate(outs, axis=0), st

    streams = [(hh, d) for hh in range(HG_HPB) for d in range(2)]

    def body(i, sts):
        new = []
        for (hh, d), st in zip(streams, sts):
            rb = i if d == 0 else nrb - 1 - i
            oblk, st = run_block(rb, st, d, hh)
            cs = slice(hh * HG_DK, (hh + 1) * HG_DK)
            if isinstance(rb, int):
                of_ref[d, rb * rb_rows:(rb + 1) * rb_rows, cs] = oblk
            else:
                of_ref[d, pl.ds(pl.multiple_of(rb * rb_rows, rb_rows), rb_rows), cs] = oblk
            new.append(st)
        return tuple(new)

    if has_h0:
        sts = tuple(h0_ref[0, d, hh].T for hh, d in streams)
    else:
        sts = tuple(jnp.zeros((HG_DK, HG_DK), F32) for _ in streams)
    sts = body(0, sts) if nrb == 1 else lax.fori_loop(0, nrb, body, sts)
    for (hh, d), st in zip(streams, sts):
        fin_ref[0, d, hh] = st.T

    for hh in range(HG_HPB):
        cs = slice(hh * HG_DK, (hh + 1) * HG_DK)
        o = of_ref[0, :, cs] + of_ref[1, :, cs]
        o = o * lax.rsqrt(jnp.mean(o * o, axis=-1, keepdims=True) + RMS_EPS)
        y_ref[:, cs] = (o * nw_ref[:, cs] * _silu(g_ref[:, cs])).astype(BF16)


def _hgrn_call(proj, lb, norm_w, h0, y_prev, fin_prev, *, L, nb, row0, n_rows, layer,
               cols=(C_QC, C_FF, C_FB, C_IC, C_GC)):
    w = lb.shape[-1]
    nh = w // HG_DK
    rblk0 = row0 // L
    hw = HG_HPB * HG_DK

    def col(off):
        return pl.BlockSpec((L, hw), lambda b, h: (rblk0 + b, off // hw + h))

    in_specs = [col(c) for c in cols] + [
                pl.BlockSpec((2, hw), lambda b, h: (0, h)),
                pl.BlockSpec((1, hw), lambda b, h: (0, h))]
    args = [proj] * 5 + [lb, norm_w.reshape(1, w)]
    if h0 is not None:
        in_specs.append(pl.BlockSpec((1, 2, HG_HPB, HG_DK, HG_DK), lambda b, h: (b, 0, h, 0, 0)))
        args.append(h0)
    aliases = {}
    for out_idx, prev in enumerate((y_prev, fin_prev)):
        if prev is not None:
            in_specs.append(pl.BlockSpec(memory_space=pl.ANY))
            args.append(prev)
            aliases[len(args) - 1] = out_idx
    st_spec = pl.BlockSpec((1, None, 2, HG_HPB, HG_DK, HG_DK), lambda b, h: (b, layer, 0, h, 0, 0))
    return pl.pallas_call(
        functools.partial(_hgrn_kernel, L=L, has_h0=h0 is not None, n_aliased=len(aliases)),
        out_shape=[jax.ShapeDtypeStruct((n_rows, w), BF16),
                   jax.ShapeDtypeStruct((nb, DEPTH, 2, nh, HG_DK, HG_DK), F32)],
        grid=(nb, nh // HG_HPB),
        in_specs=in_specs,
        out_specs=[pl.BlockSpec((L, hw), lambda b, h: (rblk0 + b, h)), st_spec],
        scratch_shapes=[pltpu.VMEM((2, L, hw), F32)],
        input_output_aliases=aliases,
        compiler_params=_cparams("arbitrary", "arbitrary"),
        name="hgrn2",
    )(*args)


def _merge_kernel(ya_ref, yb_ref, yc_ref, wa_ref, wb_ref, wc_ref, ma_ref, mb_ref, mc_ref, o_ref):
    acc = None
    for y_ref, w_ref, m_ref in ((ya_ref, wa_ref, ma_ref), (yb_ref, wb_ref, mb_ref), (yc_ref, wc_ref, mc_ref)):
        t = jnp.dot(y_ref[...], w_ref[...].astype(BF16), preferred_element_type=F32)
        t = _sigmoid(m_ref[...]) * t
        acc = t if acc is None else acc + t
    o_ref[...] = acc.astype(BF16)


def _merge_call(ya, yb, yc, w_br, proj, l):
    tm, tn = TM_MERGE, TN_MERGE
    moff = C_MA // tn
    mstep = D_MODEL // tn
    lhs = pl.BlockSpec((tm, W_BR), lambda i, j: (i, 0))

    def wspec(br):
        return pl.BlockSpec((None, None, W_BR, tn), lambda i, j: (l, br, 0, j))

    def mspec(br):
        return pl.BlockSpec((tm, tn), lambda i, j: (i, moff + br * mstep + j))

    return pl.pallas_call(
        _merge_kernel,
        out_shape=jax.ShapeDtypeStruct((N_TOK, D_MODEL), BF16),
        grid=(N_TOK // tm, D_MODEL // tn),
        in_specs=[lhs, lhs, lhs, wspec(0), wspec(1), wspec(2), mspec(0), mspec(1), mspec(2)],
        out_specs=pl.BlockSpec((tm, tn), lambda i, j: (i, j)),
        compiler_params=_cparams("arbitrary", "arbitrary"),
        name="branch_merge",
    )(ya, yb, yc, w_br, w_br, w_br, proj, proj, proj)


def _outproj_kernel(m_ref, w_ref, b_ref, xc_ref, xl_ref, gate_ref, o_ref):
    out = jnp.dot(m_ref[...], w_ref[...].astype(BF16), preferred_element_type=F32) + b_ref[...]
    gated = gate_ref[0] * out
    is_ctx = pl.program_id(0) < N_CTX // TM

    @pl.when(is_ctx)
    def _():
        o_ref[...] = DN_ALPHA * xc_ref[...] + gated

    @pl.when(jnp.logical_not(is_ctx))
    def _():
        o_ref[...] = DN_ALPHA * xl_ref[...] + gated


def _outproj_call(merged, w_out, b_out3, x_ctx, x_lat, mods3, l):
    goff = 2 * D_MODEL // TN
    ctx_map, lat_map = _split_rows(TM, 1)
    return pl.pallas_call(
        _outproj_kernel,
        out_shape=jax.ShapeDtypeStruct((N_TOK, D_MODEL), F32),
        grid=(N_TOK // TM, D_MODEL // TN),
        in_specs=[
            pl.BlockSpec((TM, D_MODEL), lambda i, j: (i, 0)),
            pl.BlockSpec((None, D_MODEL, TN), lambda i, j: (l, 0, j)),
            pl.BlockSpec((None, 1, TN), lambda i, j: (l, 0, j)),
            pl.BlockSpec((TM, TN), ctx_map),
            pl.BlockSpec((TM, TN), lat_map),
            pl.BlockSpec((1, 1, TN), lambda i, j: (_mod_row_mm(i), 0, goff + j)),
        ],
        out_specs=pl.BlockSpec((TM, TN), lambda i, j: (i, j)),
        compiler_params=_cparams("arbitrary", "arbitrary"),
        name="out_proj",
    )(merged, w_out, b_out3, x_ctx, x_lat, mods3)


def _ln_affine_kernel(r_ref, g_ref, b_ref, o_ref):
    r = r_ref[...]
    mu = jnp.mean(r, axis=-1, keepdims=True)
    rc = r - mu
    var = jnp.mean(rc * rc, axis=-1, keepdims=True)
    o_ref[...] = rc * lax.rsqrt(var + LN_EPS) * g_ref[...] + b_ref[...]


def _ln_affine_call(r, g, b, l, row0=0, n_rows=N_TOK):
    rblk0 = row0 // TM_LN
    return pl.pallas_call(
        _ln_affine_kernel,
        out_shape=jax.ShapeDtypeStruct((n_rows, D_MODEL), F32),
        grid=(n_rows // TM_LN,),
        in_specs=[
            pl.BlockSpec((TM_LN, D_MODEL), lambda i: (rblk0 + i, 0)),
            pl.BlockSpec((None, 1, D_MODEL), lambda i: (l, 0, 0)),
            pl.BlockSpec((None, 1, D_MODEL), lambda i: (l, 0, 0)),
        ],
        out_specs=pl.BlockSpec((TM_LN, D_MODEL), lambda i: (i, 0)),
        compiler_params=_cparams("arbitrary"),
        name="post_ln",
    )(r, g, b)


def kernel(x_prompt, x_sample, state_s5, state_lru, state_hgrn, c, c_ctx, w_ada, b_ada, w_in, b_in,
           s5_a_re, s5_a_im, s5_log_dt, s5_b_re, s5_b_im, s5_c_re, s5_c_im, s5_d, s5_w_glu, s5_b_glu,
           lru_conv_w, lru_conv_b, lru_w_a, lru_b_a, lru_w_x, lru_b_x, lru_lambda, hg_lb, hg_norm_w,
           w_br, w_out, b_out, ln_g, ln_b):
    lb_soft = jax.nn.softmax(hg_lb.astype(F32), axis=0)
    lb_all = jnp.cumsum(lb_soft, axis=0) - lb_soft[0]
    softplus_neg_lam = jax.nn.softplus(-lru_lambda.astype(F32))

    cvec = jnp.zeros((SUBLANE, D_MODEL), F32)
    cvec = cvec.at[0].set(c_ctx).at[1:1 + DEC_BATCH].set(c)
    mods = _mods_call(cvec, w_ada, b_ada)

    b_in3 = b_in.reshape(DEPTH, 1, IN_COLS)
    b_glu3 = s5_b_glu.reshape(DEPTH, 1, W_BR)
    b_out3 = b_out.reshape(DEPTH, 1, D_MODEL)
    ln_g3 = ln_g.reshape(DEPTH, 1, D_MODEL)
    ln_b3 = ln_b.reshape(DEPTH, 1, D_MODEL)

    x_ctx = x_prompt.reshape(N_CTX, D_MODEL)
    x_lat = x_sample.reshape(N_LAT, D_MODEL)
    st_s5, st_lru, st_hg = [], [], None
    for l in range(DEPTH):
        mods3 = mods[l, :1 + DEC_BATCH].reshape(1 + DEC_BATCH, 1, 3 * D_MODEL)
        h = _ln_mod_call(x_ctx, x_lat, mods3)
        proj = _inproj_call(h, w_in, b_in3, l)

        lam_re, lam_im, z_re, z_im = _s5_disc_call(s5_a_re[l], s5_a_im[l], s5_log_dt[l])
        bp, cp, lam = _s5_pack(lam_re, lam_im, z_re, z_im, s5_b_re[l], s5_b_im[l], s5_c_re[l], s5_c_im[l])
        ctx_rows = dict(L=SEQ, B=BATCH, nseg=1, rblk=0, n_rows=N_TOK)
        lat_rows = dict(L=LAT_LEN, B=LAT_ROWS, nseg=LAT_SEG, rblk=N_CTX // N_LAT, n_rows=N_TOK)
        y_pre, fin_s5 = _s5_call(proj, C_UA, bp, cp, lam, s5_d[l], None, None, **ctx_rows)
        y_pre = _s5_call(proj, C_UA, bp, cp, lam, s5_d[l], _s5_pack_state(state_s5[:, l]), y_pre, **lat_rows)
        y_a = _glu_call(y_pre, s5_w_glu, b_glu3, proj, l)
        st_s5.append(_s5_unpack_state(fin_s5, BATCH))

        lru_args = (lru_conv_w[l], lru_conv_b[l], lru_w_a[l], lru_b_a[l], lru_w_x[l], lru_b_x[l],
                    softplus_neg_lam[l])
        y_b, fin_lru = _lru_call(proj, C_XB, C_GB, *lru_args, None, None, period=SEQ, **ctx_rows)
        y_b = _lru_call(proj, C_XB, C_GB, *lru_args, jnp.transpose(state_lru[:, l], (1, 0, 2)), y_b,
                        period=GRID_W, **lat_rows)
        st_lru.append(jnp.transpose(fin_lru, (1, 0, 2)))

        y_c, st_hg = _hgrn_call(proj, lb_all[l], hg_norm_w[l], None, None, st_hg,
                                L=SEQ, nb=BATCH, row0=0, n_rows=N_TOK, layer=l)
        y_c, _ = _hgrn_call(proj, lb_all[l], hg_norm_w[l], state_hgrn[:, l], y_c, None,
                            L=DEC_SEQ, nb=DEC_BATCH, row0=N_CTX, n_rows=N_TOK, layer=0)

        merged = _merge_call(y_a, y_b, y_c, w_br, proj, l)
        r = _outproj_call(merged, w_out, b_out3, x_ctx, x_lat, mods3, l)
        x_ctx = _ln_affine_call(r, ln_g3, ln_b3, l, 0, N_CTX)
        x_lat = _ln_affine_call(r, ln_g3, ln_b3, l, N_CTX, N_LAT)

    y_prompt = x_ctx.reshape(BATCH, SEQ, D_MODEL)
    y_sample = x_lat.reshape(DEC_BATCH, DEC_SEQ, D_MODEL)
    new_state_s5 = jnp.stack(st_s5, axis=1)
    new_state_lru = jnp.stack(st_lru, axis=1)
    new_state_hgrn = st_hg
    return (y_prompt, y_sample, new_state_s5, new_state_lru, new_state_hgrn)
```

```python
import functools
import math

import jax
import jax.numpy as jnp
from jax import lax
from jax.experimental import pallas as pl
from jax.experimental.pallas import tpu as pltpu

F32 = jnp.float32
BF16 = jnp.bfloat16

LANE = 128
SUBLANE = 8
VMEM_LIMIT = 56 * 1024 * 1024

D_MODEL = 4096
DEPTH = 2
BATCH, SEQ = 16, 256
DEC_BATCH, DEC_SEQ = 2, 1024
GRID_W = 64
W_BR = D_MODEL // 2
S5_GROUP = 16
S5_STATE = 64
LRU_C = 8.0
CONV_W = 4
HG_DK = 128
HG_CHUNK = 16
N_BRANCH = 3
IN_COLS = 9 * W_BR + N_BRANCH * D_MODEL
DN_ALPHA = (2 * DEPTH) ** 0.25
LN_EPS = 1e-5
RMS_EPS = 1e-6

N_CTX = BATCH * SEQ
N_LAT = DEC_BATCH * DEC_SEQ
N_TOK = N_CTX + N_LAT
LAT_SEG = 8
LAT_LEN = DEC_SEQ // LAT_SEG
LAT_ROWS = LAT_SEG * DEC_BATCH

C_UA, C_GA, C_XB, C_GB, C_QC, C_FF, C_FB, C_IC, C_GC = (i * W_BR for i in range(9))
C_MA = 9 * W_BR

TM = 1024
TN = 512
TM_IN = 2048
TM_MERGE = 1024
TN_MERGE = 256
TM_LN = 512
HG_RB = 256
HG_ST = 64
HG_HPB = 4
HG_SCB = 128
S5_PAIRS = 4


def _cparams(*sem):
    return pltpu.CompilerParams(dimension_semantics=sem, vmem_limit_bytes=VMEM_LIMIT)


def _sigmoid(x):
    return 0.5 * jnp.tanh(0.5 * x) + 0.5


def _silu(x):
    return x * _sigmoid(x)


def _mods_kernel(c_ref, w_ref, b_ref, o_ref):
    c = c_ref[...]
    s = _silu(c).astype(BF16)
    w = w_ref[...].astype(BF16)
    o_ref[...] = jnp.dot(s, w, preferred_element_type=F32) + b_ref[...]


def _mods_call(cvec, w_ada, b_ada):
    tn = TN
    return pl.pallas_call(
        _mods_kernel,
        out_shape=jax.ShapeDtypeStruct((DEPTH, SUBLANE, 3 * D_MODEL), F32),
        grid=(DEPTH, 3 * D_MODEL // tn),
        in_specs=[
            pl.BlockSpec((SUBLANE, D_MODEL), lambda l, j: (0, 0)),
            pl.BlockSpec((None, D_MODEL, tn), lambda l, j: (l, 0, j)),
            pl.BlockSpec((None, 1, tn), lambda l, j: (l, 0, j)),
        ],
        out_specs=pl.BlockSpec((None, SUBLANE, tn), lambda l, j: (l, 0, j)),
        compiler_params=_cparams("arbitrary", "arbitrary"),
        name="adaln_mods",
    )(cvec, w_ada, b_ada.reshape(DEPTH, 1, 3 * D_MODEL))


def _mod_row_ln(i):
    n_ctx_tiles = N_CTX // TM_LN
    return jnp.where(i < n_ctx_tiles, 0, 1 + (i - n_ctx_tiles) // (DEC_SEQ // TM_LN))


def _mod_row_mm(i):
    return jnp.maximum(i - (N_CTX // TM - 1), 0)


def _ln_mod_kernel(xc_ref, xl_ref, shift_ref, scale_ref, o_ref):
    def emit(x_ref):
        x = x_ref[...]
        mu = jnp.mean(x, axis=-1, keepdims=True)
        xc = x - mu
        var = jnp.mean(xc * xc, axis=-1, keepdims=True)
        h = xc * lax.rsqrt(var + LN_EPS) * (1.0 + scale_ref[0]) + shift_ref[0]
        o_ref[...] = h.astype(BF16)

    is_ctx = pl.program_id(0) < N_CTX // TM_LN
    pl.when(is_ctx)(lambda: emit(xc_ref))
    pl.when(jnp.logical_not(is_ctx))(lambda: emit(xl_ref))


def _split_rows(tile, n_col_axes):
    n_ctx_tiles = N_CTX // tile
    if n_col_axes == 0:
        return (lambda i: (jnp.minimum(i, n_ctx_tiles - 1), 0),
                lambda i: (jnp.maximum(i - n_ctx_tiles, 0), 0))
    return (lambda i, j: (jnp.minimum(i, n_ctx_tiles - 1), jnp.where(i < n_ctx_tiles, j, 0)),
            lambda i, j: (jnp.maximum(i - n_ctx_tiles, 0), jnp.where(i < n_ctx_tiles, 0, j)))


def _ln_mod_call(x_ctx, x_lat, mods3):
    ctx_map, lat_map = _split_rows(TM_LN, 0)
    return pl.pallas_call(
        _ln_mod_kernel,
        out_shape=jax.ShapeDtypeStruct((N_TOK, D_MODEL), BF16),
        grid=(N_TOK // TM_LN,),
        in_specs=[
            pl.BlockSpec((TM_LN, D_MODEL), ctx_map),
            pl.BlockSpec((TM_LN, D_MODEL), lat_map),
            pl.BlockSpec((1, 1, D_MODEL), lambda i: (_mod_row_ln(i), 0, 0)),
            pl.BlockSpec((1, 1, D_MODEL), lambda i: (_mod_row_ln(i), 0, 1)),
        ],
        out_specs=pl.BlockSpec((TM_LN, D_MODEL), lambda i: (i, 0)),
        compiler_params=_cparams("arbitrary"),
        name="ln_modulate",
    )(x_ctx, x_lat, mods3, mods3)


def _inproj_kernel(h_ref, w_ref, b_ref, o_ref):
    w = w_ref[...].astype(BF16)
    o_ref[...] = jnp.dot(h_ref[...], w, preferred_element_type=F32) + b_ref[...]


def _inproj_call(h, w_in, b_in3, l):
    return pl.pallas_call(
        _inproj_kernel,
        out_shape=jax.ShapeDtypeStruct((N_TOK, IN_COLS), F32),
        grid=(N_TOK // TM_IN, IN_COLS // TN),
        in_specs=[
            pl.BlockSpec((TM_IN, D_MODEL), lambda i, j: (i, 0), pipeline_mode=pl.Buffered(1)),
            pl.BlockSpec((None, D_MODEL, TN), lambda i, j: (l, 0, j)),
            pl.BlockSpec((None, 1, TN), lambda i, j: (l, 0, j)),
        ],
        out_specs=pl.BlockSpec((TM_IN, TN), lambda i, j: (i, j)),
        compiler_params=_cparams("arbitrary", "arbitrary"),
        name="in_proj",
    )(h, w_in, b_in3)


def _s5_disc_kernel(are_ref, aim_ref, ldt_ref, lre_ref, lim_ref, zre_ref, zim_ref):
    ar = jnp.minimum(are_ref[...], -1e-4)
    ai = aim_ref[...]
    dt = jnp.exp(ldt_ref[...])
    mag = jnp.exp(dt * ar)
    lam_re = mag * jnp.cos(dt * ai)
    lam_im = mag * jnp.sin(dt * ai)
    den = ar * ar + ai * ai
    lre_ref[...] = lam_re
    lim_ref[...] = lam_im
    zre_ref[...] = ((lam_re - 1.0) * ar + lam_im * ai) / den
    zim_ref[...] = (lam_im * ar - (lam_re - 1.0) * ai) / den


def _s5_disc_call(a_re, a_im, log_dt):
    g = a_re.shape[1]
    shp = (2 * g, S5_STATE)
    ldt = jnp.broadcast_to(log_dt[..., None], (2, g, S5_STATE)).reshape(shp)
    outs = pl.pallas_call(
        _s5_disc_kernel,
        out_shape=[jax.ShapeDtypeStruct(shp, F32)] * 4,
        name="s5_discretise",
    )(a_re.reshape(shp), a_im.reshape(shp), ldt)
    return [o.reshape(2, g, S5_STATE) for o in outs]


def _s5_pack(lam_re, lam_im, z_re, z_im, b_re, b_im, c_re, c_im):
    g = lam_re.shape[1]
    nblk = g // (2 * S5_PAIRS)
    pair_rows = 2 * S5_GROUP
    bz_re = z_re[..., None] * b_re[None] - z_im[..., None] * b_im[None]
    bz_im = z_re[..., None] * b_im[None] + z_im[..., None] * b_re[None]

    def pairs(t):
        t = t.reshape(t.shape[:-3] + (g // 2, 2) + t.shape[-2:])
        return t[..., 0, :, :], t[..., 1, :, :]

    def embed(t, axis):
        t = t.reshape(t.shape[:-3] + (nblk, S5_PAIRS) + t.shape[-2:])
        out = []
        for j in range(S5_PAIRS):
            pad = [(0, 0)] * (t.ndim - 1)
            pad[axis] = (j * pair_rows, LANE - (j + 1) * pair_rows)
            out.append(jnp.pad(t[..., j, :, :], pad))
        return jnp.stack(out, axis=-3)

    br0, br1 = pairs(jnp.swapaxes(bz_re, -1, -2))
    bi0, bi1 = pairs(jnp.swapaxes(bz_im, -1, -2))
    zb = jnp.zeros_like(br0)
    bp = jnp.concatenate([jnp.concatenate([br0, zb, bi0, zb], axis=-1),
                          jnp.concatenate([zb, br1, zb, bi1], axis=-1)], axis=-2)
    bp = embed(bp, -2).astype(BF16)

    cr0, cr1 = pairs(jnp.swapaxes(c_re, -1, -2))
    ci0, ci1 = pairs(jnp.swapaxes(c_im, -1, -2))
    zc = jnp.zeros_like(cr0)
    cp = jnp.concatenate([jnp.concatenate([cr0, zc], axis=-1), jnp.concatenate([zc, cr1], axis=-1),
                          jnp.concatenate([-ci0, zc], axis=-1), jnp.concatenate([zc, -ci1], axis=-1)],
                         axis=-2)
    cp = embed(cp, -1).astype(BF16)

    def pack_l(t):
        return t.reshape(2, nblk, S5_PAIRS, LANE)

    lam = jnp.stack([pack_l(lam_re), pack_l(lam_im)], axis=-2)
    return bp, cp, lam


def _s5_pack_state(h0):
    b, _, g, _, _ = h0.shape
    nblk = g // (2 * S5_PAIRS)
    t = h0.reshape(b, 2, nblk, S5_PAIRS, 2, S5_STATE, 2)
    t = jnp.transpose(t, (1, 2, 3, 0, 6, 4, 5))
    return t.reshape(2, nblk, S5_PAIRS, b, 2 * LANE)


def _s5_unpack_state(fin, b):
    nblk = fin.shape[1]
    t = fin.reshape(2, nblk, S5_PAIRS, b, 2, 2, S5_STATE)
    t = jnp.transpose(t, (3, 0, 1, 2, 5, 6, 4))
    return t.reshape(b, 2, nblk * S5_PAIRS * 2, S5_STATE, 2)


def _cmul(ar, ai, br, bi):
    return ar * br - ai * bi, ar * bi + ai * br


def _tm_pitch(L):
    return L + SUBLANE


def _gather_tm(src_ref, dst_ref, pad_ref, L, B):
    pitch = _tm_pitch(L)
    for r in range(B):
        pad_ref[r * pitch:r * pitch + L, :] = src_ref[r * L:(r + 1) * L, :]

    def body(t, c):
        dst_ref[t] = pad_ref[pl.ds(t, B, stride=pitch), :]
        return c
    lax.fori_loop(0, L, body, 0, unroll=8)


def _scatter_tm(src_ref, pad_ref, L, B):
    pitch = _tm_pitch(L)

    def body(t, c):
        pad_ref[pl.ds(t, B, stride=pitch), :] = src_ref[t]
        return c
    lax.fori_loop(0, L, body, 0, unroll=8)


def _s5_kernel(*refs, L, B, nseg):
    if nseg > 1:
        (u_ref, bp_ref, cp_ref, lam_ref, d_ref, h0_ref, _, y_ref,
         xs_ref, yacc_ref, utm_ref, pad_ref, fl_ref, hp_ref) = refs
    else:
        u_ref, bp_ref, cp_ref, lam_ref, d_ref, y_ref, fin_ref, xs_ref, yacc_ref, utm_ref, pad_ref = refs
    nb = B // nseg
    _gather_tm(u_ref, utm_ref, pad_ref, L, B)
    u2 = utm_ref[...].reshape(L * B, LANE)
    u2b = u2.astype(BF16)
    yacc_ref[...] = jnp.zeros((L * B, LANE), F32)

    for half in range(S5_PAIRS // 2):
        chains = [(d, jj) for d in range(2) for jj in range(2)]
        for d, jj in chains:
            x = jnp.dot(u2b, bp_ref[d, 0, 2 * half + jj], preferred_element_type=F32)
            xs_ref[d, jj] = x.reshape(L, B, 2 * LANE)
        lam = {}
        for d, jj in chains:
            lr = jnp.broadcast_to(lam_ref[d, 0, 2 * half + jj, 0:1, :], (B, LANE))
            li = jnp.broadcast_to(lam_ref[d, 0, 2 * half + jj, 1:2, :], (B, LANE))
            lam[(d, jj)] = (lr, li)

        def load_x(i):
            out = []
            for d, jj in chains:
                t = i if d == 0 else L - 1 - i
                out += [xs_ref[d, jj, t, :, 0:LANE], xs_ref[d, jj, t, :, LANE:2 * LANE]]
            return out

        nc = 2 * len(chains)

        def step(i, carry):
            nxt = load_x(jnp.minimum(i + 1, L - 1))
            out = []
            for n, (d, jj) in enumerate(chains):
                t = i if d == 0 else L - 1 - i
                hr, hi = carry[2 * n], carry[2 * n + 1]
                lr, li = lam[(d, jj)]
                pr, pi = _cmul(lr, li, hr, hi)
                nr = pr + carry[nc + 2 * n]
                ni = pi + carry[nc + 2 * n + 1]
                xs_ref[d, jj, t, :, 0:LANE] = nr
                xs_ref[d, jj, t, :, LANE:2 * LANE] = ni
                out += [nr, ni]
            return tuple(out + nxt)

        zero = jnp.zeros((B, LANE), F32)
        carry = lax.fori_loop(0, L, step, (zero,) * nc + tuple(load_x(0)), unroll=4)

        if nseg == 1:
            for n, (d, jj) in enumerate(chains):
                fin_ref[d, 0, 2 * half + jj, :, 0:LANE] = carry[2 * n]
                fin_ref[d, 0, 2 * half + jj, :, LANE:2 * LANE] = carry[2 * n + 1]
        else:
            hp = {}
            for n, (d, jj) in enumerate(chains):
                plr = lam_ref[d, 0, 2 * half + jj, 0:1, :]
                pli = lam_ref[d, 0, 2 * half + jj, 1:2, :]
                for _ in range(int(math.log2(L))):
                    plr, pli = _cmul(plr, pli, plr, pli)
                fl_ref[0] = carry[2 * n]
                fl_ref[1] = carry[2 * n + 1]
                order = list(range(nseg)) if d == 0 else list(range(nseg - 1, -1, -1))
                for bb in range(nb):
                    cr = h0_ref[d, 0, 2 * half + jj, bb:bb + 1, 0:LANE]
                    ci = h0_ref[d, 0, 2 * half + jj, bb:bb + 1, LANE:2 * LANE]
                    for k, s in enumerate(order):
                        if k > 0:
                            rp = bb * nseg + order[k - 1]
                            mr, mi = _cmul(plr, pli, cr, ci)
                            cr = fl_ref[0, rp:rp + 1, :] + mr
                            ci = fl_ref[1, rp:rp + 1, :] + mi
                        r = bb * nseg + s
                        hp_ref[n, 0, r:r + 1, :] = cr
                        hp_ref[n, 1, r:r + 1, :] = ci
                hp[(d, jj)] = (hp_ref[n, 0], hp_ref[n, 1])

            def cstep(i, cw):
                nxt = load_x(jnp.minimum(i + 1, L - 1))
                out = []
                for n, (d, jj) in enumerate(chains):
                    t = i if d == 0 else L - 1 - i
                    lr, li = lam[(d, jj)]
                    nr, ni = _cmul(lr, li, cw[2 * n], cw[2 * n + 1])
                    xs_ref[d, jj, t, :, 0:LANE] = cw[nc + 2 * n] + nr
                    xs_ref[d, jj, t, :, LANE:2 * LANE] = cw[nc + 2 * n + 1] + ni
                    out += [nr, ni]
                return tuple(out + nxt)

            cw0 = []
            for d, jj in chains:
                cw0 += list(hp[(d, jj)])
            lax.fori_loop(0, L, cstep, tuple(cw0 + load_x(0)), unroll=4)

        for jj in range(2):
            hs = (xs_ref[0, jj] + xs_ref[1, jj]).reshape(L * B, 2 * LANE).astype(BF16)
            yacc_ref[...] += jnp.dot(hs, cp_ref[0, 2 * half + jj], preferred_element_type=F32)

    y = u2 * d_ref[...] + yacc_ref[...]
    utm_ref[...] = jax.nn.gelu(y).reshape(L, B, LANE)
    _scatter_tm(utm_ref, pad_ref, L, B)
    pitch = _tm_pitch(L)
    for r in range(B):
        y_ref[r * L:(r + 1) * L, :] = pad_ref[r * pitch:r * pitch + L, :]


def _s5_call(src, col0, bp, cp, lam, d_skip, h0, y_prev, *, L, B, nseg, rblk, n_rows):
    w = d_skip.shape[-1]
    nblk = w // LANE
    cblk0 = col0 // LANE
    in_specs = [
        pl.BlockSpec((L * B, LANE), lambda k: (rblk, cblk0 + k)),
        pl.BlockSpec((2, 1, S5_PAIRS, LANE, 2 * LANE), lambda k: (0, k, 0, 0, 0)),
        pl.BlockSpec((1, S5_PAIRS, 2 * LANE, LANE), lambda k: (k, 0, 0, 0)),
        pl.BlockSpec((2, 1, S5_PAIRS, 2, LANE), lambda k: (0, k, 0, 0, 0)),
        pl.BlockSpec((1, LANE), lambda k: (0, k)),
    ]
    args = [src, bp, cp, lam, d_skip.reshape(1, w)]
    y_shape = jax.ShapeDtypeStruct((n_rows, w), F32)
    y_spec = pl.BlockSpec((L * B, LANE), lambda k: (rblk, k))
    scratch = [pltpu.VMEM((2, 2, L, B, 2 * LANE), F32), pltpu.VMEM((L * B, LANE), F32),
               pltpu.VMEM((L, B, LANE), F32), pltpu.VMEM((B * _tm_pitch(L), LANE), F32)]
    aliases = {}
    if nseg > 1:
        nb = B // nseg
        in_specs += [pl.BlockSpec((2, 1, S5_PAIRS, nb, 2 * LANE), lambda k: (0, k, 0, 0, 0)),
                     pl.BlockSpec(memory_space=pl.ANY)]
        args += [h0, y_prev]
        aliases = {len(args) - 1: 0}
        out_shape, out_specs = y_shape, y_spec
        scratch += [pltpu.VMEM((2, B, LANE), F32), pltpu.VMEM((4, 2, B, LANE), F32)]
    else:
        out_shape = [y_shape, jax.ShapeDtypeStruct((2, nblk, S5_PAIRS, B, 2 * LANE), F32)]
        out_specs = [y_spec, pl.BlockSpec((2, 1, S5_PAIRS, B, 2 * LANE), lambda k: (0, k, 0, 0, 0))]
    return pl.pallas_call(
        functools.partial(_s5_kernel, L=L, B=B, nseg=nseg),
        out_shape=out_shape,
        grid=(nblk,),
        in_specs=in_specs,
        out_specs=out_specs,
        scratch_shapes=scratch,
        input_output_aliases=aliases,
        compiler_params=_cparams("arbitrary"),
        name="s5_scan_seg" if nseg > 1 else "s5_scan",
    )(*args)


def _glu_kernel(yrow_ref, w_ref, b_ref, g_ref, o_ref):
    w = w_ref[...].astype(BF16)
    z = jnp.dot(yrow_ref[...].astype(BF16), w, preferred_element_type=F32) + b_ref[...]
    col0 = pl.multiple_of(pl.program_id(1) * TN, TN)
    ytile = yrow_ref[:, pl.ds(col0, TN)]
    o_ref[...] = (ytile * _sigmoid(z) * _silu(g_ref[...])).astype(BF16)


def _glu_call(y, w_glu, b_glu3, proj, l):
    goff = C_GA // TN
    return pl.pallas_call(
        _glu_kernel,
        out_shape=jax.ShapeDtypeStruct((N_TOK, W_BR), BF16),
        grid=(N_TOK // TM, W_BR // TN),
        in_specs=[
            pl.BlockSpec((TM, W_BR), lambda i, j: (i, 0)),
            pl.BlockSpec((None, W_BR, TN), lambda i, j: (l, 0, j)),
            pl.BlockSpec((None, 1, TN), lambda i, j: (l, 0, j)),
            pl.BlockSpec((TM, TN), lambda i, j: (i, goff + j)),
        ],
        out_specs=pl.BlockSpec((TM, TN), lambda i, j: (i, j)),
        compiler_params=_cparams("arbitrary", "arbitrary"),
        name="s5_glu",
    )(y, w_glu, b_glu3, proj)


def _expm1(x):
    t = jnp.tanh(0.5 * x)
    return 2.0 * t / (1.0 - t)


def _lru_kernel(*refs, L, B, nseg, period):
    if nseg > 1:
        (x_ref, g_ref, cw_ref, cb_ref, wa_ref, ba_ref, wx_ref, bx_ref, sp_ref, h0_ref, _,
         y_ref, a_ref, b_ref, ytok_ref, fl_ref, hp_ref) = refs
    else:
        (x_ref, g_ref, cw_ref, cb_ref, wa_ref, ba_ref, wx_ref, bx_ref, sp_ref,
         y_ref, fin_ref, a_ref, b_ref, ytok_ref) = refs
    nb = B // nseg
    _gather_tm(x_ref, a_ref.at[0], ytok_ref, L, B)
    x = a_ref[0]
    pos = lax.broadcasted_iota(jnp.int32, (L, B, LANE), 0) % period
    xc = jnp.broadcast_to(cb_ref[...].reshape(1, 1, LANE), (L, B, LANE))
    for k in range(CONV_W):
        off = k - CONV_W // 2
        if off < 0:
            xs = jnp.concatenate([jnp.zeros((-off, B, LANE), F32), x[:L + off]], axis=0)
        elif off > 0:
            xs = jnp.concatenate([x[off:], jnp.zeros((off, B, LANE), F32)], axis=0)
        else:
            xs = x
        if period < L:
            xs = jnp.where((pos + off >= 0) & (pos + off < period), xs, 0.0)
        xc = xc + cw_ref[k:k + 1, :].reshape(1, 1, LANE) * xs
    xc2 = xc.reshape(L * B, LANE)
    xcb = xc2.astype(BF16)
    for d in range(2):
        r = _sigmoid(jnp.dot(xcb, wa_ref[d, 0].astype(BF16), preferred_element_type=F32)
                     + ba_ref[d:d + 1, :])
        ig = _sigmoid(jnp.dot(xcb, wx_ref[d, 0].astype(BF16), preferred_element_type=F32)
                      + bx_ref[d:d + 1, :])
        log_a = (-LRU_C) * r * sp_ref[d:d + 1, :]
        a_ref[d] = jnp.exp(log_a).reshape(L, B, LANE)
        m2 = -_expm1(2.0 * log_a)
        mult = jnp.where(m2 > 0.0, m2 * lax.rsqrt(m2), 0.0)
        b_ref[d] = (mult * (ig * xc2)).reshape(L, B, LANE)

    def load_ab(i):
        out = []
        for d in range(2):
            t = i if d == 0 else L - 1 - i
            out += [a_ref[d, t], b_ref[d, t]]
        return out

    def step(i, carry):
        nxt = load_ab(jnp.minimum(i + 1, L - 1))
        out = []
        for d in range(2):
            t = i if d == 0 else L - 1 - i
            a = carry[4 + 2 * d]
            h = a * carry[2 * d] + carry[4 + 2 * d + 1]
            b_ref[d, t] = h
            if nseg > 1:
                p = a * carry[2 * d + 1]
                a_ref[d, t] = p
            else:
                p = carry[2 * d + 1]
            out += [h, p]
        return tuple(out + nxt)

    zero = jnp.zeros((B, LANE), F32)
    one = jnp.ones((B, LANE), F32)
    carry = lax.fori_loop(0, L, step, (zero, one, zero, one) + tuple(load_ab(0)), unroll=8)

    if nseg == 1:
        fin_ref[0] = carry[0]
        fin_ref[1] = carry[2]
    else:
        hps = []
        for d in range(2):
            fl_ref[0] = carry[2 * d]
            fl_ref[1] = carry[2 * d + 1]
            order = list(range(nseg)) if d == 0 else list(range(nseg - 1, -1, -1))
            for bb in range(nb):
                c = h0_ref[d, bb:bb + 1, :]
                for k, s in enumerate(order):
                    if k > 0:
                        rp = bb * nseg + order[k - 1]
                        c = fl_ref[0, rp:rp + 1, :] + fl_ref[1, rp:rp + 1, :] * c
                    r = bb * nseg + s
                    hp_ref[d, r:r + 1, :] = c
            hps.append(hp_ref[d])

        def cstep(i, c):
            nxt = load_ab(jnp.minimum(i + 1, L - 1))
            for d in range(2):
                t = i if d == 0 else L - 1 - i
                b_ref[d, t] = c[2 * d + 1] + c[2 * d] * hps[d]
            return tuple(nxt)

        lax.fori_loop(0, L, cstep, tuple(load_ab(0)), unroll=4)

    a_ref[0] = b_ref[0] + b_ref[1]
    _scatter_tm(a_ref.at[0], ytok_ref, L, B)
    pitch = _tm_pitch(L)
    for r in range(B):
        rows = slice(r * L, (r + 1) * L)
        y_ref[rows, :] = (ytok_ref[r * pitch:r * pitch + L, :] * _silu(g_ref[rows, :])).astype(BF16)


def _lru_call(src, col_x, col_g, conv_w, conv_b, w_a, b_a, w_x, b_x, sp, h0, y_prev,
              *, L, B, nseg, period, rblk, n_rows):
    w = conv_b.shape[-1]
    nblk = w // LANE

    def rows(col0):
        return pl.BlockSpec((L * B, LANE), lambda k: (rblk, col0 // LANE + k))

    y_spec = pl.BlockSpec((L * B, LANE), lambda k: (rblk, k))
    in_specs = [
        rows(col_x), rows(col_g),
        pl.BlockSpec((CONV_W, LANE), lambda k: (0, k)),
        pl.BlockSpec((1, LANE), lambda k: (0, k)),
        pl.BlockSpec((2, 1, LANE, LANE), lambda k: (0, k, 0, 0)),
        pl.BlockSpec((2, LANE), lambda k: (0, k)),
        pl.BlockSpec((2, 1, LANE, LANE), lambda k: (0, k, 0, 0)),
        pl.BlockSpec((2, LANE), lambda k: (0, k)),
        pl.BlockSpec((2, LANE), lambda k: (0, k)),
    ]
    args = [src, src, conv_w, conv_b.reshape(1, w), w_a, b_a, w_x, b_x, sp]
    y_shape = jax.ShapeDtypeStruct((n_rows, w), BF16)
    scratch = [pltpu.VMEM((2, L, B, LANE), F32), pltpu.VMEM((2, L, B, LANE), F32),
               pltpu.VMEM((B * _tm_pitch(L), LANE), F32)]
    aliases = {}
    if nseg > 1:
        nb = B // nseg
        in_specs += [pl.BlockSpec((2, nb, LANE), lambda k: (0, 0, k)), pl.BlockSpec(memory_space=pl.ANY)]
        args += [h0, y_prev]
        aliases = {len(args) - 1: 0}
        out_shape, out_specs = y_shape, y_spec
        scratch += [pltpu.VMEM((2, B, LANE), F32), pltpu.VMEM((2, B, LANE), F32)]
    else:
        out_shape = [y_shape, jax.ShapeDtypeStruct((2, B, w), F32)]
        out_specs = [y_spec, pl.BlockSpec((2, B, LANE), lambda k: (0, 0, k))]
    return pl.pallas_call(
        functools.partial(_lru_kernel, L=L, B=B, nseg=nseg, period=period),
        out_shape=out_shape,
        grid=(nblk,),
        in_specs=in_specs,
        out_specs=out_specs,
        scratch_shapes=scratch,
        input_output_aliases=aliases,
        compiler_params=_cparams("arbitrary"),
        name="rglru_seg" if nseg > 1 else "rglru",
    )(*args)


def _nt_dot(a, b):
    return lax.dot_general(a, b, (((1,), (1,)), ((), ())), preferred_element_type=F32)


def _hgrn_kernel(*refs, L, has_h0, n_aliased):
    q_ref, ff_ref, fb_ref, v_ref, g_ref, lb_ref, nw_ref = refs[:7]
    h0_ref = refs[7] if has_h0 else None
    y_ref, fin_ref, of_ref = refs[7 + has_h0 + n_aliased:]
    rb_rows = HG_RB
    nrb = L // rb_rows
    nsub = HG_ST // HG_CHUNK
    nch = rb_rows // HG_ST
    nscb = rb_rows // HG_SCB
    row = lax.broadcasted_iota(jnp.int32, (HG_SCB, HG_SCB), 0)
    col = lax.broadcasted_iota(jnp.int32, (HG_SCB, HG_SCB), 1)
    sub_dist = jnp.where((row // HG_ST) == (col // HG_ST), row // HG_CHUNK - col // HG_CHUNK, 2 * nsub)
    ridx = lax.broadcasted_iota(jnp.int32, (rb_rows, HG_DK), 0)
    pos = ridx % HG_CHUNK
    sic = (ridx // HG_CHUNK) % nsub
    chunk_id = ridx // HG_ST

    def sub_bcast(x, idx):
        x3 = x.reshape(rb_rows // HG_CHUNK, HG_CHUNK, HG_DK)
        return jnp.broadcast_to(x3[:, idx:idx + 1, :], x3.shape).reshape(rb_rows, HG_DK)

    def chunk_bcast(x, idx):
        x3 = x.reshape(nch, HG_ST, HG_DK)
        return jnp.broadcast_to(x3[:, idx:idx + 1, :], x3.shape).reshape(rb_rows, HG_DK)

    shared = {}

    def load_qv(rows, cs, key):
        if key is not None and key in shared:
            return shared[key]
        v = v_ref[rows, cs]
        out = (_silu(q_ref[rows, cs]), v.astype(BF16), v.T.astype(BF16))
        if key is not None:
            shared[key] = out
        return out

    def run_block(rb, st, d, hh):
        cs = slice(hh * HG_DK, (hh + 1) * HG_DK)
        if isinstance(rb, int):
            rows = slice(rb * rb_rows, (rb + 1) * rb_rows)
        else:
            rows = pl.ds(pl.multiple_of(rb * rb_rows, rb_rows), rb_rows)
        qf, vb, vt = load_qv(rows, cs, (hh, rb) if isinstance(rb, int) else None)
        f = (ff_ref if d == 0 else fb_ref)[rows, cs]
        lbd = lb_ref[d:d + 1, cs]
        g = lbd + (1.0 - lbd) * jax.nn.sigmoid(f)
        kk = 1.0 - g
        b = jnp.log2(g)
        sgn = 1 if d == 0 else -1

        def shift(x, n):
            return pltpu.roll(x, n if d == 0 else rb_rows - n, 0)

        s = 1
        while s < HG_CHUNK:
            keep = (pos >= s) if d == 0 else (pos <= HG_CHUNK - 1 - s)
            b = b + jnp.where(keep, shift(b, s), 0.0)
            s *= 2
        tot = sub_bcast(b, HG_CHUNK - 1 if d == 0 else 0)
        prev = [shift(tot, HG_CHUNK * n) for n in range(1, nsub)]
        seen = sic if d == 0 else nsub - 1 - sic
        base = sum(jnp.where(seen >= n, prev[n - 1], 0.0) for n in range(1, nsub))
        ctot = chunk_bcast(b + base, HG_ST - 1 if d == 0 else 0)

        ed = jnp.exp2(b - sub_bcast(b, HG_CHUNK // 2))
        qt = (qf * ed).astype(BF16)
        kt = (kk / ed).astype(BF16)
        qe = qf * jnp.exp2(b)
        kd = kk * jnp.exp2(tot - b)
        lhs = [qe.astype(BF16)]
        acc = None
        for n in range(1, nsub - 1):
            acc = prev[n - 1] if acc is None else acc + prev[n - 1]
            lhs.append((qe * jnp.exp2(acc)).astype(BF16))
        kdb = kd.astype(BF16)
        causal = (col <= row) if d == 0 else (col >= row)
        o_parts = []
        for hb in range(nscb):
            rs = slice(hb * HG_SCB, (hb + 1) * HG_SCB)
            sc = jnp.where((sub_dist == 0) & causal, _nt_dot(qt[rs], kt[rs]), 0.0)
            scn = _nt_dot(jnp.concatenate([x[rs] for x in lhs], axis=0), kdb[rs])
            for n in range(1, nsub):
                sc = jnp.where(sub_dist == sgn * n, scn[(n - 1) * HG_SCB:n * HG_SCB], sc)
            o_parts.append(jnp.dot(sc.astype(BF16), vb[rs], preferred_element_type=F32))
        o = jnp.concatenate(o_parts, axis=0)

        qe_st = (qe * jnp.exp2(base)).astype(BF16)
        kd_st = kd * jnp.exp2(ctot - base - tot)
        uts = []
        for c in range(0, nch, 2):
            rhs = jnp.concatenate([jnp.where(chunk_id == c, kd_st, 0.0),
                                   jnp.where(chunk_id == c + 1, kd_st, 0.0)], axis=1).astype(BF16)
            ut2 = jnp.dot(vt, rhs, preferred_element_type=F32)
            uts += [ut2[:, :HG_DK], ut2[:, HG_DK:]]
        order = range(nch) if d == 0 else range(nch - 1, -1, -1)
        entering = [None] * nch
        for c in order:
            entering[c] = st
            tot_row = c * HG_ST + (HG_ST - 1 if d == 0 else 0)
            st = st * jnp.exp2(ctot[tot_row:tot_row + 1, :]) + uts[c]
        outs = []
        for c in range(nch):
            lo = c * HG_ST
            oi = _nt_dot(qe_st[lo:lo + HG_ST], entering[c].astype(BF16))
            outs.append(o[lo:lo + HG_ST] + oi)
        return jnp.concatenate(outs, axis=0), st

    streams = [(hh, d) for hh in range(HG_HPB) for d in range(2)]

    def body(i, sts):
        new = []
        for (hh, d), st in zip(streams, sts):
            rb = i if d == 0 else nrb - 1 - i
            oblk, st = run_block(rb, st, d, hh)
            cs = slice(hh * HG_DK, (hh + 1) * HG_DK)
            if isinstance(rb, int):
                of_ref[d, rb * rb_rows:(rb + 1) * rb_rows, cs] = oblk
            else:
                of_ref[d, pl.ds(pl.multiple_of(rb * rb_rows, rb_rows), rb_rows), cs] = oblk
            new.append(st)
        return tuple(new)

    if has_h0:
        sts = tuple(h0_ref[0, d, hh].T for hh, d in streams)
    else:
        sts = tuple(jnp.zeros((HG_DK, HG_DK), F32) for _ in streams)
    sts = body(0, sts) if nrb == 1 else lax.fori_loop(0, nrb, body, sts)
    for (hh, d), st in zip(streams, sts):
        fin_ref[0, d, hh] = st.T

    for hh in range(HG_HPB):
        cs = slice(hh * HG_DK, (hh + 1) * HG_DK)
        o = of_ref[0, :, cs] + of_ref[1, :, cs]
        o = o * lax.rsqrt(jnp.mean(o * o, axis=-1, keepdims=True) + RMS_EPS)
        y_ref[:, cs] = (o * nw_ref[:, cs] * _silu(g_ref[:, cs])).astype(BF16)


def _hgrn_call(proj, lb, norm_w, h0, y_prev, fin_prev, *, L, nb, row0, n_rows, layer,
               cols=(C_QC, C_FF, C_FB, C_IC, C_GC)):
    w = lb.shape[-1]
    nh = w // HG_DK
    rblk0 = row0 // L
    hw = HG_HPB * HG_DK

    def col(off):
        return pl.BlockSpec((L, hw), lambda b, h: (rblk0 + b, off // hw + h))

    in_specs = [col(c) for c in cols] + [
                pl.BlockSpec((2, hw), lambda b, h: (0, h)),
                pl.BlockSpec((1, hw), lambda b, h: (0, h))]
    args = [proj] * 5 + [lb, norm_w.reshape(1, w)]
    if h0 is not None:
        in_specs.append(pl.BlockSpec((1, 2, HG_HPB, HG_DK, HG_DK), lambda b, h: (b, 0, h, 0, 0)))
        args.append(h0)
    aliases = {}
    for out_idx, prev in enumerate((y_prev, fin_prev)):
        if prev is not None:
            in_specs.append(pl.BlockSpec(memory_space=pl.ANY))
            args.append(prev)
            aliases[len(args) - 1] = out_idx
    st_spec = pl.BlockSpec((1, None, 2, HG_HPB, HG_DK, HG_DK), lambda b, h: (b, layer, 0, h, 0, 0))
    return pl.pallas_call(
        functools.partial(_hgrn_kernel, L=L, has_h0=h0 is not None, n_aliased=len(aliases)),
        out_shape=[jax.ShapeDtypeStruct((n_rows, w), BF16),
                   jax.ShapeDtypeStruct((nb, DEPTH, 2, nh, HG_DK, HG_DK), F32)],
        grid=(nb, nh // HG_HPB),
        in_specs=in_specs,
        out_specs=[pl.BlockSpec((L, hw), lambda b, h: (rblk0 + b, h)), st_spec],
        scratch_shapes=[pltpu.VMEM((2, L, hw), F32)],
        input_output_aliases=aliases,
        compiler_params=_cparams("arbitrary", "arbitrary"),
        name="hgrn2",
    )(*args)


def _merge_kernel(ya_ref, yb_ref, yc_ref, wa_ref, wb_ref, wc_ref, ma_ref, mb_ref, mc_ref, o_ref):
    acc = None
    for y_ref, w_ref, m_ref in ((ya_ref, wa_ref, ma_ref), (yb_ref, wb_ref, mb_ref), (yc_ref, wc_ref, mc_ref)):
        t = jnp.dot(y_ref[...], w_ref[...].astype(BF16), preferred_element_type=F32)
        t = _sigmoid(m_ref[...]) * t
        acc = t if acc is None else acc + t
    o_ref[...] = acc.astype(BF16)


def _merge_call(ya, yb, yc, w_br, proj, l):
    tm, tn = TM_MERGE, TN_MERGE
    moff = C_MA // tn
    mstep = D_MODEL // tn
    lhs = pl.BlockSpec((tm, W_BR), lambda i, j: (i, 0))

    def wspec(br):
        return pl.BlockSpec((None, None, W_BR, tn), lambda i, j: (l, br, 0, j))

    def mspec(br):
        return pl.BlockSpec((tm, tn), lambda i, j: (i, moff + br * mstep + j))

    return pl.pallas_call(
        _merge_kernel,
        out_shape=jax.ShapeDtypeStruct((N_TOK, D_MODEL), BF16),
        grid=(N_TOK // tm, D_MODEL // tn),
        in_specs=[lhs, lhs, lhs, wspec(0), wspec(1), wspec(2), mspec(0), mspec(1), mspec(2)],
        out_specs=pl.BlockSpec((tm, tn), lambda i, j: (i, j)),
        compiler_params=_cparams("arbitrary", "arbitrary"),
        name="branch_merge",
    )(ya, yb, yc, w_br, w_br, w_br, proj, proj, proj)


def _outproj_kernel(m_ref, w_ref, b_ref, xc_ref, xl_ref, gate_ref, o_ref):
    out = jnp.dot(m_ref[...], w_ref[...].astype(BF16), preferred_element_type=F32) + b_ref[...]
    gated = gate_ref[0] * out
    is_ctx = pl.program_id(0) < N_CTX // TM

    @pl.when(is_ctx)
    def _():
        o_ref[...] = DN_ALPHA * xc_ref[...] + gated

    @pl.when(jnp.logical_not(is_ctx))
    def _():
        o_ref[...] = DN_ALPHA * xl_ref[...] + gated


def _outproj_call(merged, w_out, b_out3, x_ctx, x_lat, mods3, l):
    goff = 2 * D_MODEL // TN
    ctx_map, lat_map = _split_rows(TM, 1)
    return pl.pallas_call(
        _outproj_kernel,
        out_shape=jax.ShapeDtypeStruct((N_TOK, D_MODEL), F32),
        grid=(N_TOK // TM, D_MODEL // TN),
        in_specs=[
            pl.BlockSpec((TM, D_MODEL), lambda i, j: (i, 0)),
            pl.BlockSpec((None, D_MODEL, TN), lambda i, j: (l, 0, j)),
            pl.BlockSpec((None, 1, TN), lambda i, j: (l, 0, j)),
            pl.BlockSpec((TM, TN), ctx_map),
            pl.BlockSpec((TM, TN), lat_map),
            pl.BlockSpec((1, 1, TN), lambda i, j: (_mod_row_mm(i), 0, goff + j)),
        ],
        out_specs=pl.BlockSpec((TM, TN), lambda i, j: (i, j)),
        compiler_params=_cparams("arbitrary", "arbitrary"),
        name="out_proj",
    )(merged, w_out, b_out3, x_ctx, x_lat, mods3)


def _ln_affine_kernel(r_ref, g_ref, b_ref, o_ref):
    r = r_ref[...]
    mu = jnp.mean(r, axis=-1, keepdims=True)
    rc = r - mu
    var = jnp.mean(rc * rc, axis=-1, keepdims=True)
    o_ref[...] = rc * lax.rsqrt(var + LN_EPS) * g_ref[...] + b_ref[...]


def _ln_affine_call(r, g, b, l, row0=0, n_rows=N_TOK):
    rblk0 = row0 // TM_LN
    return pl.pallas_call(
        _ln_affine_kernel,
        out_shape=jax.ShapeDtypeStruct((n_rows, D_MODEL), F32),
        grid=(n_rows // TM_LN,),
        in_specs=[
            pl.BlockSpec((TM_LN, D_MODEL), lambda i: (rblk0 + i, 0)),
            pl.BlockSpec((None, 1, D_MODEL), lambda i: (l, 0, 0)),
            pl.BlockSpec((None, 1, D_MODEL), lambda i: (l, 0, 0)),
        ],
        out_specs=pl.BlockSpec((TM_LN, D_MODEL), lambda i: (i, 0)),
        compiler_params=_cparams("arbitrary"),
        name="post_ln",
    )(r, g, b)


def kernel(x_prompt, x_sample, state_s5, state_lru, state_hgrn, c, c_ctx, w_ada, b_ada, w_in, b_in,
           s5_a_re, s5_a_im, s5_log_dt, s5_b_re, s5_b_im, s5_c_re, s5_c_im, s5_d, s5_w_glu, s5_b_glu,
           lru_conv_w, lru_conv_b, lru_w_a, lru_b_a, lru_w_x, lru_b_x, lru_lambda, hg_lb, hg_norm_w,
           w_br, w_out, b_out, ln_g, ln_b):
    lb_soft = jax.nn.softmax(hg_lb.astype(F32), axis=0)
    lb_all = jnp.cumsum(lb_soft, axis=0) - lb_soft[0]
    softplus_neg_lam = jax.nn.softplus(-lru_lambda.astype(F32))

    cvec = jnp.zeros((SUBLANE, D_MODEL), F32)
    cvec = cvec.at[0].set(c_ctx).at[1:1 + DEC_BATCH].set(c)
    mods = _mods_call(cvec, w_ada, b_ada)

    b_in3 = b_in.reshape(DEPTH, 1, IN_COLS)
    b_glu3 = s5_b_glu.reshape(DEPTH, 1, W_BR)
    b_out3 = b_out.reshape(DEPTH, 1, D_MODEL)
    ln_g3 = ln_g.reshape(DEPTH, 1, D_MODEL)
    ln_b3 = ln_b.reshape(DEPTH, 1, D_MODEL)

    x_ctx = x_prompt.reshape(N_CTX, D_MODEL)
    x_lat = x_sample.reshape(N_LAT, D_MODEL)
    st_s5, st_lru, st_hg = [], [], None
    for l in range(DEPTH):
        mods3 = mods[l, :1 + DEC_BATCH].reshape(1 + DEC_BATCH, 1, 3 * D_MODEL)
        h = _ln_mod_call(x_ctx, x_lat, mods3)
        proj = _inproj_call(h, w_in, b_in3, l)

        lam_re, lam_im, z_re, z_im = _s5_disc_call(s5_a_re[l], s5_a_im[l], s5_log_dt[l])
        bp, cp, lam = _s5_pack(lam_re, lam_im, z_re, z_im, s5_b_re[l], s5_b_im[l], s5_c_re[l], s5_c_im[l])
        ctx_rows = dict(L=SEQ, B=BATCH, nseg=1, rblk=0, n_rows=N_TOK)
        lat_rows = dict(L=LAT_LEN, B=LAT_ROWS, nseg=LAT_SEG, rblk=N_CTX // N_LAT, n_rows=N_TOK)
        y_pre, fin_s5 = _s5_call(proj, C_UA, bp, cp, lam, s5_d[l], None, None, **ctx_rows)
        y_pre = _s5_call(proj, C_UA, bp, cp, lam, s5_d[l], _s5_pack_state(state_s5[:, l]), y_pre, **lat_rows)
        y_a = _glu_call(y_pre, s5_w_glu, b_glu3, proj, l)
        st_s5.append(_s5_unpack_state(fin_s5, BATCH))

        lru_args = (lru_conv_w[l], lru_conv_b[l], lru_w_a[l], lru_b_a[l], lru_w_x[l], lru_b_x[l],
                    softplus_neg_lam[l])
        y_b, fin_lru = _lru_call(proj, C_XB, C_GB, *lru_args, None, None, period=SEQ, **ctx_rows)
        y_b = _lru_call(proj, C_XB, C_GB, *lru_args, jnp.transpose(state_lru[:, l], (1, 0, 2)), y_b,
                        period=GRID_W, **lat_rows)
        st_lru.append(jnp.transpose(fin_lru, (1, 0, 2)))

        y_c, st_hg = _hgrn_call(proj, lb_all[l], hg_norm_w[l], None, None, st_hg,
                                L=SEQ, nb=BATCH, row0=0, n_rows=N_TOK, layer=l)
        y_c, _ = _hgrn_call(proj, lb_all[l], hg_norm_w[l], state_hgrn[:, l], y_c, None,
                            L=DEC_SEQ, nb=DEC_BATCH, row0=N_CTX, n_rows=N_TOK, layer=0)

        merged = _merge_call(y_a, y_b, y_c, w_br, proj, l)
        r = _outproj_call(merged, w_out, b_out3, x_ctx, x_lat, mods3, l)
        x_ctx = _ln_affine_call(r, ln_g3, ln_b3, l, 0, N_CTX)
        x_lat = _ln_affine_call(r, ln_g3, ln_b3, l, N_CTX, N_LAT)

    y_prompt = x_ctx.reshape(BATCH, SEQ, D_MODEL)
    y_sample = x_lat.reshape(DEC_BATCH, DEC_SEQ, D_MODEL)
    new_state_s5 = jnp.stack(st_s5, axis=1)
    new_state_lru = jnp.stack(st_lru, axis=1)
    new_state_hgrn = st_hg
    return (y_prompt, y_sample, new_state_s5, new_state_lru, new_state_hgrn)
```
